```python
import math
import jax, jax.numpy as jnp
from jax import lax
import numpy as np

D_MODEL = 2048
BATCH = 2
SEQ = 16384
DEPTH = 2

HEAD_DIM = 128
N_BRANCHES = 4
CONV_GROUPS = 4
CONV_WIDTH = CONV_GROUPS * HEAD_DIM
CONV_K = 3
SGU_GROUPS = 4
SGU_GROUP_DIM = HEAD_DIM
SGU_WIDTH = SGU_GROUPS * SGU_GROUP_DIM
CHUNK = 128
DIFF_HEADS = 4
DIFF_SUB_DIM = HEAD_DIM // 2
DIFF_WIDTH = DIFF_HEADS * HEAD_DIM
DSA_HEADS = 4
DSA_WIDTH = DSA_HEADS * HEAD_DIM
IDX_HEADS = 8
IDX_DIM = 64
TOPK_MAX = 256
Q_BLOCK = 128
ROPE_THETA = 10000.0
D_FF = 7168
N_EXPERTS = 8
TOP_K_EXPERTS = 2
N_DENSE = (DEPTH + 1) // 2
N_MOE = DEPTH // 2
EPS = 1e-6

PROJ_SIZES = (CONV_WIDTH, CONV_WIDTH, CONV_WIDTH,
              SGU_WIDTH, SGU_WIDTH,
              DIFF_WIDTH, DIFF_WIDTH, DIFF_WIDTH,
              DSA_WIDTH, DSA_WIDTH, DSA_WIDTH,
              IDX_HEADS * IDX_DIM, IDX_DIM, IDX_HEADS,
              N_BRANCHES * D_MODEL)
PROJ_WIDTH = sum(PROJ_SIZES)

kernel_name = "hybrid_parallel_conv_sgu_diffattn_dsa_moe"


def rmsnorm(x, g):
    xf = x.astype(jnp.float32)
    xf = xf * lax.rsqrt(jnp.mean(xf * xf, axis=-1, keepdims=True) + EPS)
    return (xf * g.astype(jnp.float32)).astype(x.dtype)


def rope_tables(pos, dim):
    inv = ROPE_THETA ** (-jnp.arange(0, dim, 2, dtype=jnp.float32) / dim)
    ang = pos.astype(jnp.float32)[:, None] * inv[None, :]
    ang = jnp.concatenate([ang, ang], axis=-1)
    return jnp.cos(ang), jnp.sin(ang)


def apply_rope(x, cos, sin):
    extra = x.ndim - 3
    c = cos.reshape(cos.shape[0], *([1] * extra), cos.shape[1])
    s = sin.reshape(sin.shape[0], *([1] * extra), sin.shape[1])
    xf = x.astype(jnp.float32)
    x1, x2 = jnp.split(xf, 2, axis=-1)
    rot = jnp.concatenate([-x2, x1], axis=-1)
    return (xf * c + rot * s).astype(x.dtype)


def to_blocks(t, nb):
    return t.reshape(t.shape[0], nb, Q_BLOCK, *t.shape[2:]).swapaxes(0, 1)


def from_blocks(t):
    t = t.swapaxes(0, 1)
    return t.reshape(t.shape[0], t.shape[1] * t.shape[2], *t.shape[3:])


def short_conv_mixer(b_gate, c_gate, h, conv_w):
    z = c_gate * h
    S = z.shape[1]
    zp = jnp.pad(z, ((0, 0), (CONV_K - 1, 0), (0, 0)))
    conv = conv_w[0] * zp[:, 0:S]
    for j in range(1, CONV_K):
        conv = conv + conv_w[j] * zp[:, j:j + S]
    return b_gate * conv


def sgu_mixer(u, v, sgu_norm, sgu_w, sgu_b):
    B, S, _ = u.shape
    u = jax.nn.gelu(u)
    v = rmsnorm(jax.nn.gelu(v), sgu_norm)
    v = v.reshape(B, S // CHUNK, CHUNK, SGU_GROUPS, SGU_GROUP_DIM)
    mask = jnp.tril(jnp.ones((CHUNK, CHUNK), dtype=bool))
    w = jnp.where(mask[None], sgu_w, 0).astype(v.dtype)
    y = jnp.einsum('gts,bnsgc->bntgc', w, v) + sgu_b.T[:, :, None].astype(v.dtype)
    return u * y.reshape(B, S, SGU_WIDTH)


def diff_attention(q, k, v, lam, lam_init, subln, cos, sin):
    B, S = q.shape[:2]
    q = apply_rope(q, cos, sin)
    k = apply_rope(k, cos, sin)
    scale = DIFF_SUB_DIM ** -0.5
    nb = S // Q_BLOCK
    kpos = jnp.arange(S)

    def block(args):
        qi, i = args
        qpos = i * Q_BLOCK + jnp.arange(Q_BLOCK)
        s = jnp.einsum('bqhid,bkhid->bhiqk', qi, k).astype(jnp.float32) * scale
        s = jnp.where(kpos[None, :] <= qpos[:, None], s, -jnp.inf)
        p = jax.nn.softmax(s, axis=-1)
        a = (p[:, :, 0] - lam * p[:, :, 1]).astype(v.dtype)
        return jnp.einsum('bhqk,bkhd->bqhd', a, v)

    o = from_blocks(lax.map(block, (to_blocks(q, nb), jnp.arange(nb))))
    o = rmsnorm(o, subln) * (1.0 - lam_init)
    return o.reshape(B, S, DIFF_WIDTH)


def dsa_attention(q, k, v, iq, ik, iw, rope128, rope64):
    B, S = q.shape[:2]
    q = apply_rope(q, *rope128)
    k = apply_rope(k, *rope128)
    iq = apply_rope(iq, *rope64)
    ik = apply_rope(ik, *rope64)
    topk = min(TOPK_MAX, S // 4)
    nb = S // Q_BLOCK
    kpos = jnp.arange(S)
    idx_scale = IDX_DIM ** -0.5 * IDX_HEADS ** -0.5
    scale = HEAD_DIM ** -0.5

    def block(args):
        qi, iqi, iwi, i = args
        qpos = i * Q_BLOCK + jnp.arange(Q_BLOCK)
        rel = jax.nn.relu(jnp.einsum('bqhd,bkd->bqhk', iqi, ik).astype(jnp.float32))
        score = jnp.einsum('bqh,bqhk->bqk', iwi.astype(jnp.float32), rel) * idx_scale
        score = jnp.where(kpos[None, None, :] <= qpos[None, :, None], score, -jnp.inf)
        _, sel = lax.top_k(score, topk)
        valid = sel <= qpos[None, :, None]
        ksel = jax.vmap(lambda kb, ib: kb[ib])(k, sel)
        vsel = jax.vmap(lambda vb, ib: vb[ib])(v, sel)
        s = jnp.einsum('bqhd,bqjhd->bhqj', qi, ksel).astype(jnp.float32) * scale
        s = jnp.where(valid[:, None], s, -jnp.inf)
        p = jax.nn.softmax(s, axis=-1).astype(v.dtype)
        return jnp.einsum('bhqj,bqjhd->bqhd', p, vsel)

    o = lax.map(block, (to_blocks(q, nb), to_blocks(iq, nb), to_blocks(iw, nb), jnp.arange(nb)))
    return from_blocks(o).reshape(B, S, DSA_WIDTH)


def mixer_layer(h, layer, w_in, conv_w, sgu_norm, sgu_w, sgu_b, diff_lambda, diff_subln,
                w_branch, w_out, rope64, rope128):
    B, S, _ = h.shape
    proj = jnp.einsum('bsd,dn->bsn', h, w_in)
    splits = np.cumsum(PROJ_SIZES)[:-1].tolist()
    (cb, cc, ch, su, sv, dq, dk, dv, aq, ak, av, iq, ik, iw, gate) = jnp.split(proj, splits, axis=-1)

    y_a = short_conv_mixer(cb, cc, ch, conv_w)
    y_b = sgu_mixer(su, sv, sgu_norm, sgu_w, sgu_b)

    lam_init = 0.8 - 0.6 * math.exp(-0.3 * layer)
    lq1, lk1, lq2, lk2 = [diff_lambda[i].astype(jnp.float32) for i in range(4)]
    lam = jnp.exp(jnp.sum(lq1 * lk1)) - jnp.exp(jnp.sum(lq2 * lk2)) + lam_init
    y_c = diff_attention(dq.reshape(B, S, DIFF_HEADS, 2, DIFF_SUB_DIM),
                         dk.reshape(B, S, DIFF_HEADS, 2, DIFF_SUB_DIM),
                         dv.reshape(B, S, DIFF_HEADS, HEAD_DIM),
                         lam, lam_init, diff_subln, *rope64)

    y_d = dsa_attention(aq.reshape(B, S, DSA_HEADS, HEAD_DIM),
                        ak.reshape(B, S, DSA_HEADS, HEAD_DIM),
                        av.reshape(B, S, DSA_HEADS, HEAD_DIM),
                        iq.reshape(B, S, IDX_HEADS, IDX_DIM), ik, iw, rope128, rope64)

    gates = jax.nn.sigmoid(gate.reshape(B, S, N_BRANCHES, D_MODEL))
    branches = (y_a, y_b, y_c, y_d)
    merged = gates[:, :, 0] * (branches[0] @ w_branch[0])
    for n in range(1, N_BRANCHES):
        merged = merged + gates[:, :, n] * (branches[n] @ w_branch[n])
    return merged @ w_out


def swiglu(h, w1, w3, w2):
    return (jax.nn.silu(h @ w1) * (h @ w3)) @ w2


def moe_ffn(h, router, w1, w3, w2):
    logits = (h @ router).astype(jnp.float32)
    top_vals, top_idx = lax.top_k(logits, TOP_K_EXPERTS)
    top_w = jax.nn.softmax(top_vals, axis=-1)
    combine = jnp.sum(jax.nn.one_hot(top_idx, N_EXPERTS, dtype=jnp.float32) * top_w[..., None], axis=-2)
    combine = combine.astype(h.dtype)
    out = jnp.zeros_like(h)
    for e in range(N_EXPERTS):
        out = out + combine[..., e:e + 1] * swiglu(h, w1[e], w3[e], w2[e])
    return out


def _normal(key, shape, scale):
    return jax.random.normal(key, shape, jnp.float32) * scale


def setup_inputs(seed: int = 0) -> dict:
    key = jax.random.key(seed)
    ks = jax.random.split(key, 20)
    return {
        "x": _normal(ks[0], (BATCH, SEQ, D_MODEL), 1.0),
        "attn_norm": 1.0 + _normal(ks[1], (DEPTH, D_MODEL), 0.02),
        "w_in": _normal(ks[2], (DEPTH, D_MODEL, PROJ_WIDTH), D_MODEL ** -0.5),
        "conv_w": _normal(ks[3], (DEPTH, CONV_K, CONV_WIDTH), CONV_K ** -0.5),
        "sgu_norm": 1.0 + _normal(ks[4], (DEPTH, SGU_WIDTH), 0.02),
        "sgu_w": _normal(ks[5], (DEPTH, SGU_GROUPS, CHUNK, CHUNK), CHUNK ** -0.5),
        "sgu_b": 1.0 + _normal(ks[6], (DEPTH, SGU_GROUPS, CHUNK), 0.02),
        "diff_lambda": _normal(ks[7], (DEPTH, 4, DIFF_SUB_DIM), 0.1),
        "diff_subln": 1.0 + _normal(ks[8], (DEPTH, HEAD_DIM), 0.02),
        "w_branch": _normal(ks[9], (DEPTH, N_BRANCHES, CONV_WIDTH, D_MODEL), CONV_WIDTH ** -0.5),
        "w_out": _normal(ks[10], (DEPTH, D_MODEL, D_MODEL), D_MODEL ** -0.5),
        "ffn_norm": 1.0 + _normal(ks[11], (DEPTH, D_MODEL), 0.02),
        "dense_w1": _normal(ks[12], (N_DENSE, D_MODEL, D_FF), D_MODEL ** -0.5),
        "dense_w3": _normal(ks[13], (N_DENSE, D_MODEL, D_FF), D_MODEL ** -0.5),
        "dense_w2": _normal(ks[14], (N_DENSE, D_FF, D_MODEL), D_FF ** -0.5),
        "router": _normal(ks[15], (N_MOE, D_MODEL, N_EXPERTS), D_MODEL ** -0.5),
        "moe_w1": _normal(ks[16], (N_MOE, N_EXPERTS, D_MODEL, D_FF), D_MODEL ** -0.5),
        "moe_w3": _normal(ks[17], (N_MOE, N_EXPERTS, D_MODEL, D_FF), D_MODEL ** -0.5),
        "moe_w2": _normal(ks[18], (N_MOE, N_EXPERTS, D_FF, D_MODEL), D_FF ** -0.5),
        "final_norm": 1.0 + _normal(ks[19], (D_MODEL,), 0.02),
    }


def reference(x, attn_norm, w_in, conv_w, sgu_norm, sgu_w, sgu_b, diff_lambda, diff_subln,
              w_branch, w_out, ffn_norm, dense_w1, dense_w3, dense_w2, router,
              moe_w1, moe_w3, moe_w2, final_norm):
    S = x.shape[1]
    pos = jnp.arange(S)
    rope64 = rope_tables(pos, DIFF_SUB_DIM)
    rope128 = rope_tables(pos, HEAD_DIM)
    h = x
    for layer in range(DEPTH):
        hn = rmsnorm(h, attn_norm[layer])
        h = h + mixer_layer(hn, layer, w_in[layer], conv_w[layer], sgu_norm[layer], sgu_w[layer],
                            sgu_b[layer], diff_lambda[layer], diff_subln[layer], w_branch[layer],
                            w_out[layer], rope64, rope128)
        hn = rmsnorm(h, ffn_norm[layer])
        j = layer // 2
        if layer % 2 == 0:
            h = h + swiglu(hn, dense_w1[j], dense_w3[j], dense_w2[j])
        else:
            h = h + moe_ffn(hn, router[j], moe_w1[j], moe_w3[j], moe_w2[j])
    return rmsnorm(h, final_norm)
```

```python
import functools
import math

import jax
import jax.numpy as jnp
from jax import lax
from jax.experimental import pallas as pl
from jax.experimental.pallas import tpu as pltpu

F32 = jnp.float32
BF16 = jnp.bfloat16
I32 = jnp.int32

LANES = 128
HEAD_DIM = 128
BRANCH_WIDTH = 512
N_BRANCHES = 4
N_HEADS = 4
IDX_HEADS = 8
IDX_DIM = 64
DIFF_SUB_DIM = 64
CONV_K = 3
CHUNK = 128
TOPK_MAX = 256
ROPE_THETA = 10000.0
EPS = 1e-6
N_MAIN_TILES = 12
MAIN_WIDTH = N_MAIN_TILES * BRANCH_WIDTH
(T_CB, T_CC, T_CH, T_SU, T_SV, T_DQ, T_DK, T_DV, T_AQ, T_AK, T_AV, T_IQ) = range(12)
NEG_BIG = -1e30
INT_MIN = -(2 ** 31)
VMEM_LIMIT = 56 * 1024 * 1024


def _pick(n, pref):
    t = min(pref, n)
    while n % t:
        t //= 2
    return t


def _cparams(sem):
    return pltpu.CompilerParams(dimension_semantics=sem, vmem_limit_bytes=VMEM_LIMIT)


def _dot(a, b):
    return jnp.dot(a, b, preferred_element_type=F32)


def _dot_nt(a, b):
    return lax.dot_general(a, b, (((1,), (1,)), ((), ())), preferred_element_type=F32)


def _rmsnorm_rows(x, g):
    ms = jnp.mean(x * x, axis=-1, keepdims=True)
    return (x * lax.rsqrt(ms + EPS)) * g


def _rope64(x, c, s):
    lane = lax.broadcasted_iota(I32, x.shape, 1)
    rot = jnp.where((lane & 32) == 0, pltpu.roll(x, 96, 1), pltpu.roll(x, 32, 1))
    return x * c + rot * s


def _rope128(x, c, s):
    return x * c + pltpu.roll(x, 64, 1) * s


def _gelu_tanh(x):
    c = math.sqrt(2.0 / math.pi)
    return x * (0.5 * (1.0 + jnp.tanh(c * (x + 0.044715 * (x * x * x)))))


def _proj_kernel(x_ref, g_ref, w_ref, ws_ref, c64_ref, s64_ref, c128_ref, s128_ref,
                 proj_ref, hn_ref, ikd_ref, iw_ref, hn_scr):
    j = pl.program_id(1)

    @pl.when(j == 0)
    def _():
        hb = _rmsnorm_rows(x_ref[...], g_ref[...]).astype(BF16)
        hn_scr[...] = hb
        hn_ref[...] = hb
        small = _dot(hb, ws_ref[...])
        ikd_ref[...] = _rope64(small[:, :LANES], c64_ref[...], s64_ref[...]).astype(BF16)
        iw_ref[...] = small[:, LANES:]

    acc = _dot(hn_scr[...], w_ref[...])
    is64 = (j == T_DQ) | (j == T_DK) | (j == T_IQ)
    is128 = (j == T_AQ) | (j == T_AK)

    @pl.when(jnp.logical_not(is64 | is128))
    def _():
        proj_ref[...] = acc.astype(BF16)

    @pl.when(is64)
    def _():
        scale = jnp.where(j == T_DQ, DIFF_SUB_DIM ** -0.5, 1.0).astype(F32)
        c, s = c64_ref[...], s64_ref[...]
        for cb in range(BRANCH_WIDTH // LANES):
            sl = slice(cb * LANES, (cb + 1) * LANES)
            proj_ref[:, sl] = (_rope64(acc[:, sl], c, s) * scale).astype(BF16)

    @pl.when(is128)
    def _():
        c, s = c128_ref[...], s128_ref[...]
        for cb in range(BRANCH_WIDTH // LANES):
            sl = slice(cb * LANES, (cb + 1) * LANES)
            proj_ref[:, sl] = _rope128(acc[:, sl], c, s).astype(BF16)


def _proj_call(h2, g, w_main, w_small, rope, seq):
    T, D = h2.shape
    tm = _pick(seq, 1024)
    nrb = seq // tm
    rope_spec = pl.BlockSpec((tm, LANES), lambda i, j: (i % nrb, 0))
    row_spec = lambda w: pl.BlockSpec((tm, w), lambda i, j: (i, 0))
    return pl.pallas_call(
        _proj_kernel,
        grid=(T // tm, N_MAIN_TILES),
        in_specs=[
            row_spec(D),
            pl.BlockSpec((1, D), lambda i, j: (0, 0)),
            pl.BlockSpec((D, BRANCH_WIDTH), lambda i, j: (0, j)),
            pl.BlockSpec((D, 2 * LANES), lambda i, j: (0, 0)),
            rope_spec, rope_spec, rope_spec, rope_spec,
        ],
        out_specs=[
            pl.BlockSpec((tm, BRANCH_WIDTH), lambda i, j: (i, j)),
            row_spec(D), row_spec(LANES), row_spec(LANES),
        ],
        out_shape=[
            jax.ShapeDtypeStruct((T, MAIN_WIDTH), BF16),
            jax.ShapeDtypeStruct((T, D), BF16),
            jax.ShapeDtypeStruct((T, LANES), BF16),
            jax.ShapeDtypeStruct((T, LANES), F32),
        ],
        scratch_shapes=[pltpu.VMEM((tm, D), BF16)],
        compiler_params=_cparams(("parallel", "arbitrary")),
        name="proj",
    )(h2, g, w_main, w_small, *rope)


def _online_softmax_step(s, v, m_scr, l_scr, acc_scr):
    m_prev = m_scr[...]
    m_new = jnp.maximum(m_prev, jnp.max(s, axis=-1, keepdims=True))
    alpha = jnp.exp(m_prev - m_new)
    p = jnp.exp(s - m_new)
    l_scr[...] = alpha * l_scr[...] + jnp.sum(p, axis=-1, keepdims=True)
    acc_scr[...] = alpha * acc_scr[...] + _dot(p.astype(BF16), v)
    m_scr[...] = m_new


def _diff_kernel(q_ref, k_ref, v_ref, dl_ref, subln_ref, o_ref,
                 qq_scr, m_scr, l_scr, acc_scr, *, tq, lam_init):
    qi = pl.program_id(2)
    q = q_ref[...]
    lane = lax.broadcasted_iota(I32, q.shape, 1)
    zero = jnp.zeros_like(q)
    qq_scr[0:tq, :] = jnp.where(lane < DIFF_SUB_DIM, q, zero)
    qq_scr[tq:2 * tq, :] = jnp.where(lane >= DIFF_SUB_DIM, q, zero)
    m_scr[...] = jnp.full(m_scr.shape, NEG_BIG, F32)
    l_scr[...] = jnp.zeros(l_scr.shape, F32)
    acc_scr[...] = jnp.zeros(acc_scr.shape, F32)

    def step(kj, causal):
        off = pl.multiple_of(kj * tq, tq)
        k = k_ref[pl.ds(off, tq), :]
        v = v_ref[pl.ds(off, tq), :]
        s = _dot_nt(qq_scr[...], k)
        if causal:
            row = lax.broadcasted_iota(I32, s.shape, 0)
            row = jnp.where(row >= tq, row - tq, row)
            col = lax.broadcasted_iota(I32, s.shape, 1)
            s = jnp.where(col <= row, s, NEG_BIG)
        _online_softmax_step(s, v, m_scr, l_scr, acc_scr)

    def body(kj, carry):
        step(kj, False)
        return carry

    lax.fori_loop(0, qi, body, 0)
    step(qi, True)

    dl = dl_ref[...]
    lam = (jnp.exp(jnp.sum(dl[0:1] * dl[1:2], axis=-1, keepdims=True))
           - jnp.exp(jnp.sum(dl[2:3] * dl[3:4], axis=-1, keepdims=True)) + lam_init)
    o1 = acc_scr[0:tq, :] / l_scr[0:tq, :]
    o2 = acc_scr[tq:2 * tq, :] / l_scr[tq:2 * tq, :]
    o = o1 - lam * o2
    o_ref[...] = (_rmsnorm_rows(o, subln_ref[...]) * (1.0 - lam_init)).astype(BF16)


def _diff_call(proj, diff_lambda, subln, lam_init, batch, seq):
    T = proj.shape[0]
    tq = _pick(seq, 512)
    nq = seq // tq
    hpt = BRANCH_WIDTH // LANES
    kern = functools.partial(_diff_kernel, tq=tq, lam_init=lam_init)
    return pl.pallas_call(
        kern,
        grid=(batch, N_HEADS, nq),
        in_specs=[
            pl.BlockSpec((tq, LANES), lambda b, h, i: (b * nq + i, T_DQ * hpt + h)),
            pl.BlockSpec((seq, LANES), lambda b, h, i: (b, T_DK * hpt + h)),
            pl.BlockSpec((seq, LANES), lambda b, h, i: (b, T_DV * hpt + h)),
            pl.BlockSpec((4, DIFF_SUB_DIM), lambda b, h, i: (0, 0)),
            pl.BlockSpec((1, HEAD_DIM), lambda b, h, i: (0, 0)),
        ],
        out_specs=pl.BlockSpec((tq, LANES), lambda b, h, i: (b * nq + i, h)),
        out_shape=jax.ShapeDtypeStruct((T, BRANCH_WIDTH), BF16),
        scratch_shapes=[
            pltpu.VMEM((2 * tq, LANES), BF16),
            pltpu.VMEM((2 * tq, 1), F32),
            pltpu.VMEM((2 * tq, 1), F32),
            pltpu.VMEM((2 * tq, LANES), F32),
        ],
        compiler_params=_cparams(("parallel", "parallel", "arbitrary")),
        name="diff_attn",
    )(proj, proj, proj, diff_lambda, subln)


def _dsa_select_kernel(iq_ref, ikd_ref, iw_ref, mask_ref, key_scr, *, tq, tc, nc, topk, idx_bits):
    qi = pl.program_id(1)
    n_ch = (qi * tq + tq + tc - 1) // tc
    iq = iq_ref[...]
    iw = iw_ref[...]
    lane = lax.broadcasted_iota(I32, (tq, LANES), 1)
    zero = jnp.zeros((tq, LANES), BF16)
    qm = []
    for h in range(IDX_HEADS):
        blk = iq[:, (h // 2) * LANES:(h // 2 + 1) * LANES]
        keep = (lane < IDX_DIM) if h % 2 == 0 else (lane >= IDX_DIM)
        qm.append(jnp.where(keep, blk, zero))
    idx_scale = IDX_DIM ** -0.5 * IDX_HEADS ** -0.5
    row_pos = qi * tq + lax.broadcasted_iota(I32, (tq, tc), 0)
    col_iota = lax.broadcasted_iota(I32, (tq, tc), 1)

    def score_chunk(c, carry):
        ik = ikd_ref[pl.ds(pl.multiple_of(c * tc, tc), tc), :]
        sc = jnp.zeros((tq, tc), F32)
        for h in range(IDX_HEADS):
            sc = sc + iw[:, h:h + 1] * jnp.maximum(_dot_nt(qm[h], ik), 0.0)
        sc = sc * idx_scale
        sc = jnp.where(c * tc + col_iota <= row_pos, sc, -jnp.inf)
        bits = pltpu.bitcast(sc, I32)
        key_scr[c] = jnp.where(bits < 0, INT_MIN - bits, bits)
        return carry

    lax.fori_loop(0, n_ch, score_chunk, 0)

    def count(pred):
        def body(c, acc):
            kc = key_scr[c]
            hit = pred(kc, c)
            for lb in range(tc // LANES):
                acc = acc + jnp.where(hit[:, lb * LANES:(lb + 1) * LANES], 1, 0)
            return acc
        acc = lax.fori_loop(0, n_ch, body, jnp.zeros((tq, LANES), I32))
        return jnp.sum(acc, axis=1, keepdims=True)

    def count_ge(cand):
        cb = jnp.broadcast_to(cand, (tq, tc))
        return count(lambda kc, c: kc >= cb)

    thr = jnp.where(count_ge(jnp.zeros((tq, 1), I32)) >= topk, 0, INT_MIN).astype(I32)

    def bit_body(it, thr):
        cand = thr + jnp.left_shift(jnp.int32(1), 30 - it)
        return jnp.where(count_ge(cand) >= topk, cand, thr)

    thr = lax.fori_loop(0, 31, bit_body, thr)
    thr_b = jnp.broadcast_to(thr, (tq, tc))
    cnt_gt = count(lambda kc, c: kc > thr_b)
    cnt_ge = count_ge(thr)
    need = topk - cnt_gt
    excess = jnp.max(cnt_ge - topk) > 0

    def tie_search(_):
        def tbit(it, p):
            cand = p + jnp.left_shift(jnp.int32(1), idx_bits - 1 - it)
            cb = jnp.broadcast_to(cand, (tq, tc))
            cnt = count(lambda kc, c: (kc == thr_b) & (c * tc + col_iota < cb))
            return jnp.where(cnt < need, cand, p)
        return lax.fori_loop(0, idx_bits, tbit, jnp.zeros((tq, 1), I32))

    last = lax.cond(excess, tie_search, lambda _: jnp.full((tq, 1), 2 ** idx_bits, I32), 0)
    last_b = jnp.broadcast_to(last, (tq, tc))

    def emit(c, carry):
        kc = key_scr[c]
        kpos = c * tc + col_iota
        sel = (kc > thr_b) | ((kc == thr_b) & (kpos <= last_b))
        sel = sel & (kpos <= row_pos)
        mask_ref[0, 0, c] = jnp.where(sel, 1, 0).astype(jnp.int8)
        return carry

    lax.fori_loop(0, n_ch, emit, 0)

    def clear(c, carry):
        mask_ref[0, 0, c] = jnp.zeros((tq, tc), jnp.int8)
        return carry

    lax.fori_loop(n_ch, nc, clear, 0)


def _dsa_tiles(seq):
    tq = _pick(seq, 256)
    tc = _pick(seq, 512)
    return tq, tc


def _dsa_select_call(proj, ikd, iw, batch, seq):
    tq, tc = _dsa_tiles(seq)
    nq, nc = seq // tq, seq // tc
    topk = min(TOPK_MAX, seq // 4)
    assert tc >= topk
    idx_bits = max(1, (seq - 1).bit_length())
    kern = functools.partial(_dsa_select_kernel, tq=tq, tc=tc, nc=nc, topk=topk, idx_bits=idx_bits)
    return pl.pallas_call(
        kern,
        grid=(batch, nq),
        in_specs=[
            pl.BlockSpec((tq, BRANCH_WIDTH), lambda b, i: (b * nq + i, T_IQ)),
            pl.BlockSpec((seq, LANES), lambda b, i: (b, 0)),
            pl.BlockSpec((tq, LANES), lambda b, i: (b * nq + i, 0)),
        ],
        out_specs=pl.BlockSpec((1, 1, nc, tq, tc), lambda b, i: (b, i, 0, 0, 0)),
        out_shape=jax.ShapeDtypeStruct((batch, nq, nc, tq, tc), jnp.int8),
        scratch_shapes=[pltpu.VMEM((nc, tq, tc), I32)],
        compiler_params=_cparams(("parallel", "arbitrary")),
        name="dsa_select",
    )(proj, ikd, iw)


def _dsa_attn_kernel(q_ref, k_ref, v_ref, mask_ref, o_ref, m_scr, l_scr, acc_scr, *, tq, tc):
    qi = pl.program_id(2)
    n_ch = (qi * tq + tq + tc - 1) // tc
    q = q_ref[...]
    scale = HEAD_DIM ** -0.5
    m_scr[...] = jnp.full(m_scr.shape, NEG_BIG, F32)
    l_scr[...] = jnp.zeros(l_scr.shape, F32)
    acc_scr[...] = jnp.zeros(acc_scr.shape, F32)

    def body(c, carry):
        off = pl.multiple_of(c * tc, tc)
        k = k_ref[pl.ds(off, tc), :]
        v = v_ref[pl.ds(off, tc), :]
        s = _dot_nt(q, k) * scale
        sel = mask_ref[0, 0, c].astype(I32) != 0
        s = jnp.where(sel, s, NEG_BIG)
        _online_softmax_step(s, v, m_scr, l_scr, acc_scr)
        return carry

    lax.fori_loop(0, n_ch, body, 0)
    o_ref[...] = (acc_scr[...] / l_scr[...]).astype(BF16)


def _dsa_attn_call(proj, mask, batch, seq):
    T = proj.shape[0]
    tq, tc = _dsa_tiles(seq)
    nq, nc = seq // tq, seq // tc
    hpt = BRANCH_WIDTH // LANES
    kern = functools.partial(_dsa_attn_kernel, tq=tq, tc=tc)
    return pl.pallas_call(
        kern,
        grid=(batch, N_HEADS, nq),
        in_specs=[
            pl.BlockSpec((tq, LANES), lambda b, h, i: (b * nq + i, T_AQ * hpt + h)),
            pl.BlockSpec((seq, LANES), lambda b, h, i: (b, T_AK * hpt + h)),
            pl.BlockSpec((seq, LANES), lambda b, h, i: (b, T_AV * hpt + h)),
            pl.BlockSpec((1, 1, nc, tq, tc), lambda b, h, i: (b, i, 0, 0, 0)),
        ],
        out_specs=pl.BlockSpec((tq, LANES), lambda b, h, i: (b * nq + i, h)),
        out_shape=jax.ShapeDtypeStruct((T, BRANCH_WIDTH), BF16),
        scratch_shapes=[
            pltpu.VMEM((tq, 1), F32),
            pltpu.VMEM((tq, 1), F32),
            pltpu.VMEM((tq, LANES), F32),
        ],
        compiler_params=_cparams(("parallel", "parallel", "arbitrary")),
        name="dsa_attn",
    )(proj, proj, proj, mask)


def _merge_kernel(cb_ref, cc_ref, ch_ref, ccp_ref, chp_ref, su_ref, sv_ref, yc_ref, yd_ref,
                  hn_ref, res_ref, convw_ref, sgun_ref, sguw_ref, sgub_ref,
                  wg0_ref, wg1_ref, wg2_ref, wg3_ref, wb_ref, wo_ref, fng_ref, router_ref,
                  h_ref, hn2_ref, comb_ref, ycat_scr, acc_scr, *, tm, rows_per_seq, n_j, n_experts):
    i = pl.program_id(0)
    j = pl.program_id(1)

    @pl.when(j == 0)
    def _():
        acc_scr[...] = jnp.zeros(acc_scr.shape, F32)
        z = cc_ref[...].astype(F32) * ch_ref[...].astype(F32)
        first = ((i * tm) % rows_per_seq) == 0
        zp = ccp_ref[...].astype(F32) * chp_ref[...].astype(F32)
        zp = zp * jnp.where(first, 0.0, 1.0).astype(F32)
        row = lax.broadcasted_iota(I32, z.shape, 0)
        z1 = jnp.where(row == 0, zp[7:8, :], pltpu.roll(z, 1, 0))
        z2 = jnp.where(row == 0, zp[6:7, :], jnp.where(row == 1, zp[7:8, :], pltpu.roll(z, 2, 0)))
        cw = convw_ref[...]
        conv = cw[0:1, :] * z2 + cw[1:2, :] * z1 + cw[2:3, :] * z
        ycat_scr[:, 0:BRANCH_WIDTH] = (cb_ref[...].astype(F32) * conv).astype(BF16)
        u = _gelu_tanh(su_ref[...].astype(F32))
        v = _rmsnorm_rows(_gelu_tanh(sv_ref[...].astype(F32)), sgun_ref[...]).astype(BF16)
        tri_r = lax.broadcasted_iota(I32, (CHUNK, CHUNK), 0)
        tri_c = lax.broadcasted_iota(I32, (CHUNK, CHUNK), 1)
        bias = sgub_ref[...]
        for g in range(BRANCH_WIDTH // LANES):
            wg = jnp.where(tri_c <= tri_r, sguw_ref[g], 0.0).astype(BF16)
            cols = slice(g * LANES, (g + 1) * LANES)
            for ck in range(tm // CHUNK):
                rows = slice(ck * CHUNK, (ck + 1) * CHUNK)
                y = _dot(wg, v[rows, cols]) + bias[:, g:g + 1]
                ycat_scr[rows, BRANCH_WIDTH + g * LANES:BRANCH_WIDTH + (g + 1) * LANES] = (
                    u[rows, cols] * y).astype(BF16)
        ycat_scr[:, 2 * BRANCH_WIDTH:3 * BRANCH_WIDTH] = yc_ref[...]
        ycat_scr[:, 3 * BRANCH_WIDTH:4 * BRANCH_WIDTH] = yd_ref[...]

    hn = hn_ref[...]
    merged = None
    for n, wg_ref in enumerate((wg0_ref, wg1_ref, wg2_ref, wg3_ref)):
        gate = jax.nn.sigmoid(_dot(hn, wg_ref[...]))
        br = _dot(ycat_scr[:, n * BRANCH_WIDTH:(n + 1) * BRANCH_WIDTH], wb_ref[n])
        merged = gate * br if merged is None else merged + gate * br
    acc_scr[...] += _dot(merged.astype(BF16), wo_ref[...])

    @pl.when(j == n_j - 1)
    def _():
        h_new = res_ref[...] + acc_scr[...]
        h_ref[...] = h_new
        hn2 = _rmsnorm_rows(h_new, fng_ref[...])
        hb = hn2.astype(BF16)
        hn2_ref[...] = hb
        logits = _dot(hb, router_ref[...])
        lane = lax.broadcasted_iota(I32, logits.shape, 1)
        logits = jnp.where(lane < n_experts, logits, -jnp.inf)
        v1 = jnp.max(logits, axis=-1, keepdims=True)
        i1 = jnp.min(jnp.where(logits == v1, lane, LANES), axis=-1, keepdims=True)
        rest = jnp.where(lane == i1, -jnp.inf, logits)
        v2 = jnp.max(rest, axis=-1, keepdims=True)
        i2 = jnp.min(jnp.where(rest == v2, lane, LANES), axis=-1, keepdims=True)
        e2 = jnp.exp(v2 - v1)
        w1 = 1.0 / (1.0 + e2)
        w2 = e2 / (1.0 + e2)
        comb_ref[...] = jnp.where(lane == i1, w1, 0.0) + jnp.where(lane == i2, w2, 0.0)


def _merge_call(proj, y_c, y_d, hn, h2, conv_w, sgu_norm, sgu_w, sgu_b_t, w_gate, w_branch, w_out,
                ffn_g, router_pad, seq, n_experts):
    T, D = h2.shape
    tm = _pick(seq, 256)
    tn = _pick(D, 256)
    n_j = D // tn
    rb8 = tm // 8
    kern = functools.partial(_merge_kernel, tm=tm, rows_per_seq=seq, n_j=n_j, n_experts=n_experts)
    tile = lambda t: pl.BlockSpec((tm, BRANCH_WIDTH), lambda i, j, t=t: (i, t))
    prev = lambda t: pl.BlockSpec((8, BRANCH_WIDTH), lambda i, j, t=t: (jnp.maximum(i * rb8 - 1, 0), t))
    full = lambda shp: pl.BlockSpec(shp, lambda i, j: (0,) * len(shp))
    gate = lambda n: pl.BlockSpec((D, tn), lambda i, j, n=n: (0, n * n_j + j))
    rows = lambda w: pl.BlockSpec((tm, w), lambda i, j: (i, 0))
    return pl.pallas_call(
        kern,
        grid=(T // tm, n_j),
        in_specs=[
            tile(T_CB), tile(T_CC), tile(T_CH), prev(T_CC), prev(T_CH), tile(T_SU), tile(T_SV),
            rows(BRANCH_WIDTH), rows(BRANCH_WIDTH), rows(D), rows(D),
            full((CONV_K, BRANCH_WIDTH)), full((1, BRANCH_WIDTH)),
            full((BRANCH_WIDTH // LANES, CHUNK, CHUNK)), full((CHUNK, BRANCH_WIDTH // LANES)),
            gate(0), gate(1), gate(2), gate(3),
            pl.BlockSpec((N_BRANCHES, BRANCH_WIDTH, tn), lambda i, j: (0, 0, j)),
            pl.BlockSpec((tn, D), lambda i, j: (j, 0)),
            full((1, D)), full((D, LANES)),
        ],
        out_specs=[rows(D), rows(D), rows(LANES)],
        out_shape=[
            jax.ShapeDtypeStruct((T, D), F32),
            jax.ShapeDtypeStruct((T, D), BF16),
            jax.ShapeDtypeStruct((T, LANES), F32),
        ],
        scratch_shapes=[pltpu.VMEM((tm, N_BRANCHES * BRANCH_WIDTH), BF16), pltpu.VMEM((tm, D), F32)],
        compiler_params=_cparams(("parallel", "arbitrary")),
        name="merge",
    )(proj, proj, proj, proj, proj, proj, proj, y_c, y_d, hn, h2, conv_w, sgu_norm, sgu_w, sgu_b_t,
      w_gate, w_gate, w_gate, w_gate, w_branch, w_out, ffn_g, router_pad)


def _ffn_kernel(x_ref, comb_ref, w1_ref, w3_ref, w2_ref, res_ref, g_ref, o_ref, acc_e, acc_t,
                *, n_e, n_f, final_norm):
    e = pl.program_id(1)
    f = pl.program_id(2)

    @pl.when((e == 0) & (f == 0))
    def _():
        acc_t[...] = jnp.zeros(acc_t.shape, F32)

    @pl.when(f == 0)
    def _():
        acc_e[...] = jnp.zeros(acc_e.shape, F32)

    x = x_ref[...]
    h1 = _dot(x, w1_ref[...])
    h3 = _dot(x, w3_ref[...])
    a = (h1 * jax.nn.sigmoid(h1)) * h3
    acc_e[...] += _dot(a.astype(BF16), w2_ref[...])

    @pl.when(f == n_f - 1)
    def _():
        comb = comb_ref[...]
        lane = lax.broadcasted_iota(I32, comb.shape, 1)
        c = jnp.sum(jnp.where(lane == e, comb, 0.0), axis=-1, keepdims=True)
        acc_t[...] += c * acc_e[...]

    @pl.when((e == n_e - 1) & (f == n_f - 1))
    def _():
        out = res_ref[...] + acc_t[...]
        if final_norm:
            out = _rmsnorm_rows(out, g_ref[...])
        o_ref[...] = out


def _ffn_call(xb, comb, w1, w3, w2, res, g, final_norm):
    T, D = res.shape
    n_e, _, d_ff = w1.shape
    tm = _pick(T, 512)
    tf = _pick(d_ff, 512)
    n_f = d_ff // tf
    kern = functools.partial(_ffn_kernel, n_e=n_e, n_f=n_f, final_norm=final_norm)
    rows = lambda w: pl.BlockSpec((tm, w), lambda i, e, f: (i, 0))
    return pl.pallas_call(
        kern,
        grid=(T // tm, n_e, n_f),
        in_specs=[
            rows(D), rows(LANES),
            pl.BlockSpec((None, D, tf), lambda i, e, f: (e, 0, f)),
            pl.BlockSpec((None, D, tf), lambda i, e, f: (e, 0, f)),
            pl.BlockSpec((None, tf, D), lambda i, e, f: (e, f, 0)),
            rows(D),
            pl.BlockSpec((1, D), lambda i, e, f: (0, 0)),
        ],
        out_specs=rows(D),
        out_shape=jax.ShapeDtypeStruct((T, D), F32),
        scratch_shapes=[pltpu.VMEM((tm, D), F32), pltpu.VMEM((tm, D), F32)],
        compiler_params=_cparams(("parallel", "arbitrary", "arbitrary")),
        name="ffn",
    )(xb, comb, w1, w3, w2, res, g)


def _rope_tables(seq):
    pos = jnp.arange(seq, dtype=F32)

    def tab(dim):
        inv = ROPE_THETA ** (-jnp.arange(0, dim, 2, dtype=F32) / dim)
        ang = pos[:, None] * inv[None, :]
        cos = jnp.concatenate([jnp.cos(ang), jnp.cos(ang)], axis=-1)
        sin = jnp.concatenate([-jnp.sin(ang), jnp.sin(ang)], axis=-1)
        return cos, sin

    c64, s64 = tab(IDX_DIM)
    c128, s128 = tab(HEAD_DIM)
    return (jnp.tile(c64, (1, 2)), jnp.tile(s64, (1, 2)), c128, s128)


def kernel(x, attn_norm, w_in, conv_w, sgu_norm, sgu_w, sgu_b, diff_lambda, diff_subln, w_branch, w_out,
           ffn_norm, dense_w1, dense_w3, dense_w2, router, moe_w1, moe_w3, moe_w2, final_norm):
    batch, seq, D = x.shape
    depth = w_in.shape[0]
    n_experts = router.shape[-1]
    T = batch * seq
    rope = _rope_tables(seq)
    h = x.reshape(T, D)
    ik0 = MAIN_WIDTH
    iw0 = ik0 + IDX_DIM
    g0 = iw0 + IDX_HEADS
    ones_comb = jnp.zeros((T, LANES), F32).at[:, 0].set(1.0)

    for layer in range(depth):
        wl = w_in[layer]
        w_main = wl[:, :MAIN_WIDTH].astype(BF16)
        w_small = jnp.concatenate(
            [wl[:, ik0:iw0], wl[:, ik0:iw0], wl[:, iw0:g0], jnp.zeros((D, LANES - IDX_HEADS), F32)],
            axis=1).astype(BF16)
        w_gate = wl[:, g0:].astype(BF16)
        lam_init = 0.8 - 0.6 * math.exp(-0.3 * layer)

        proj, hn, ikd, iw = _proj_call(h, attn_norm[layer][None, :], w_main, w_small, rope, seq)
        y_c = _diff_call(proj, diff_lambda[layer], diff_subln[layer][None, :], lam_init, batch, seq)
        mask = _dsa_select_call(proj, ikd, iw, batch, seq)
        y_d = _dsa_attn_call(proj, mask, batch, seq)

        j = layer // 2
        is_moe = layer % 2 == 1
        router_pad = jnp.zeros((D, LANES), F32)
        if is_moe:
            router_pad = router_pad.at[:, :n_experts].set(router[j])
        h, hn2, comb = _merge_call(
            proj, y_c, y_d, hn, h, conv_w[layer], sgu_norm[layer][None, :], sgu_w[layer],
            sgu_b[layer].T, w_gate, w_branch[layer].astype(BF16), w_out[layer].astype(BF16),
            ffn_norm[layer][None, :], router_pad.astype(BF16), seq, n_experts)

        last = layer == depth - 1
        if is_moe:
            w1, w3, w2 = moe_w1[j], moe_w3[j], moe_w2[j]
        else:
            w1, w3, w2 = dense_w1[j][None], dense_w3[j][None], dense_w2[j][None]
            comb = ones_comb
        h = _ffn_call(hn2, comb, w1.astype(BF16), w3.astype(BF16), w2.astype(BF16), h,
                      final_norm[None, :], last)

    if depth == 0:
        raise ValueError("depth must be positive")
    return h.reshape(batch, seq, D)
```

```python
import functools
import math

import jax
import jax.numpy as jnp
from jax import lax
from jax.experimental import pallas as pl
from jax.experimental.pallas import tpu as pltpu

F32 = jnp.float32
BF16 = jnp.bfloat16
I32 = jnp.int32

LANES = 128
SLAB = 8
HEAD_DIM = 128
BRANCH_WIDTH = 512
N_BRANCHES = 4
N_HEADS = 4
IDX_HEADS = 8
IDX_DIM = 64
DIFF_SUB_DIM = 64
CONV_K = 3
CHUNK = 128
TOPK_MAX = 256
ROPE_THETA = 10000.0
EPS = 1e-6
N_MAIN_TILES = 12
MAIN_WIDTH = N_MAIN_TILES * BRANCH_WIDTH
(T_CB, T_CC, T_CH, T_SU, T_SV, T_DQ, T_DK, T_DV, T_AQ, T_AK, T_AV, T_IQ) = range(12)
NEG_BIG = -1e30
INT_MIN = -(2 ** 31)
VMEM_LIMIT = 56 * 1024 * 1024


def _pick(n, pref):
    t = min(pref, n)
    while n % t:
        t //= 2
    return t


def _cparams(sem):
    return pltpu.CompilerParams(dimension_semantics=sem, vmem_limit_bytes=VMEM_LIMIT)


def _dot(a, b):
    return jnp.dot(a, b, preferred_element_type=F32)


def _dot_nt(a, b):
    return lax.dot_general(a, b, (((1,), (1,)), ((), ())), preferred_element_type=F32)


def _dot_tn(a, b):
    return lax.dot_general(a, b, (((0,), (0,)), ((), ())), preferred_element_type=F32)


def _rmsnorm_rows(x, g):
    ms = jnp.mean(x * x, axis=-1, keepdims=True)
    return (x * lax.rsqrt(ms + EPS)) * g


def _rope64(x, c, s):
    lane = lax.broadcasted_iota(I32, x.shape, 1)
    rot = jnp.where((lane & 32) == 0, pltpu.roll(x, 96, 1), pltpu.roll(x, 32, 1))
    return x * c + rot * s


def _rope128(x, c, s):
    return x * c + pltpu.roll(x, 64, 1) * s


def _gelu_tanh(x):
    c = math.sqrt(2.0 / math.pi)
    return x * (0.5 * (1.0 + jnp.tanh(c * (x + 0.044715 * (x * x * x)))))


def _proj_kernel(x_ref, g_ref, w_ref, ws_ref, c64_ref, s64_ref, c128_ref, s128_ref,
                 proj_ref, hn_ref, ikd_ref, iw_ref, hn_scr):
    j = pl.program_id(1)

    @pl.when(j == 0)
    def _():
        hb = _rmsnorm_rows(x_ref[...], g_ref[...]).astype(BF16)
        hn_scr[...] = hb
        hn_ref[...] = hb
        small = _dot(hb, ws_ref[...])
        ikd_ref[...] = _rope64(small[:, :LANES], c64_ref[...], s64_ref[...]).astype(BF16)
        iw_ref[...] = small[:, LANES:]

    acc = _dot(hn_scr[...], w_ref[...])
    is64 = (j == T_DQ) | (j == T_DK) | (j == T_IQ)
    is128 = (j == T_AQ) | (j == T_AK)

    @pl.when(jnp.logical_not(is64 | is128))
    def _():
        proj_ref[...] = acc.astype(BF16)

    @pl.when(is64)
    def _():
        scale = jnp.where(j == T_DQ, DIFF_SUB_DIM ** -0.5, 1.0).astype(F32)
        c, s = c64_ref[...], s64_ref[...]
        for cb in range(BRANCH_WIDTH // LANES):
            sl = slice(cb * LANES, (cb + 1) * LANES)
            proj_ref[:, sl] = (_rope64(acc[:, sl], c, s) * scale).astype(BF16)

    @pl.when(is128)
    def _():
        c, s = c128_ref[...], s128_ref[...]
        for cb in range(BRANCH_WIDTH // LANES):
            sl = slice(cb * LANES, (cb + 1) * LANES)
            proj_ref[:, sl] = _rope128(acc[:, sl], c, s).astype(BF16)


def _proj_call(h2, g, w_main, w_small, rope, seq):
    T, D = h2.shape
    tm = _pick(seq, 1024)
    nrb = seq // tm
    rope_spec = pl.BlockSpec((tm, LANES), lambda i, j: (i % nrb, 0))
    row_spec = lambda w: pl.BlockSpec((tm, w), lambda i, j: (i, 0))
    return pl.pallas_call(
        _proj_kernel,
        grid=(T // tm, N_MAIN_TILES),
        in_specs=[
            row_spec(D),
            pl.BlockSpec((1, D), lambda i, j: (0, 0)),
            pl.BlockSpec((D, BRANCH_WIDTH), lambda i, j: (0, j)),
            pl.BlockSpec((D, 2 * LANES), lambda i, j: (0, 0)),
            rope_spec, rope_spec, rope_spec, rope_spec,
        ],
        out_specs=[
            pl.BlockSpec((tm, BRANCH_WIDTH), lambda i, j: (i, j)),
            row_spec(D), row_spec(LANES), row_spec(LANES),
        ],
        out_shape=[
            jax.ShapeDtypeStruct((T, MAIN_WIDTH), BF16),
            jax.ShapeDtypeStruct((T, D), BF16),
            jax.ShapeDtypeStruct((T, LANES), BF16),
            jax.ShapeDtypeStruct((T, LANES), F32),
        ],
        scratch_shapes=[pltpu.VMEM((tm, D), BF16)],
        compiler_params=_cparams(("parallel", "arbitrary")),
        name="proj",
    )(h2, g, w_main, w_small, *rope)


def _softmax_step_t(parts, m_scr, l_scr, acc_scr):
    m_prev = m_scr[...]
    m_new = m_prev
    for s_t, _ in parts:
        m_new = jnp.maximum(m_new, jnp.max(s_t, axis=0, keepdims=True))
    alpha = jnp.exp(m_prev - m_new)
    l_new = alpha * l_scr[...]
    acc = alpha * acc_scr[...]
    for s_t, v in parts:
        p_t = jnp.exp(s_t - m_new)
        l_new = l_new + jnp.sum(p_t, axis=0, keepdims=True)
        acc = acc + _dot_tn(v, p_t.astype(BF16))
    l_scr[...] = l_new
    acc_scr[...] = acc
    m_scr[...] = m_new


def _init_softmax_state(m_scr, l_scr, acc_scr):
    m_scr[...] = jnp.full(m_scr.shape, NEG_BIG, F32)
    l_scr[...] = jnp.zeros(l_scr.shape, F32)
    acc_scr[...] = jnp.zeros(acc_scr.shape, F32)


def _diff_kernel(q_ref, k_ref, v_ref, dl_ref, subln_ref, o_ref,
                 qq_scr, m_scr, l_scr, acc_scr, *, tq, lam_init):
    qi = pl.program_id(2)
    q = q_ref[...]
    lane = lax.broadcasted_iota(I32, q.shape, 1)
    zero = jnp.zeros_like(q)
    qq_scr[0:tq, :] = jnp.where(lane < DIFF_SUB_DIM, q, zero)
    qq_scr[tq:2 * tq, :] = jnp.where(lane >= DIFF_SUB_DIM, q, zero)
    _init_softmax_state(m_scr, l_scr, acc_scr)

    def step(kj, causal):
        off = pl.multiple_of(kj * tq, tq)
        k = k_ref[pl.ds(off, tq), :]
        v = v_ref[pl.ds(off, tq), :]
        s_t = _dot_nt(k, qq_scr[...])
        if causal:
            kpos = lax.broadcasted_iota(I32, s_t.shape, 0)
            qpos = lax.broadcasted_iota(I32, s_t.shape, 1)
            qpos = jnp.where(qpos >= tq, qpos - tq, qpos)
            s_t = jnp.where(kpos <= qpos, s_t, NEG_BIG)
        _softmax_step_t([(s_t, v)], m_scr, l_scr, acc_scr)

    def body(kj, carry):
        step(kj, False)
        return carry

    lax.fori_loop(0, qi, body, 0)
    step(qi, True)

    dl = dl_ref[...]
    lam = (jnp.exp(jnp.sum(dl[0:1] * dl[1:2], axis=-1, keepdims=True))
           - jnp.exp(jnp.sum(dl[2:3] * dl[3:4], axis=-1, keepdims=True)) + lam_init)
    o_all = acc_scr[...] / l_scr[...]
    o_t = o_all[:, 0:tq] - lam * o_all[:, tq:2 * tq]
    ms = jnp.mean(o_t * o_t, axis=0, keepdims=True)
    o_t = (o_t * lax.rsqrt(ms + EPS)) * subln_ref[...] * (1.0 - lam_init)
    o_ref[...] = o_t.T.astype(BF16)


def _diff_call(proj, diff_lambda, subln_col, lam_init, batch, seq):
    T = proj.shape[0]
    tq = _pick(seq, 512)
    nq = seq // tq
    hpt = BRANCH_WIDTH // LANES
    kern = functools.partial(_diff_kernel, tq=tq, lam_init=lam_init)
    return pl.pallas_call(
        kern,
        grid=(batch, N_HEADS, nq),
        in_specs=[
            pl.BlockSpec((tq, LANES), lambda b, h, i: (b * nq + i, T_DQ * hpt + h)),
            pl.BlockSpec((seq, LANES), lambda b, h, i: (b, T_DK * hpt + h)),
            pl.BlockSpec((seq, LANES), lambda b, h, i: (b, T_DV * hpt + h)),
            pl.BlockSpec((4, DIFF_SUB_DIM), lambda b, h, i: (0, 0)),
            pl.BlockSpec((HEAD_DIM, 1), lambda b, h, i: (0, 0)),
        ],
        out_specs=pl.BlockSpec((tq, LANES), lambda b, h, i: (b * nq + i, h)),
        out_shape=jax.ShapeDtypeStruct((T, BRANCH_WIDTH), BF16),
        scratch_shapes=[
            pltpu.VMEM((2 * tq, LANES), BF16),
            pltpu.VMEM((1, 2 * tq), F32),
            pltpu.VMEM((1, 2 * tq), F32),
            pltpu.VMEM((HEAD_DIM, 2 * tq), F32),
        ],
        compiler_params=_cparams(("parallel", "parallel", "arbitrary")),
        name="diff_attn",
    )(proj, proj, proj, diff_lambda, subln_col)


def _dsa_select_kernel(iq_ref, ikd_ref, iwt_ref, mask_ref, key_scr, *, tq, tc, nc, topk, idx_bits):
    qi = pl.program_id(1)
    n_ch = (qi * tq + tq + tc - 1) // tc
    iq = iq_ref[...]
    iwt = iwt_ref[...]
    lane = lax.broadcasted_iota(I32, (tq, LANES), 1)
    zero = jnp.zeros((tq, LANES), BF16)
    qm = []
    for h in range(IDX_HEADS):
        blk = iq[:, (h // 2) * LANES:(h // 2 + 1) * LANES]
        keep = (lane < IDX_DIM) if h % 2 == 0 else (lane >= IDX_DIM)
        qm.append(jnp.where(keep, blk, zero))
    idx_scale = IDX_DIM ** -0.5 * IDX_HEADS ** -0.5
    qpos = qi * tq + lax.broadcasted_iota(I32, (tc, tq), 1)
    krow = lax.broadcasted_iota(I32, (tc, tq), 0)
    krow8 = lax.broadcasted_iota(I32, (SLAB, tq), 0)

    def score_chunk(c, carry):
        ik = ikd_ref[pl.ds(pl.multiple_of(c * tc, tc), tc), :]
        sc = jnp.zeros((tc, tq), F32)
        for h in range(IDX_HEADS):
            sc = sc + iwt[h:h + 1, :] * jnp.maximum(_dot_nt(ik, qm[h]), 0.0)
        sc = sc * idx_scale
        sc = jnp.where(c * tc + krow <= qpos, sc, -jnp.inf)
        bits = pltpu.bitcast(sc, I32)
        key_scr[c] = jnp.where(bits < 0, INT_MIN - bits, bits)
        return carry

    lax.fori_loop(0, n_ch, score_chunk, 0)

    def count(pred):
        def body(c, acc):
            for r in range(tc // SLAB):
                ks = key_scr[c, r * SLAB:(r + 1) * SLAB, :]
                acc = acc + jnp.where(pred(ks, c * tc + r * SLAB), 1, 0)
            return acc
        acc = lax.fori_loop(0, n_ch, body, jnp.zeros((SLAB, tq), I32))
        return jnp.sum(acc, axis=0, keepdims=True)

    def rows8(x):
        return jnp.broadcast_to(x, (SLAB, tq))

    def count_ge(cand):
        c8 = rows8(cand)
        return count(lambda ks, base: ks >= c8)

    thr = jnp.where(count_ge(jnp.zeros((1, tq), I32)) >= topk, 0, INT_MIN).astype(I32)

    def bit_body(it, thr):
        cand = thr + jnp.left_shift(jnp.int32(1), 30 - it)
        return jnp.where(count_ge(cand) >= topk, cand, thr)

    thr = lax.fori_loop(0, 31, bit_body, thr)
    thr8 = rows8(thr)
    cnt_gt = count(lambda ks, base: ks > thr8)
    cnt_ge = count_ge(thr)
    need = topk - cnt_gt
    excess = jnp.max(cnt_ge - topk) > 0

    def tie_search(_):
        def tbit(it, p):
            cand = p + jnp.left_shift(jnp.int32(1), idx_bits - 1 - it)
            cand8 = rows8(cand)
            cnt = count(lambda ks, base: (ks == thr8) & (base + krow8 < cand8))
            return jnp.where(cnt < need, cand, p)
        return lax.fori_loop(0, idx_bits, tbit, jnp.zeros((1, tq), I32))

    last = lax.cond(excess, tie_search, lambda _: jnp.full((1, tq), 2 ** idx_bits, I32), 0)

    def emit(c, carry):
        kc = key_scr[c]
        kpos = c * tc + krow
        sel = (kc > thr) | ((kc == thr) & (kpos <= last))
        sel = sel & (kpos <= qpos)
        mask_ref[0, 0, c] = jnp.where(sel, 1, 0).astype(jnp.int8)
        return carry

    lax.fori_loop(0, n_ch, emit, 0)

    def clear(c, carry):
        mask_ref[0, 0, c] = jnp.zeros((tc, tq), jnp.int8)
        return carry

    lax.fori_loop(n_ch, nc, clear, 0)


def _dsa_tiles(seq):
    tq = _pick(seq, 256)
    tc = _pick(seq, 512)
    return tq, tc


def _dsa_select_call(proj, ikd, iw_t, batch, seq):
    tq, tc = _dsa_tiles(seq)
    nq, nc = seq // tq, seq // tc
    topk = min(TOPK_MAX, seq // 4)
    assert tc >= topk
    idx_bits = max(1, (seq - 1).bit_length())
    kern = functools.partial(_dsa_select_kernel, tq=tq, tc=tc, nc=nc, topk=topk, idx_bits=idx_bits)
    return pl.pallas_call(
        kern,
        grid=(batch, nq),
        in_specs=[
            pl.BlockSpec((tq, BRANCH_WIDTH), lambda b, i: (b * nq + i, T_IQ)),
            pl.BlockSpec((seq, LANES), lambda b, i: (b, 0)),
            pl.BlockSpec((IDX_HEADS, tq), lambda b, i: (0, b * nq + i)),
        ],
        out_specs=pl.BlockSpec((1, 1, nc, tc, tq), lambda b, i: (b, i, 0, 0, 0)),
        out_shape=jax.ShapeDtypeStruct((batch, nq, nc, tc, tq), jnp.int8),
        scratch_shapes=[pltpu.VMEM((nc, tc, tq), I32)],
        compiler_params=_cparams(("parallel", "arbitrary")),
        name="dsa_select",
    )(proj, ikd, iw_t)


def _dsa_attn_kernel(q_ref, k_ref, v_ref, mask_ref, o_ref, m_scr, l_scr, acc_scr, *, tq, tc, cpi):
    qi = pl.program_id(2)
    n_ch = (qi * tq + tq + tc - 1) // tc
    q = q_ref[...]
    scale = HEAD_DIM ** -0.5
    _init_softmax_state(m_scr, l_scr, acc_scr)

    def body(it, carry):
        parts = []
        for u in range(cpi):
            c = it * cpi + u
            off = pl.multiple_of(c * tc, tc)
            s_t = _dot_nt(k_ref[pl.ds(off, tc), :], q) * scale
            sel = mask_ref[0, 0, c].astype(I32) != 0
            parts.append((jnp.where(sel, s_t, NEG_BIG), v_ref[pl.ds(off, tc), :]))
        _softmax_step_t(parts, m_scr, l_scr, acc_scr)
        return carry

    lax.fori_loop(0, (n_ch + cpi - 1) // cpi, body, 0)
    o_ref[...] = (acc_scr[...] / l_scr[...]).T.astype(BF16)


def _dsa_attn_call(proj, mask, batch, seq):
    T = proj.shape[0]
    tq, tc = _dsa_tiles(seq)
    nq, nc = seq // tq, seq // tc
    hpt = BRANCH_WIDTH // LANES
    cpi = 2 if nc % 2 == 0 else 1
    kern = functools.partial(_dsa_attn_kernel, tq=tq, tc=tc, cpi=cpi)
    return pl.pallas_call(
        kern,
        grid=(batch, N_HEADS, nq),
        in_specs=[
            pl.BlockSpec((tq, LANES), lambda b, h, i: (b * nq + i, T_AQ * hpt + h)),
            pl.BlockSpec((seq, LANES), lambda b, h, i: (b, T_AK * hpt + h)),
            pl.BlockSpec((seq, LANES), lambda b, h, i: (b, T_AV * hpt + h)),
            pl.BlockSpec((1, 1, nc, tc, tq), lambda b, h, i: (b, i, 0, 0, 0)),
        ],
        out_specs=pl.BlockSpec((tq, LANES), lambda b, h, i: (b * nq + i, h)),
        out_shape=jax.ShapeDtypeStruct((T, BRANCH_WIDTH), BF16),
        scratch_shapes=[
            pltpu.VMEM((1, tq), F32),
            pltpu.VMEM((1, tq), F32),
            pltpu.VMEM((HEAD_DIM, tq), F32),
        ],
        compiler_params=_cparams(("parallel", "parallel", "arbitrary")),
        name="dsa_attn",
    )(proj, proj, proj, mask)


def _merge_kernel(cb_ref, cc_ref, ch_ref, ccp_ref, chp_ref, su_ref, sv_ref, yc_ref, yd_ref,
                  hn_ref, res_ref, convw_ref, sgun_ref, sguw_ref, sgub_ref,
                  wg0_ref, wg1_ref, wg2_ref, wg3_ref, wb_ref, wo_ref, fng_ref, router_ref,
                  h_ref, hn2_ref, ridx_ref, rw_ref, ycat_scr, acc_scr, *, tm, rows_per_seq, n_j, n_experts):
    i = pl.program_id(0)
    j = pl.program_id(1)

    @pl.when(j == 0)
    def _():
        acc_scr[...] = jnp.zeros(acc_scr.shape, F32)
        z = cc_ref[...].astype(F32) * ch_ref[...].astype(F32)
        first = ((i * tm) % rows_per_seq) == 0
        zp = ccp_ref[...].astype(F32) * chp_ref[...].astype(F32)
        zp = zp * jnp.where(first, 0.0, 1.0).astype(F32)
        row = lax.broadcasted_iota(I32, z.shape, 0)
        z1 = jnp.where(row == 0, zp[7:8, :], pltpu.roll(z, 1, 0))
        z2 = jnp.where(row == 0, zp[6:7, :], jnp.where(row == 1, zp[7:8, :], pltpu.roll(z, 2, 0)))
        cw = convw_ref[...]
        conv = cw[0:1, :] * z2 + cw[1:2, :] * z1 + cw[2:3, :] * z
        ycat_scr[:, 0:BRANCH_WIDTH] = (cb_ref[...].astype(F32) * conv).astype(BF16)
        u = _gelu_tanh(su_ref[...].astype(F32))
        v = _rmsnorm_rows(_gelu_tanh(sv_ref[...].astype(F32)), sgun_ref[...]).astype(BF16)
        tri_r = lax.broadcasted_iota(I32, (CHUNK, CHUNK), 0)
        tri_c = lax.broadcasted_iota(I32, (CHUNK, CHUNK), 1)
        bias = sgub_ref[...]
        for g in range(BRANCH_WIDTH // LANES):
            wg = jnp.where(tri_c <= tri_r, sguw_ref[g], 0.0).astype(BF16)
            cols = slice(g * LANES, (g + 1) * LANES)
            for ck in range(tm // CHUNK):
                rows = slice(ck * CHUNK, (ck + 1) * CHUNK)
                y = _dot(wg, v[rows, cols]) + bias[:, g:g + 1]
                ycat_scr[rows, BRANCH_WIDTH + g * LANES:BRANCH_WIDTH + (g + 1) * LANES] = (
                    u[rows, cols] * y).astype(BF16)
        ycat_scr[:, 2 * BRANCH_WIDTH:3 * BRANCH_WIDTH] = yc_ref[...]
        ycat_scr[:, 3 * BRANCH_WIDTH:4 * BRANCH_WIDTH] = yd_ref[...]

    hn = hn_ref[...]
    merged = None
    for n, wg_ref in enumerate((wg0_ref, wg1_ref, wg2_ref, wg3_ref)):
        gate = jax.nn.sigmoid(_dot(hn, wg_ref[...]))
        br = _dot(ycat_scr[:, n * BRANCH_WIDTH:(n + 1) * BRANCH_WIDTH], wb_ref[n])
        merged = gate * br if merged is None else merged + gate * br
    acc_scr[...] += _dot(merged.astype(BF16), wo_ref[...])

    @pl.when(j == n_j - 1)
    def _():
        h_new = res_ref[...] + acc_scr[...]
        h_ref[...] = h_new
        hn2 = _rmsnorm_rows(h_new, fng_ref[...])
        hb = hn2.astype(BF16)
        hn2_ref[...] = hb
        logits = _dot(hb, router_ref[...])
        lane = lax.broadcasted_iota(I32, logits.shape, 1)
        logits = jnp.where(lane < n_experts, logits, -jnp.inf)
        v1 = jnp.max(logits, axis=-1, keepdims=True)
        i1 = jnp.min(jnp.where(logits == v1, lane, LANES), axis=-1, keepdims=True)
        rest = jnp.where(lane == i1, -jnp.inf, logits)
        v2 = jnp.max(rest, axis=-1, keepdims=True)
        i2 = jnp.min(jnp.where(rest == v2, lane, LANES), axis=-1, keepdims=True)
        e2 = jnp.exp(v2 - v1)
        w1 = 1.0 / (1.0 + e2)
        w2 = e2 / (1.0 + e2)
        ridx_ref[...] = jnp.where(lane == 0, i1, jnp.where(lane == 1, i2, 0))
        rw_ref[...] = jnp.where(lane == 0, w1, jnp.where(lane == 1, w2, 0.0))


def _merge_call(proj, y_c, y_d, hn, h2, conv_w, sgu_norm, sgu_w, sgu_b_t, w_gate, w_branch, w_out,
                ffn_g, router_pad, seq, n_experts):
    T, D = h2.shape
    tm = _pick(seq, 256)
    tn = _pick(D, 256)
    n_j = D // tn
    rb8 = tm // 8
    kern = functools.partial(_merge_kernel, tm=tm, rows_per_seq=seq, n_j=n_j, n_experts=n_experts)
    tile = lambda t: pl.BlockSpec((tm, BRANCH_WIDTH), lambda i, j, t=t: (i, t))
    prev = lambda t: pl.BlockSpec((8, BRANCH_WIDTH), lambda i, j, t=t: (jnp.maximum(i * rb8 - 1, 0), t))
    full = lambda shp: pl.BlockSpec(shp, lambda i, j: (0,) * len(shp))
    gate = lambda n: pl.BlockSpec((D, tn), lambda i, j, n=n: (0, n * n_j + j))
    rows = lambda w: pl.BlockSpec((tm, w), lambda i, j: (i, 0))
    return pl.pallas_call(
        kern,
        grid=(T // tm, n_j),
        in_specs=[
            tile(T_CB), tile(T_CC), tile(T_CH), prev(T_CC), prev(T_CH), tile(T_SU), tile(T_SV),
            rows(BRANCH_WIDTH), rows(BRANCH_WIDTH), rows(D), rows(D),
            full((CONV_K, BRANCH_WIDTH)), full((1, BRANCH_WIDTH)),
            full((BRANCH_WIDTH // LANES, CHUNK, CHUNK)), full((CHUNK, BRANCH_WIDTH // LANES)),
            gate(0), gate(1), gate(2), gate(3),
            pl.BlockSpec((N_BRANCHES, BRANCH_WIDTH, tn), lambda i, j: (0, 0, j)),
            pl.BlockSpec((tn, D), lambda i, j: (j, 0)),
            full((1, D)), full((D, LANES)),
        ],
        out_specs=[rows(D), rows(D), rows(LANES), rows(LANES)],
        out_shape=[
            jax.ShapeDtypeStruct((T, D), F32),
            jax.ShapeDtypeStruct((T, D), BF16),
            jax.ShapeDtypeStruct((T, LANES), I32),
            jax.ShapeDtypeStruct((T, LANES), F32),
        ],
        scratch_shapes=[pltpu.VMEM((tm, N_BRANCHES * BRANCH_WIDTH), BF16), pltpu.VMEM((tm, D), F32)],
        compiler_params=_cparams(("parallel", "arbitrary")),
        name="merge",
    )(proj, proj, proj, proj, proj, proj, proj, y_c, y_d, hn, h2, conv_w, sgu_norm, sgu_w, sgu_b_t,
      w_gate, w_gate, w_gate, w_gate, w_branch, w_out, ffn_g, router_pad)


def _ffn_kernel(x_ref, comb_ref, w1_ref, w3_ref, w2_ref, res_ref, g_ref, o_ref, acc_e, acc_t,
                *, n_e, n_f, final_norm):
    e = pl.program_id(1)
    f = pl.program_id(2)

    @pl.when((e == 0) & (f == 0))
    def _():
        acc_t[...] = jnp.zeros(acc_t.shape, F32)

    @pl.when(f == 0)
    def _():
        acc_e[...] = jnp.zeros(acc_e.shape, F32)

    x = x_ref[...]
    h1 = _dot(x, w1_ref[...])
    h3 = _dot(x, w3_ref[...])
    a = (h1 * jax.nn.sigmoid(h1)) * h3
    acc_e[...] += _dot(a.astype(BF16), w2_ref[...])

    @pl.when(f == n_f - 1)
    def _():
        comb = comb_ref[...]
        lane = lax.broadcasted_iota(I32, comb.shape, 1)
        c = jnp.sum(jnp.where(lane == e, comb, 0.0), axis=-1, keepdims=True)
        acc_t[...] += c * acc_e[...]

    @pl.when((e == n_e - 1) & (f == n_f - 1))
    def _():
        out = res_ref[...] + acc_t[...]
        if final_norm:
            out = _rmsnorm_rows(out, g_ref[...])
        o_ref[...] = out


def _ffn_call(xb, comb, w1, w3, w2, res, g, final_norm):
    T, D = res.shape
    n_e, _, d_ff = w1.shape
    tm = _pick(T, 512)
    tf = _pick(d_ff, 512)
    n_f = d_ff // tf
    kern = functools.partial(_ffn_kernel, n_e=n_e, n_f=n_f, final_norm=final_norm)
    rows = lambda w: pl.BlockSpec((tm, w), lambda i, e, f: (i, 0))
    return pl.pallas_call(
        kern,
        grid=(T // tm, n_e, n_f),
        in_specs=[
            rows(D), rows(LANES),
            pl.BlockSpec((None, D, tf), lambda i, e, f: (e, 0, f)),
            pl.BlockSpec((None, D, tf), lambda i, e, f: (e, 0, f)),
            pl.BlockSpec((None, tf, D), lambda i, e, f: (e, f, 0)),
            rows(D),
            pl.BlockSpec((1, D), lambda i, e, f: (0, 0)),
        ],
        out_specs=rows(D),
        out_shape=jax.ShapeDtypeStruct((T, D), F32),
        scratch_shapes=[pltpu.VMEM((tm, D), F32), pltpu.VMEM((tm, D), F32)],
        compiler_params=_cparams(("parallel", "arbitrary", "arbitrary")),
        name="ffn",
    )(xb, comb, w1, w3, w2, res, g)


def _row_copy(src_hbm, row, dst_vmem, slot, sem):
    return pltpu.make_async_copy(src_hbm.at[pl.ds(row, 1), :], dst_vmem.at[pl.ds(slot, 1), :], sem)


def _moe_gather_kernel(tok_ref, h_hbm, g_ref, o_ref, buf, sem, *, tg):
    def start(r, c):
        _row_copy(h_hbm, tok_ref[0, 0, r], buf, r, sem).start()
        return c

    lax.fori_loop(0, tg, start, 0)

    def wait(r, c):
        _row_copy(h_hbm, tok_ref[0, 0, r], buf, r, sem).wait()
        return c

    lax.fori_loop(0, tg, wait, 0)
    o_ref[...] = _rmsnorm_rows(buf[...], g_ref[...]).astype(BF16)


def _moe_gather_call(h2, g, row_token, tg):
    T, D = h2.shape
    R = row_token.shape[0]
    kern = functools.partial(_moe_gather_kernel, tg=tg)
    return pl.pallas_call(
        kern,
        grid=(R // tg,),
        in_specs=[
            pl.BlockSpec((1, 1, tg), lambda i: (i, 0, 0), memory_space=pltpu.SMEM),
            pl.BlockSpec(memory_space=pl.ANY),
            pl.BlockSpec((1, D), lambda i: (0, 0)),
        ],
        out_specs=pl.BlockSpec((tg, D), lambda i: (i, 0)),
        out_shape=jax.ShapeDtypeStruct((R, D), BF16),
        scratch_shapes=[pltpu.VMEM((tg, D), F32), pltpu.SemaphoreType.DMA(())],
        compiler_params=_cparams(("arbitrary",)),
        name="moe_gather",
    )(row_token.reshape(R // tg, 1, tg), h2, g)


def _ffn_routed_kernel(te_ref, nu_ref, x_ref, rw_ref, w1_ref, w3_ref, w2_ref, o_ref, acc, *, n_f):
    i = pl.program_id(0)
    f = pl.program_id(1)

    @pl.when(f == 0)
    def _():
        acc[...] = jnp.zeros(acc.shape, F32)

    @pl.when(i < nu_ref[0])
    def _():
        x = x_ref[...]
        h1 = _dot(x, w1_ref[...])
        h3 = _dot(x, w3_ref[...])
        a = (h1 * jax.nn.sigmoid(h1)) * h3
        acc[...] += _dot(a.astype(BF16), w2_ref[...])

    @pl.when(f == n_f - 1)
    def _():
        o_ref[...] = rw_ref[...] * acc[...]


def _ffn_routed_call(xs, row_w, tile_e, n_used, w1, w3, w2, tm):
    R, D = xs.shape
    d_ff = w1.shape[-1]
    tf = _pick(d_ff, 512)
    n_f = d_ff // tf
    kern = functools.partial(_ffn_routed_kernel, n_f=n_f)

    def wmap(i, f, te, nu):
        return te[i], jnp.where(i < nu[0], f, n_f - 1)

    return pl.pallas_call(
        kern,
        grid_spec=pltpu.PrefetchScalarGridSpec(
            num_scalar_prefetch=2,
            grid=(R // tm, n_f),
            in_specs=[
                pl.BlockSpec((tm, D), lambda i, f, te, nu: (i, 0)),
                pl.BlockSpec((tm, 1), lambda i, f, te, nu: (i, 0)),
                pl.BlockSpec((None, D, tf), lambda i, f, te, nu: (wmap(i, f, te, nu)[0], 0, wmap(i, f, te, nu)[1])),
                pl.BlockSpec((None, D, tf), lambda i, f, te, nu: (wmap(i, f, te, nu)[0], 0, wmap(i, f, te, nu)[1])),
                pl.BlockSpec((None, tf, D), lambda i, f, te, nu: (wmap(i, f, te, nu)[0], wmap(i, f, te, nu)[1], 0)),
            ],
            out_specs=pl.BlockSpec((tm, D), lambda i, f, te, nu: (i, 0)),
            scratch_shapes=[pltpu.VMEM((tm, D), F32)],
        ),
        out_shape=jax.ShapeDtypeStruct((R, D), F32),
        compiler_params=_cparams(("arbitrary", "arbitrary")),
        name="ffn_routed",
    )(tile_e, n_used, xs, row_w, w1, w3, w2)


def _moe_combine_kernel(pos_ref, h_ref, y_hbm, g_ref, o_ref, buf0, buf1, sem, *, tc, final_norm):
    def start(r, c):
        _row_copy(y_hbm, pos_ref[0, 0, 2 * r], buf0, r, sem).start()
        _row_copy(y_hbm, pos_ref[0, 0, 2 * r + 1], buf1, r, sem).start()
        return c

    lax.fori_loop(0, tc, start, 0)

    def wait(r, c):
        _row_copy(y_hbm, pos_ref[0, 0, 2 * r], buf0, r, sem).wait()
        _row_copy(y_hbm, pos_ref[0, 0, 2 * r + 1], buf1, r, sem).wait()
        return c

    lax.fori_loop(0, tc, wait, 0)
    out = h_ref[...] + (buf0[...] + buf1[...])
    if final_norm:
        out = _rmsnorm_rows(out, g_ref[...])
    o_ref[...] = out


def _moe_combine_call(h2, ys, pos, g, final_norm):
    T, D = h2.shape
    tc = _pick(T, 256)
    kern = functools.partial(_moe_combine_kernel, tc=tc, final_norm=final_norm)
    return pl.pallas_call(
        kern,
        grid=(T // tc,),
        in_specs=[
            pl.BlockSpec((1, 1, 2 * tc), lambda i: (i, 0, 0), memory_space=pltpu.SMEM),
            pl.BlockSpec((tc, D), lambda i: (i, 0)),
            pl.BlockSpec(memory_space=pl.ANY),
            pl.BlockSpec((1, D), lambda i: (0, 0)),
        ],
        out_specs=pl.BlockSpec((tc, D), lambda i: (i, 0)),
        out_shape=jax.ShapeDtypeStruct((T, D), F32),
        scratch_shapes=[pltpu.VMEM((tc, D), F32), pltpu.VMEM((tc, D), F32), pltpu.SemaphoreType.DMA(())],
        compiler_params=_cparams(("arbitrary",)),
        name="moe_combine",
    )(pos.reshape(T // tc, 1, 2 * tc), h2, ys, g)


def _route(ridx, rw, n_experts, tm):
    T = ridx.shape[0]
    flat_e = ridx[:, :2].reshape(-1)
    onehot = (flat_e[:, None] == jnp.arange(n_experts, dtype=I32)[None, :]).astype(I32)
    cum = jnp.cumsum(onehot, axis=0)
    rank = jnp.take_along_axis(cum, flat_e[:, None], axis=1)[:, 0] - 1
    counts = cum[-1]
    padded = ((counts + tm - 1) // tm) * tm
    gend = jnp.cumsum(padded)
    gstart = gend - padded
    pos = gstart[flat_e] + rank
    R = 2 * T + n_experts * tm
    row_token = jnp.zeros((R,), I32).at[pos].set(jnp.arange(2 * T, dtype=I32) // 2)
    row_w = jnp.zeros((R,), F32).at[pos].set(rw[:, :2].reshape(-1))
    tile_e = jnp.searchsorted(gend, jnp.arange(R // tm, dtype=I32) * tm, side="right")
    tile_e = jnp.minimum(tile_e, n_experts - 1).astype(I32)
    n_used = (gend[-1:] // tm).astype(I32)
    return pos.astype(I32), row_token, row_w[:, None], tile_e, n_used


def _rope_tables(seq):
    pos = jnp.arange(seq, dtype=F32)

    def tab(dim):
        inv = ROPE_THETA ** (-jnp.arange(0, dim, 2, dtype=F32) / dim)
        ang = pos[:, None] * inv[None, :]
        cos = jnp.concatenate([jnp.cos(ang), jnp.cos(ang)], axis=-1)
        sin = jnp.concatenate([-jnp.sin(ang), jnp.sin(ang)], axis=-1)
        return cos, sin

    c64, s64 = tab(IDX_DIM)
    c128, s128 = tab(HEAD_DIM)
    return (jnp.tile(c64, (1, 2)), jnp.tile(s64, (1, 2)), c128, s128)


def kernel(x, attn_norm, w_in, conv_w, sgu_norm, sgu_w, sgu_b, diff_lambda, diff_subln, w_branch, w_out,
           ffn_norm, dense_w1, dense_w3, dense_w2, router, moe_w1, moe_w3, moe_w2, final_norm):
    batch, seq, D = x.shape
    depth = w_in.shape[0]
    n_experts = router.shape[-1]
    T = batch * seq
    rope = _rope_tables(seq)
    h = x.reshape(T, D)
    ik0 = MAIN_WIDTH
    iw0 = ik0 + IDX_DIM
    g0 = iw0 + IDX_HEADS
    ones_comb = jnp.zeros((T, LANES), F32).at[:, 0].set(1.0)

    for layer in range(depth):
        wl = w_in[layer]
        w_main = wl[:, :MAIN_WIDTH].astype(BF16)
        w_small = jnp.concatenate(
            [wl[:, ik0:iw0], wl[:, ik0:iw0], wl[:, iw0:g0], jnp.zeros((D, LANES - IDX_HEADS), F32)],
            axis=1).astype(BF16)
        w_gate = wl[:, g0:].astype(BF16)
        lam_init = 0.8 - 0.6 * math.exp(-0.3 * layer)

        proj, hn, ikd, iw = _proj_call(h, attn_norm[layer][None, :], w_main, w_small, rope, seq)
        y_c = _diff_call(proj, diff_lambda[layer], diff_subln[layer][:, None], lam_init, batch, seq)
        mask = _dsa_select_call(proj, ikd, iw[:, :IDX_HEADS].T, batch, seq)
        y_d = _dsa_attn_call(proj, mask, batch, seq)

        j = layer // 2
        is_moe = layer % 2 == 1
        router_pad = jnp.zeros((D, LANES), F32)
        if is_moe:
            router_pad = router_pad.at[:, :n_experts].set(router[j])
        h, hn2, ridx, rw = _merge_call(
            proj, y_c, y_d, hn, h, conv_w[layer], sgu_norm[layer][None, :], sgu_w[layer],
            sgu_b[layer].T, w_gate, w_branch[layer].astype(BF16), w_out[layer].astype(BF16),
            ffn_norm[layer][None, :], router_pad.astype(BF16), seq, n_experts)

        last = layer == depth - 1
        if is_moe:
            tm = _pick(T, 1024)
            pos, row_token, row_w, tile_e, n_used = _route(ridx, rw, n_experts, tm)
            xs = _moe_gather_call(h, ffn_norm[layer][None, :], row_token, _pick(tm, 512))
            ys = _ffn_routed_call(xs, row_w, tile_e, n_used, moe_w1[j].astype(BF16),
                                  moe_w3[j].astype(BF16), moe_w2[j].astype(BF16), tm)
            h = _moe_combine_call(h, ys, pos, final_norm[None, :], last)
        else:
            h = _ffn_call(hn2, ones_comb, dense_w1[j][None].astype(BF16), dense_w3[j][None].astype(BF16),
                          dense_w2[j][None].astype(BF16), h, final_norm[None, :], last)

    if depth == 0:
        raise ValueError("depth must be positive")
    return h.reshape(batch, seq, D)
```

```python
import functools
import math

import jax
import jax.numpy as jnp
from jax import lax
from jax.experimental import pallas as pl
from jax.experimental.pallas import tpu as pltpu

F32 = jnp.float32
BF16 = jnp.bfloat16
I32 = jnp.int32

LANES = 128
SLAB = 8
N_COUNTERS = 4
HEAD_DIM = 128
BRANCH_WIDTH = 512
N_BRANCHES = 4
N_HEADS = 4
IDX_HEADS = 8
IDX_DIM = 64
DIFF_SUB_DIM = 64
CONV_K = 3
CHUNK = 128
TOPK_MAX = 256
ROPE_THETA = 10000.0
EPS = 1e-6
N_MAIN_TILES = 12
MAIN_WIDTH = N_MAIN_TILES * BRANCH_WIDTH
(T_CB, T_CC, T_CH, T_SU, T_SV, T_DQ, T_DK, T_DV, T_AQ, T_AK, T_AV, T_IQ) = range(12)
NEG_BIG = -1e30
INT_MIN = -(2 ** 31)
VMEM_LIMIT = 56 * 1024 * 1024
MERGE_VMEM_LIMIT = 61 * 1024 * 1024


def _pick(n, pref):
    t = min(pref, n)
    while n % t:
        t //= 2
    return t


def _cparams(sem, vmem_limit=VMEM_LIMIT):
    return pltpu.CompilerParams(dimension_semantics=sem, vmem_limit_bytes=vmem_limit)


def _dot(a, b):
    return jnp.dot(a, b, preferred_element_type=F32)


def _dot_nt(a, b):
    return lax.dot_general(a, b, (((1,), (1,)), ((), ())), preferred_element_type=F32)


def _dot_tn(a, b):
    return lax.dot_general(a, b, (((0,), (0,)), ((), ())), preferred_element_type=F32)


def _rmsnorm_rows(x, g):
    ms = jnp.mean(x * x, axis=-1, keepdims=True)
    return (x * lax.rsqrt(ms + EPS)) * g


def _rope64(x, c, s):
    lane = lax.broadcasted_iota(I32, x.shape, 1)
    rot = jnp.where((lane & 32) == 0, pltpu.roll(x, 96, 1), pltpu.roll(x, 32, 1))
    return x * c + rot * s


def _rope128(x, c, s):
    return x * c + pltpu.roll(x, 64, 1) * s


def _gelu_tanh(x):
    c = math.sqrt(2.0 / math.pi)
    return x * (0.5 * (1.0 + jnp.tanh(c * (x + 0.044715 * (x * x * x)))))


def _proj_kernel(x_ref, g_ref, w_ref, ws_ref, c64_ref, s64_ref, c128_ref, s128_ref,
                 proj_ref, hn_ref, ikd_ref, iw_ref, hn_scr):
    j = pl.program_id(1)

    @pl.when(j == 0)
    def _():
        hb = _rmsnorm_rows(x_ref[...], g_ref[...]).astype(BF16)
        hn_scr[...] = hb
        hn_ref[...] = hb
        small = _dot(hb, ws_ref[...])
        ikd_ref[...] = _rope64(small[:, :LANES], c64_ref[...], s64_ref[...]).astype(BF16)
        iw_ref[...] = small[:, LANES:]

    acc = _dot(hn_scr[...], w_ref[...])
    is64 = (j == T_DQ) | (j == T_DK) | (j == T_IQ)
    is128 = (j == T_AQ) | (j == T_AK)

    @pl.when(jnp.logical_not(is64 | is128))
    def _():
        proj_ref[...] = acc.astype(BF16)

    @pl.when(is64)
    def _():
        scale = jnp.where(j == T_DQ, DIFF_SUB_DIM ** -0.5, 1.0).astype(F32)
        c, s = c64_ref[...], s64_ref[...]
        for cb in range(BRANCH_WIDTH // LANES):
            sl = slice(cb * LANES, (cb + 1) * LANES)
            proj_ref[:, sl] = (_rope64(acc[:, sl], c, s) * scale).astype(BF16)

    @pl.when(is128)
    def _():
        c, s = c128_ref[...], s128_ref[...]
        for cb in range(BRANCH_WIDTH // LANES):
            sl = slice(cb * LANES, (cb + 1) * LANES)
            proj_ref[:, sl] = _rope128(acc[:, sl], c, s).astype(BF16)


def _proj_call(h2, g, w_main, w_small, rope, seq):
    T, D = h2.shape
    tm = _pick(seq, 1024)
    nrb = seq // tm
    rope_spec = pl.BlockSpec((tm, LANES), lambda i, j: (i % nrb, 0))
    row_spec = lambda w: pl.BlockSpec((tm, w), lambda i, j: (i, 0))
    return pl.pallas_call(
        _proj_kernel,
        grid=(T // tm, N_MAIN_TILES),
        in_specs=[
            row_spec(D),
            pl.BlockSpec((1, D), lambda i, j: (0, 0)),
            pl.BlockSpec((D, BRANCH_WIDTH), lambda i, j: (0, j)),
            pl.BlockSpec((D, 2 * LANES), lambda i, j: (0, 0)),
            rope_spec, rope_spec, rope_spec, rope_spec,
        ],
        out_specs=[
            pl.BlockSpec((tm, BRANCH_WIDTH), lambda i, j: (i, j)),
            row_spec(D), row_spec(LANES), row_spec(LANES),
        ],
        out_shape=[
            jax.ShapeDtypeStruct((T, MAIN_WIDTH), BF16),
            jax.ShapeDtypeStruct((T, D), BF16),
            jax.ShapeDtypeStruct((T, LANES), BF16),
            jax.ShapeDtypeStruct((T, LANES), F32),
        ],
        scratch_shapes=[pltpu.VMEM((tm, D), BF16)],
        compiler_params=_cparams(("parallel", "arbitrary")),
        name="proj",
    )(h2, g, w_main, w_small, *rope)


def _softmax_step_t(parts, m_scr, l_scr, acc_scr):
    m_prev = m_scr[...]
    m_new = m_prev
    for s_t, _ in parts:
        m_new = jnp.maximum(m_new, jnp.max(s_t, axis=0, keepdims=True))
    alpha = jnp.exp(m_prev - m_new)
    l_new = alpha * l_scr[...]
    acc = alpha * acc_scr[...]
    for s_t, v in parts:
        p_t = jnp.exp(s_t - m_new)
        l_new = l_new + jnp.sum(p_t, axis=0, keepdims=True)
        acc = acc + _dot_tn(v, p_t.astype(BF16))
    l_scr[...] = l_new
    acc_scr[...] = acc
    m_scr[...] = m_new


def _init_softmax_state(m_scr, l_scr, acc_scr):
    m_scr[...] = jnp.full(m_scr.shape, NEG_BIG, F32)
    l_scr[...] = jnp.zeros(l_scr.shape, F32)
    acc_scr[...] = jnp.zeros(acc_scr.shape, F32)


def _pipelined_tiles(n, qk, sm):
    qk(0, 0)

    def pair(p, carry):
        qk(2 * p + 1, 1)
        sm(2 * p, 0, False)
        qk(2 * p + 2, 0)
        sm(2 * p + 1, 1, False)
        return carry

    n_pairs = (n - 1) // 2
    lax.fori_loop(0, n_pairs, pair, 0)
    rest = n - 2 * n_pairs

    @pl.when(rest == 1)
    def _():
        sm(n - 1, 0, True)

    @pl.when(rest == 2)
    def _():
        qk(n - 1, 1)
        sm(n - 2, 0, False)
        sm(n - 1, 1, True)


def _diff_kernel(q_ref, k_ref, v_ref, dl_ref, subln_ref, o_ref,
                 qq_scr, s_scr, m_scr, l_scr, acc_scr, *, tq, lam_init):
    qi = pl.program_id(2)
    q = q_ref[...]
    lane = lax.broadcasted_iota(I32, q.shape, 1)
    zero = jnp.zeros_like(q)
    qq_scr[0:tq, :] = jnp.where(lane < DIFF_SUB_DIM, q, zero)
    qq_scr[tq:2 * tq, :] = jnp.where(lane >= DIFF_SUB_DIM, q, zero)
    _init_softmax_state(m_scr, l_scr, acc_scr)

    def qk(kj, slot):
        off = pl.multiple_of(kj * tq, tq)
        s_scr[slot] = _dot_nt(k_ref[pl.ds(off, tq), :], qq_scr[...])

    def sm(kj, slot, causal):
        off = pl.multiple_of(kj * tq, tq)
        s_t = s_scr[slot]
        if causal:
            kpos = lax.broadcasted_iota(I32, s_t.shape, 0)
            qpos = lax.broadcasted_iota(I32, s_t.shape, 1)
            qpos = jnp.where(qpos >= tq, qpos - tq, qpos)
            s_t = jnp.where(kpos <= qpos, s_t, NEG_BIG)
        _softmax_step_t([(s_t, v_ref[pl.ds(off, tq), :])], m_scr, l_scr, acc_scr)

    _pipelined_tiles(qi + 1, qk, sm)

    dl = dl_ref[...]
    lam = (jnp.exp(jnp.sum(dl[0:1] * dl[1:2], axis=-1, keepdims=True))
           - jnp.exp(jnp.sum(dl[2:3] * dl[3:4], axis=-1, keepdims=True)) + lam_init)
    o_all = acc_scr[...] / l_scr[...]
    o_t = o_all[:, 0:tq] - lam * o_all[:, tq:2 * tq]
    ms = jnp.mean(o_t * o_t, axis=0, keepdims=True)
    o_t = (o_t * lax.rsqrt(ms + EPS)) * subln_ref[...] * (1.0 - lam_init)
    o_ref[...] = o_t.T.astype(BF16)


def _diff_call(proj, diff_lambda, subln_col, lam_init, batch, seq):
    T = proj.shape[0]
    tq = _pick(seq, 512)
    nq = seq // tq
    hpt = BRANCH_WIDTH // LANES
    kern = functools.partial(_diff_kernel, tq=tq, lam_init=lam_init)
    return pl.pallas_call(
        kern,
        grid=(batch, N_HEADS, nq),
        in_specs=[
            pl.BlockSpec((tq, LANES), lambda b, h, i: (b * nq + i, T_DQ * hpt + h)),
            pl.BlockSpec((seq, LANES), lambda b, h, i: (b, T_DK * hpt + h)),
            pl.BlockSpec((seq, LANES), lambda b, h, i: (b, T_DV * hpt + h)),
            pl.BlockSpec((4, DIFF_SUB_DIM), lambda b, h, i: (0, 0)),
            pl.BlockSpec((HEAD_DIM, 1), lambda b, h, i: (0, 0)),
        ],
        out_specs=pl.BlockSpec((tq, LANES), lambda b, h, i: (b * nq + i, h)),
        out_shape=jax.ShapeDtypeStruct((T, BRANCH_WIDTH), BF16),
        scratch_shapes=[
            pltpu.VMEM((2 * tq, LANES), BF16),
            pltpu.VMEM((2, tq, 2 * tq), F32),
            pltpu.VMEM((1, 2 * tq), F32),
            pltpu.VMEM((1, 2 * tq), F32),
            pltpu.VMEM((HEAD_DIM, 2 * tq), F32),
        ],
        compiler_params=_cparams(("parallel", "parallel", "arbitrary")),
        name="diff_attn",
    )(proj, proj, proj, diff_lambda, subln_col)


def _dsa_select_kernel(iq_ref, ikd_ref, iwt_ref, mask_ref, key_scr, *, tq, tc, nc, topk, idx_bits):
    qi = pl.program_id(1)
    n_ch = (qi * tq + tq + tc - 1) // tc
    iq = iq_ref[...]
    iwt = iwt_ref[...]
    lane = lax.broadcasted_iota(I32, (tq, LANES), 1)
    zero = jnp.zeros((tq, LANES), BF16)
    qm = []
    for h in range(IDX_HEADS):
        blk = iq[:, (h // 2) * LANES:(h // 2 + 1) * LANES]
        keep = (lane < IDX_DIM) if h % 2 == 0 else (lane >= IDX_DIM)
        qm.append(jnp.where(keep, blk, zero))
    idx_scale = IDX_DIM ** -0.5 * IDX_HEADS ** -0.5
    qpos = qi * tq + lax.broadcasted_iota(I32, (tc, tq), 1)
    krow = lax.broadcasted_iota(I32, (tc, tq), 0)
    krow8 = lax.broadcasted_iota(I32, (SLAB, tq), 0)

    def score_chunk(c, carry):
        ik = ikd_ref[pl.ds(pl.multiple_of(c * tc, tc), tc), :]
        sc = jnp.zeros((tc, tq), F32)
        for h in range(IDX_HEADS):
            sc = sc + iwt[h:h + 1, :] * jnp.maximum(_dot_nt(ik, qm[h]), 0.0)
        sc = sc * idx_scale
        sc = jnp.where(c * tc + krow <= qpos, sc, -jnp.inf)
        bits = pltpu.bitcast(sc, I32)
        key_scr[c] = jnp.where(bits < 0, INT_MIN - bits, bits)
        return carry

    lax.fori_loop(0, n_ch, score_chunk, 0)

    def count(pred):
        def body(c, accs):
            accs = list(accs)
            for r in range(tc // SLAB):
                ks = key_scr[c, r * SLAB:(r + 1) * SLAB, :]
                accs[r % N_COUNTERS] = accs[r % N_COUNTERS] + jnp.where(pred(ks, c * tc + r * SLAB), 1, 0)
            return tuple(accs)
        zero = jnp.zeros((SLAB, tq), I32)
        accs = lax.fori_loop(0, n_ch, body, (zero,) * N_COUNTERS)
        return jnp.sum(functools.reduce(lambda a, b: a + b, accs), axis=0, keepdims=True)

    def rows8(x):
        return jnp.broadcast_to(x, (SLAB, tq))

    def count_ge(cand):
        c8 = rows8(cand)
        return count(lambda ks, base: ks >= c8)

    thr = jnp.where(count_ge(jnp.zeros((1, tq), I32)) >= topk, 0, INT_MIN).astype(I32)

    def bit_body(it, thr):
        cand = thr + jnp.left_shift(jnp.int32(1), 30 - it)
        return jnp.where(count_ge(cand) >= topk, cand, thr)

    thr = lax.fori_loop(0, 31, bit_body, thr)
    thr8 = rows8(thr)
    cnt_gt = count(lambda ks, base: ks > thr8)
    cnt_ge = count_ge(thr)
    need = topk - cnt_gt
    excess = jnp.max(cnt_ge - topk) > 0

    def tie_search(_):
        def tbit(it, p):
            cand = p + jnp.left_shift(jnp.int32(1), idx_bits - 1 - it)
            cand8 = rows8(cand)
            cnt = count(lambda ks, base: (ks == thr8) & (base + krow8 < cand8))
            return jnp.where(cnt < need, cand, p)
        return lax.fori_loop(0, idx_bits, tbit, jnp.zeros((1, tq), I32))

    last = lax.cond(excess, tie_search, lambda _: jnp.full((1, tq), 2 ** idx_bits, I32), 0)

    def emit(c, carry):
        kc = key_scr[c]
        kpos = c * tc + krow
        sel = (kc > thr) | ((kc == thr) & (kpos <= last))
        sel = sel & (kpos <= qpos)
        mask_ref[0, 0, c] = jnp.where(sel, 1, 0).astype(jnp.int8)
        return carry

    lax.fori_loop(0, n_ch, emit, 0)

    def clear(c, carry):
        mask_ref[0, 0, c] = jnp.zeros((tc, tq), jnp.int8)
        return carry

    lax.fori_loop(n_ch, nc, clear, 0)


def _dsa_tiles(seq):
    tq = _pick(seq, 256)
    tc = _pick(seq, 512)
    return tq, tc


def _dsa_select_call(proj, ikd, iw_t, batch, seq):
    tq, tc = _dsa_tiles(seq)
    nq, nc = seq // tq, seq // tc
    topk = min(TOPK_MAX, seq // 4)
    assert tc >= topk
    idx_bits = max(1, (seq - 1).bit_length())
    kern = functools.partial(_dsa_select_kernel, tq=tq, tc=tc, nc=nc, topk=topk, idx_bits=idx_bits)
    return pl.pallas_call(
        kern,
        grid=(batch, nq),
        in_specs=[
            pl.BlockSpec((tq, BRANCH_WIDTH), lambda b, i: (b * nq + i, T_IQ)),
            pl.BlockSpec((seq, LANES), lambda b, i: (b, 0)),
            pl.BlockSpec((IDX_HEADS, tq), lambda b, i: (0, b * nq + i)),
        ],
        out_specs=pl.BlockSpec((1, 1, nc, tc, tq), lambda b, i: (b, i, 0, 0, 0)),
        out_shape=jax.ShapeDtypeStruct((batch, nq, nc, tc, tq), jnp.int8),
        scratch_shapes=[pltpu.VMEM((nc, tc, tq), I32)],
        compiler_params=_cparams(("parallel", "arbitrary")),
        name="dsa_select",
    )(proj, ikd, iw_t)


def _dsa_attn_kernel(q_ref, k_ref, v_ref, mask_ref, o_ref, s_scr, m_scr, l_scr, acc_scr, *, tq, tc, cpi):
    qi = pl.program_id(2)
    n_ch = (qi * tq + tq + tc - 1) // tc
    q = q_ref[...]
    scale = HEAD_DIM ** -0.5
    _init_softmax_state(m_scr, l_scr, acc_scr)

    def qk(it, slot):
        for u in range(cpi):
            c = it * cpi + u
            off = pl.multiple_of(c * tc, tc)
            s_t = _dot_nt(k_ref[pl.ds(off, tc), :], q) * scale
            sel = mask_ref[0, 0, c].astype(I32) != 0
            s_scr[slot, u] = jnp.where(sel, s_t, NEG_BIG)

    def sm(it, slot, last):
        parts = []
        for u in range(cpi):
            off = pl.multiple_of((it * cpi + u) * tc, tc)
            parts.append((s_scr[slot, u], v_ref[pl.ds(off, tc), :]))
        _softmax_step_t(parts, m_scr, l_scr, acc_scr)

    _pipelined_tiles((n_ch + cpi - 1) // cpi, qk, sm)
    o_ref[...] = (acc_scr[...] / l_scr[...]).T.astype(BF16)


def _dsa_attn_call(proj, mask, batch, seq):
    T = proj.shape[0]
    tq, tc = _dsa_tiles(seq)
    nq, nc = seq // tq, seq // tc
    hpt = BRANCH_WIDTH // LANES
    cpi = 2 if nc % 2 == 0 else 1
    kern = functools.partial(_dsa_attn_kernel, tq=tq, tc=tc, cpi=cpi)
    return pl.pallas_call(
        kern,
        grid=(batch, N_HEADS, nq),
        in_specs=[
            pl.BlockSpec((tq, LANES), lambda b, h, i: (b * nq + i, T_AQ * hpt + h)),
            pl.BlockSpec((seq, LANES), lambda b, h, i: (b, T_AK * hpt + h)),
            pl.BlockSpec((seq, LANES), lambda b, h, i: (b, T_AV * hpt + h)),
            pl.BlockSpec((1, 1, nc, tc, tq), lambda b, h, i: (b, i, 0, 0, 0)),
        ],
        out_specs=pl.BlockSpec((tq, LANES), lambda b, h, i: (b * nq + i, h)),
        out_shape=jax.ShapeDtypeStruct((T, BRANCH_WIDTH), BF16),
        scratch_shapes=[
            pltpu.VMEM((2, cpi, tc, tq), F32),
            pltpu.VMEM((1, tq), F32),
            pltpu.VMEM((1, tq), F32),
            pltpu.VMEM((HEAD_DIM, tq), F32),
        ],
        compiler_params=_cparams(("parallel", "parallel", "arbitrary")),
        name="dsa_attn",
    )(proj, proj, proj, mask)


def _merge_kernel(cb_ref, cc_ref, ch_ref, ccp_ref, chp_ref, su_ref, sv_ref, yc_ref, yd_ref,
                  hn_ref, res_ref, convw_ref, sgun_ref, sguw_ref, sgub_ref,
                  wg0_ref, wg1_ref, wg2_ref, wg3_ref, wb_ref, wo_ref, fng_ref, router_ref,
                  h_ref, hn2_ref, ridx_ref, rw_ref, ycat_scr, acc_scr, *, tm, rows_per_seq, n_j, n_experts):
    i = pl.program_id(0)
    j = pl.program_id(1)

    @pl.when(j == 0)
    def _():
        acc_scr[...] = jnp.zeros(acc_scr.shape, F32)
        z = cc_ref[...].astype(F32) * ch_ref[...].astype(F32)
        first = ((i * tm) % rows_per_seq) == 0
        zp = ccp_ref[...].astype(F32) * chp_ref[...].astype(F32)
        zp = zp * jnp.where(first, 0.0, 1.0).astype(F32)
        row = lax.broadcasted_iota(I32, z.shape, 0)
        z1 = jnp.where(row == 0, zp[7:8, :], pltpu.roll(z, 1, 0))
        z2 = jnp.where(row == 0, zp[6:7, :], jnp.where(row == 1, zp[7:8, :], pltpu.roll(z, 2, 0)))
        cw = convw_ref[...]
        conv = cw[0:1, :] * z2 + cw[1:2, :] * z1 + cw[2:3, :] * z
        ycat_scr[:, 0:BRANCH_WIDTH] = (cb_ref[...].astype(F32) * conv).astype(BF16)
        u = _gelu_tanh(su_ref[...].astype(F32))
        v = _rmsnorm_rows(_gelu_tanh(sv_ref[...].astype(F32)), sgun_ref[...]).astype(BF16)
        tri_r = lax.broadcasted_iota(I32, (CHUNK, CHUNK), 0)
        tri_c = lax.broadcasted_iota(I32, (CHUNK, CHUNK), 1)
        bias = sgub_ref[...]
        for g in range(BRANCH_WIDTH // LANES):
            wg = jnp.where(tri_c <= tri_r, sguw_ref[g], 0.0).astype(BF16)
            cols = slice(g * LANES, (g + 1) * LANES)
            for ck in range(tm // CHUNK):
                rows = slice(ck * CHUNK, (ck + 1) * CHUNK)
                y = _dot(wg, v[rows, cols]) + bias[:, g:g + 1]
                ycat_scr[rows, BRANCH_WIDTH + g * LANES:BRANCH_WIDTH + (g + 1) * LANES] = (
                    u[rows, cols] * y).astype(BF16)
        ycat_scr[:, 2 * BRANCH_WIDTH:3 * BRANCH_WIDTH] = yc_ref[...]
        ycat_scr[:, 3 * BRANCH_WIDTH:4 * BRANCH_WIDTH] = yd_ref[...]

    hn = hn_ref[...]
    merged = None
    for n, wg_ref in enumerate((wg0_ref, wg1_ref, wg2_ref, wg3_ref)):
        gate = jax.nn.sigmoid(_dot(hn, wg_ref[...]))
        br = _dot(ycat_scr[:, n * BRANCH_WIDTH:(n + 1) * BRANCH_WIDTH], wb_ref[n])
        merged = gate * br if merged is None else merged + gate * br
    acc_scr[...] += _dot(merged.astype(BF16), wo_ref[...])

    @pl.when(j == n_j - 1)
    def _():
        h_new = res_ref[...] + acc_scr[...]
        h_ref[...] = h_new
        hn2 = _rmsnorm_rows(h_new, fng_ref[...])
        hb = hn2.astype(BF16)
        hn2_ref[...] = hb
        logits = _dot(hb, router_ref[...])
        lane = lax.broadcasted_iota(I32, logits.shape, 1)
        logits = jnp.where(lane < n_experts, logits, -jnp.inf)
        v1 = jnp.max(logits, axis=-1, keepdims=True)
        i1 = jnp.min(jnp.where(logits == v1, lane, LANES), axis=-1, keepdims=True)
        rest = jnp.where(lane == i1, -jnp.inf, logits)
        v2 = jnp.max(rest, axis=-1, keepdims=True)
        i2 = jnp.min(jnp.where(rest == v2, lane, LANES), axis=-1, keepdims=True)
        e2 = jnp.exp(v2 - v1)
        w1 = 1.0 / (1.0 + e2)
        w2 = e2 / (1.0 + e2)
        ridx_ref[...] = jnp.where(lane == 0, i1, jnp.where(lane == 1, i2, 0))
        rw_ref[...] = jnp.where(lane == 0, w1, jnp.where(lane == 1, w2, 0.0))


def _merge_call(proj, y_c, y_d, hn, h2, conv_w, sgu_norm, sgu_w, sgu_b_t, w_gate, w_branch, w_out,
                ffn_g, router_pad, seq, n_experts):
    T, D = h2.shape
    tm = _pick(seq, 512)
    tn = _pick(D, 256)
    n_j = D // tn
    rb8 = tm // 8
    kern = functools.partial(_merge_kernel, tm=tm, rows_per_seq=seq, n_j=n_j, n_experts=n_experts)
    tile = lambda t: pl.BlockSpec((tm, BRANCH_WIDTH), lambda i, j, t=t: (i, t))
    prev = lambda t: pl.BlockSpec((8, BRANCH_WIDTH), lambda i, j, t=t: (jnp.maximum(i * rb8 - 1, 0), t))
    full = lambda shp: pl.BlockSpec(shp, lambda i, j: (0,) * len(shp))
    gate = lambda n: pl.BlockSpec((D, tn), lambda i, j, n=n: (0, n * n_j + j))
    rows = lambda w: pl.BlockSpec((tm, w), lambda i, j: (i, 0))
    return pl.pallas_call(
        kern,
        grid=(T // tm, n_j),
        in_specs=[
            tile(T_CB), tile(T_CC), tile(T_CH), prev(T_CC), prev(T_CH), tile(T_SU), tile(T_SV),
            rows(BRANCH_WIDTH), rows(BRANCH_WIDTH), rows(D), rows(D),
            full((CONV_K, BRANCH_WIDTH)), full((1, BRANCH_WIDTH)),
            full((BRANCH_WIDTH // LANES, CHUNK, CHUNK)), full((CHUNK, BRANCH_WIDTH // LANES)),
            gate(0), gate(1), gate(2), gate(3),
            pl.BlockSpec((N_BRANCHES, BRANCH_WIDTH, tn), lambda i, j: (0, 0, j)),
            pl.BlockSpec((tn, D), lambda i, j: (j, 0)),
            full((1, D)), full((D, LANES)),
        ],
        out_specs=[rows(D), rows(D), rows(LANES), rows(LANES)],
        out_shape=[
            jax.ShapeDtypeStruct((T, D), F32),
            jax.ShapeDtypeStruct((T, D), BF16),
            jax.ShapeDtypeStruct((T, LANES), I32),
            jax.ShapeDtypeStruct((T, LANES), F32),
        ],
        scratch_shapes=[pltpu.VMEM((tm, N_BRANCHES * BRANCH_WIDTH), BF16), pltpu.VMEM((tm, D), F32)],
        compiler_params=_cparams(("parallel", "arbitrary"), MERGE_VMEM_LIMIT),
        name="merge",
    )(proj, proj, proj, proj, proj, proj, proj, y_c, y_d, hn, h2, conv_w, sgu_norm, sgu_w, sgu_b_t,
      w_gate, w_gate, w_gate, w_gate, w_branch, w_out, ffn_g, router_pad)


def _ffn_kernel(x_ref, comb_ref, w1_ref, w3_ref, w2_ref, res_ref, g_ref, o_ref, acc_e, acc_t,
                *, n_e, n_f, final_norm):
    e = pl.program_id(1)
    f = pl.program_id(2)

    @pl.when((e == 0) & (f == 0))
    def _():
        acc_t[...] = jnp.zeros(acc_t.shape, F32)

    @pl.when(f == 0)
    def _():
        acc_e[...] = jnp.zeros(acc_e.shape, F32)

    x = x_ref[...]
    h1 = _dot(x, w1_ref[...])
    h3 = _dot(x, w3_ref[...])
    a = (h1 * jax.nn.sigmoid(h1)) * h3
    acc_e[...] += _dot(a.astype(BF16), w2_ref[...])

    @pl.when(f == n_f - 1)
    def _():
        comb = comb_ref[...]
        lane = lax.broadcasted_iota(I32, comb.shape, 1)
        c = jnp.sum(jnp.where(lane == e, comb, 0.0), axis=-1, keepdims=True)
        acc_t[...] += c * acc_e[...]

    @pl.when((e == n_e - 1) & (f == n_f - 1))
    def _():
        out = res_ref[...] + acc_t[...]
        if final_norm:
            out = _rmsnorm_rows(out, g_ref[...])
        o_ref[...] = out


def _ffn_call(xb, comb, w1, w3, w2, res, g, final_norm):
    T, D = res.shape
    n_e, _, d_ff = w1.shape
    tm = _pick(T, 512)
    tf = _pick(d_ff, 512)
    n_f = d_ff // tf
    kern = functools.partial(_ffn_kernel, n_e=n_e, n_f=n_f, final_norm=final_norm)
    rows = lambda w: pl.BlockSpec((tm, w), lambda i, e, f: (i, 0))
    return pl.pallas_call(
        kern,
        grid=(T // tm, n_e, n_f),
        in_specs=[
            rows(D), rows(LANES),
            pl.BlockSpec((None, D, tf), lambda i, e, f: (e, 0, f)),
            pl.BlockSpec((None, D, tf), lambda i, e, f: (e, 0, f)),
            pl.BlockSpec((None, tf, D), lambda i, e, f: (e, f, 0)),
            rows(D),
            pl.BlockSpec((1, D), lambda i, e, f: (0, 0)),
        ],
        out_specs=rows(D),
        out_shape=jax.ShapeDtypeStruct((T, D), F32),
        scratch_shapes=[pltpu.VMEM((tm, D), F32), pltpu.VMEM((tm, D), F32)],
        compiler_params=_cparams(("parallel", "arbitrary", "arbitrary")),
        name="ffn",
    )(xb, comb, w1, w3, w2, res, g)


def _row_copy(src_hbm, row, dst_vmem, slot, sem):
    return pltpu.make_async_copy(src_hbm.at[pl.ds(row, 1), :], dst_vmem.at[pl.ds(slot, 1), :], sem)


def _moe_gather_kernel(tok_ref, h_hbm, g_ref, o_ref, buf, sem, *, tg):
    def start(r, c):
        _row_copy(h_hbm, tok_ref[0, 0, r], buf, r, sem).start()
        return c

    lax.fori_loop(0, tg, start, 0)

    def wait(r, c):
        _row_copy(h_hbm, tok_ref[0, 0, r], buf, r, sem).wait()
        return c

    lax.fori_loop(0, tg, wait, 0)
    o_ref[...] = _rmsnorm_rows(buf[...], g_ref[...]).astype(BF16)


def _moe_gather_call(h2, g, row_token, tg):
    T, D = h2.shape
    R = row_token.shape[0]
    kern = functools.partial(_moe_gather_kernel, tg=tg)
    return pl.pallas_call(
        kern,
        grid=(R // tg,),
        in_specs=[
            pl.BlockSpec((1, 1, tg), lambda i: (i, 0, 0), memory_space=pltpu.SMEM),
            pl.BlockSpec(memory_space=pl.ANY),
            pl.BlockSpec((1, D), lambda i: (0, 0)),
        ],
        out_specs=pl.BlockSpec((tg, D), lambda i: (i, 0)),
        out_shape=jax.ShapeDtypeStruct((R, D), BF16),
        scratch_shapes=[pltpu.VMEM((tg, D), F32), pltpu.SemaphoreType.DMA(())],
        compiler_params=_cparams(("arbitrary",)),
        name="moe_gather",
    )(row_token.reshape(R // tg, 1, tg), h2, g)


def _ffn_routed_kernel(te_ref, nu_ref, x_ref, rw_ref, w1_ref, w3_ref, w2_ref, o_ref, acc, *, n_f):
    i = pl.program_id(0)
    f = pl.program_id(1)

    @pl.when(f == 0)
    def _():
        acc[...] = jnp.zeros(acc.shape, F32)

    @pl.when(i < nu_ref[0])
    def _():
        x = x_ref[...]
        h1 = _dot(x, w1_ref[...])
        h3 = _dot(x, w3_ref[...])
        a = (h1 * jax.nn.sigmoid(h1)) * h3
        acc[...] += _dot(a.astype(BF16), w2_ref[...])

    @pl.when(f == n_f - 1)
    def _():
        o_ref[...] = rw_ref[...] * acc[...]


def _ffn_routed_call(xs, row_w, tile_e, n_used, w1, w3, w2, tm):
    R, D = xs.shape
    d_ff = w1.shape[-1]
    tf = _pick(d_ff, 512)
    n_f = d_ff // tf
    kern = functools.partial(_ffn_routed_kernel, n_f=n_f)

    def wmap(i, f, te, nu):
        return te[i], jnp.where(i < nu[0], f, n_f - 1)

    return pl.pallas_call(
        kern,
        grid_spec=pltpu.PrefetchScalarGridSpec(
            num_scalar_prefetch=2,
            grid=(R // tm, n_f),
            in_specs=[
                pl.BlockSpec((tm, D), lambda i, f, te, nu: (i, 0)),
                pl.BlockSpec((tm, 1), lambda i, f, te, nu: (i, 0)),
                pl.BlockSpec((None, D, tf), lambda i, f, te, nu: (wmap(i, f, te, nu)[0], 0, wmap(i, f, te, nu)[1])),
                pl.BlockSpec((None, D, tf), lambda i, f, te, nu: (wmap(i, f, te, nu)[0], 0, wmap(i, f, te, nu)[1])),
                pl.BlockSpec((None, tf, D), lambda i, f, te, nu: (wmap(i, f, te, nu)[0], wmap(i, f, te, nu)[1], 0)),
            ],
            out_specs=pl.BlockSpec((tm, D), lambda i, f, te, nu: (i, 0)),
            scratch_shapes=[pltpu.VMEM((tm, D), F32)],
        ),
        out_shape=jax.ShapeDtypeStruct((R, D), F32),
        compiler_params=_cparams(("arbitrary", "arbitrary")),
        name="ffn_routed",
    )(tile_e, n_used, xs, row_w, w1, w3, w2)


def _moe_combine_kernel(pos_ref, h_ref, y_hbm, g_ref, o_ref, buf0, buf1, sem, *, tc, final_norm):
    def start(r, c):
        _row_copy(y_hbm, pos_ref[0, 0, 2 * r], buf0, r, sem).start()
        _row_copy(y_hbm, pos_ref[0, 0, 2 * r + 1], buf1, r, sem).start()
        return c

    lax.fori_loop(0, tc, start, 0)

    def wait(r, c):
        _row_copy(y_hbm, pos_ref[0, 0, 2 * r], buf0, r, sem).wait()
        _row_copy(y_hbm, pos_ref[0, 0, 2 * r + 1], buf1, r, sem).wait()
        return c

    lax.fori_loop(0, tc, wait, 0)
    out = h_ref[...] + (buf0[...] + buf1[...])
    if final_norm:
        out = _rmsnorm_rows(out, g_ref[...])
    o_ref[...] = out


def _moe_combine_call(h2, ys, pos, g, final_norm):
    T, D = h2.shape
    tc = _pick(T, 256)
    kern = functools.partial(_moe_combine_kernel, tc=tc, final_norm=final_norm)
    return pl.pallas_call(
        kern,
        grid=(T // tc,),
        in_specs=[
            pl.BlockSpec((1, 1, 2 * tc), lambda i: (i, 0, 0), memory_space=pltpu.SMEM),
            pl.BlockSpec((tc, D), lambda i: (i, 0)),
            pl.BlockSpec(memory_space=pl.ANY),
            pl.BlockSpec((1, D), lambda i: (0, 0)),
        ],
        out_specs=pl.BlockSpec((tc, D), lambda i: (i, 0)),
        out_shape=jax.ShapeDtypeStruct((T, D), F32),
        scratch_shapes=[pltpu.VMEM((tc, D), F32), pltpu.VMEM((tc, D), F32), pltpu.SemaphoreType.DMA(())],
        compiler_params=_cparams(("arbitrary",)),
        name="moe_combine",
    )(pos.reshape(T // tc, 1, 2 * tc), h2, ys, g)


def _route(ridx, rw, n_experts, tm):
    T = ridx.shape[0]
    flat_e = ridx[:, :2].reshape(-1)
    onehot = (flat_e[:, None] == jnp.arange(n_experts, dtype=I32)[None, :]).astype(I32)
    cum = jnp.cumsum(onehot, axis=0)
    rank = jnp.take_along_axis(cum, flat_e[:, None], axis=1)[:, 0] - 1
    counts = cum[-1]
    padded = ((counts + tm - 1) // tm) * tm
    gend = jnp.cumsum(padded)
    gstart = gend - padded
    pos = gstart[flat_e] + rank
    R = 2 * T + n_experts * tm
    row_token = jnp.zeros((R,), I32).at[pos].set(jnp.arange(2 * T, dtype=I32) // 2)
    row_w = jnp.zeros((R,), F32).at[pos].set(rw[:, :2].reshape(-1))
    tile_e = jnp.searchsorted(gend, jnp.arange(R // tm, dtype=I32) * tm, side="right")
    tile_e = jnp.minimum(tile_e, n_experts - 1).astype(I32)
    n_used = (gend[-1:] // tm).astype(I32)
    return pos.astype(I32), row_token, row_w[:, None], tile_e, n_used


def _rope_tables(seq):
    pos = jnp.arange(seq, dtype=F32)

    def tab(dim):
        inv = ROPE_THETA ** (-jnp.arange(0, dim, 2, dtype=F32) / dim)
        ang = pos[:, None] * inv[None, :]
        cos = jnp.concatenate([jnp.cos(ang), jnp.cos(ang)], axis=-1)
        sin = jnp.concatenate([-jnp.sin(ang), jnp.sin(ang)], axis=-1)
        return cos, sin

    c64, s64 = tab(IDX_DIM)
    c128, s128 = tab(HEAD_DIM)
    return (jnp.tile(c64, (1, 2)), jnp.tile(s64, (1, 2)), c128, s128)


def kernel(x, attn_norm, w_in, conv_w, sgu_norm, sgu_w, sgu_b, diff_lambda, diff_subln, w_branch, w_out,
           ffn_norm, dense_w1, dense_w3, dense_w2, router, moe_w1, moe_w3, moe_w2, final_norm):
    batch, seq, D = x.shape
    depth = w_in.shape[0]
    n_experts = router.shape[-1]
    T = batch * seq
    rope = _rope_tables(seq)
    h = x.reshape(T, D)
    ik0 = MAIN_WIDTH
    iw0 = ik0 + IDX_DIM
    g0 = iw0 + IDX_HEADS
    ones_comb = jnp.zeros((T, LANES), F32).at[:, 0].set(1.0)

    for layer in range(depth):
        wl = w_in[layer]
        w_main = wl[:, :MAIN_WIDTH].astype(BF16)
        w_small = jnp.concatenate(
            [wl[:, ik0:iw0], wl[:, ik0:iw0], wl[:, iw0:g0], jnp.zeros((D, LANES - IDX_HEADS), F32)],
            axis=1).astype(BF16)
        w_gate = wl[:, g0:].astype(BF16)
        lam_init = 0.8 - 0.6 * math.exp(-0.3 * layer)

        proj, hn, ikd, iw = _proj_call(h, attn_norm[layer][None, :], w_main, w_small, rope, seq)
        y_c = _diff_call(proj, diff_lambda[layer], diff_subln[layer][:, None], lam_init, batch, seq)
        mask = _dsa_select_call(proj, ikd, iw[:, :IDX_HEADS].T, batch, seq)
        y_d = _dsa_attn_call(proj, mask, batch, seq)

        j = layer // 2
        is_moe = layer % 2 == 1
        router_pad = jnp.zeros((D, LANES), F32)
        if is_moe:
            router_pad = router_pad.at[:, :n_experts].set(router[j])
        h, hn2, ridx, rw = _merge_call(
            proj, y_c, y_d, hn, h, conv_w[layer], sgu_norm[layer][None, :], sgu_w[layer],
            sgu_b[layer].T, w_gate, w_branch[layer].astype(BF16), w_out[layer].astype(BF16),
            ffn_norm[layer][None, :], router_pad.astype(BF16), seq, n_experts)

        last = layer == depth - 1
        if is_moe:
            tm = _pick(T, 1024)
            pos, row_token, row_w, tile_e, n_used = _route(ridx, rw, n_experts, tm)
            xs = _moe_gather_call(h, ffn_norm[layer][None, :], row_token, _pick(tm, 512))
            ys = _ffn_routed_call(xs, row_w, tile_e, n_used, moe_w1[j].astype(BF16),
                                  moe_w3[j].astype(BF16), moe_w2[j].astype(BF16), tm)
            h = _moe_combine_call(h, ys, pos, final_norm[None, :], last)
        else:
            h = _ffn_call(hn2, ones_comb, dense_w1[j][None].astype(BF16), dense_w3[j][None].astype(BF16),
                          dense_w2[j][None].astype(BF16), h, final_norm[None, :], last)

    if depth == 0:
        raise ValueError("depth must be positive")
    return h.reshape(batch, seq, D)
```

```python
import functools
import math

import jax
import jax.numpy as jnp
from jax import lax
from jax.experimental import pallas as pl
from jax.experimental.pallas import tpu as pltpu

F32 = jnp.float32
BF16 = jnp.bfloat16
I32 = jnp.int32

LANES = 128
SLAB = 8
N_COUNTERS = 4
HEAD_DIM = 128
BRANCH_WIDTH = 512
N_BRANCHES = 4
N_HEADS = 4
IDX_HEADS = 8
IDX_DIM = 64
DIFF_SUB_DIM = 64
CONV_K = 3
CHUNK = 128
TOPK_MAX = 256
ROPE_THETA = 10000.0
EPS = 1e-6
N_MAIN_TILES = 12
MAIN_WIDTH = N_MAIN_TILES * BRANCH_WIDTH
(T_CB, T_CC, T_CH, T_SU, T_SV, T_DQ, T_DK, T_DV, T_AQ, T_AK, T_AV, T_IQ) = range(12)
NEG_BIG = -1e30
LOG2E = 1.4426950408889634
INT_MIN = -(2 ** 31)
VMEM_LIMIT = 56 * 1024 * 1024
MERGE_VMEM_LIMIT = 61 * 1024 * 1024


def _pick(n, pref):
    t = min(pref, n)
    while n % t:
        t //= 2
    return t


def _cparams(sem, vmem_limit=VMEM_LIMIT):
    return pltpu.CompilerParams(dimension_semantics=sem, vmem_limit_bytes=vmem_limit)


def _dot(a, b):
    return jnp.dot(a, b, preferred_element_type=F32)


def _dot_nt(a, b):
    return lax.dot_general(a, b, (((1,), (1,)), ((), ())), preferred_element_type=F32)


def _dot_tn(a, b):
    return lax.dot_general(a, b, (((0,), (0,)), ((), ())), preferred_element_type=F32)


def _rmsnorm_rows(x, g):
    ms = jnp.mean(x * x, axis=-1, keepdims=True)
    return (x * lax.rsqrt(ms + EPS)) * g


def _rope64(x, c, s):
    lane = lax.broadcasted_iota(I32, x.shape, 1)
    rot = jnp.where((lane & 32) == 0, pltpu.roll(x, 96, 1), pltpu.roll(x, 32, 1))
    return x * c + rot * s


def _rope128(x, c, s):
    return x * c + pltpu.roll(x, 64, 1) * s


def _gelu_tanh(x):
    c = math.sqrt(2.0 / math.pi)
    return x * (0.5 * (1.0 + jnp.tanh(c * (x + 0.044715 * (x * x * x)))))


def _proj_kernel(x_ref, g_ref, w_ref, ws_ref, c64_ref, s64_ref, c128_ref, s128_ref,
                 proj_ref, hn_ref, ikd_ref, iw_ref, hn_scr):
    j = pl.program_id(1)

    @pl.when(j == 0)
    def _():
        hb = _rmsnorm_rows(x_ref[...], g_ref[...]).astype(BF16)
        hn_scr[...] = hb
        hn_ref[...] = hb
        small = _dot(hb, ws_ref[...])
        ikd_ref[...] = _rope64(small[:, :LANES], c64_ref[...], s64_ref[...]).astype(BF16)
        iw_ref[...] = small[:, LANES:]

    acc = _dot(hn_scr[...], w_ref[...])
    is64 = (j == T_DQ) | (j == T_DK) | (j == T_IQ)
    is128 = (j == T_AQ) | (j == T_AK)

    @pl.when(jnp.logical_not(is64 | is128))
    def _():
        proj_ref[...] = acc.astype(BF16)

    @pl.when(is64)
    def _():
        scale = jnp.where(j == T_DQ, DIFF_SUB_DIM ** -0.5, 1.0).astype(F32)
        c, s = c64_ref[...], s64_ref[...]
        for cb in range(BRANCH_WIDTH // LANES):
            sl = slice(cb * LANES, (cb + 1) * LANES)
            proj_ref[:, sl] = (_rope64(acc[:, sl], c, s) * scale).astype(BF16)

    @pl.when(is128)
    def _():
        c, s = c128_ref[...], s128_ref[...]
        for cb in range(BRANCH_WIDTH // LANES):
            sl = slice(cb * LANES, (cb + 1) * LANES)
            proj_ref[:, sl] = _rope128(acc[:, sl], c, s).astype(BF16)


def _proj_call(h2, g, w_main, w_small, rope, seq):
    T, D = h2.shape
    tm = _pick(seq, 1024)
    nrb = seq // tm
    rope_spec = pl.BlockSpec((tm, LANES), lambda i, j: (i % nrb, 0))
    row_spec = lambda w: pl.BlockSpec((tm, w), lambda i, j: (i, 0))
    return pl.pallas_call(
        _proj_kernel,
        grid=(T // tm, N_MAIN_TILES),
        in_specs=[
            row_spec(D),
            pl.BlockSpec((1, D), lambda i, j: (0, 0)),
            pl.BlockSpec((D, BRANCH_WIDTH), lambda i, j: (0, j)),
            pl.BlockSpec((D, 2 * LANES), lambda i, j: (0, 0)),
            rope_spec, rope_spec, rope_spec, rope_spec,
        ],
        out_specs=[
            pl.BlockSpec((tm, BRANCH_WIDTH), lambda i, j: (i, j)),
            row_spec(D), row_spec(LANES), row_spec(LANES),
        ],
        out_shape=[
            jax.ShapeDtypeStruct((T, MAIN_WIDTH), BF16),
            jax.ShapeDtypeStruct((T, D), BF16),
            jax.ShapeDtypeStruct((T, LANES), BF16),
            jax.ShapeDtypeStruct((T, LANES), F32),
        ],
        scratch_shapes=[pltpu.VMEM((tm, D), BF16)],
        compiler_params=_cparams(("parallel", "arbitrary")),
        name="proj",
    )(h2, g, w_main, w_small, *rope)


def _softmax_step_t(parts, m_scr, l_scr, acc_scr, scale=1.0):
    c = scale * LOG2E
    m_prev = m_scr[...]
    m_new = m_prev
    for s_t, _ in parts:
        m_new = jnp.maximum(m_new, jnp.max(s_t, axis=0, keepdims=True))
    alpha = jnp.exp2((m_prev - m_new) * c)
    l_new = alpha * l_scr[...]
    acc = alpha * acc_scr[...]
    for s_t, v in parts:
        p_t = jnp.exp2((s_t - m_new) * c)
        l_new = l_new + jnp.sum(p_t, axis=0, keepdims=True)
        acc = acc + _dot_tn(v, p_t.astype(BF16))
    l_scr[...] = l_new
    acc_scr[...] = acc
    m_scr[...] = m_new


def _init_softmax_state(m_scr, l_scr, acc_scr):
    m_scr[...] = jnp.full(m_scr.shape, NEG_BIG, F32)
    l_scr[...] = jnp.zeros(l_scr.shape, F32)
    acc_scr[...] = jnp.zeros(acc_scr.shape, F32)


def _pipelined_tiles(n, qk, sm):
    qk(0, 0)

    def pair(p, carry):
        qk(2 * p + 1, 1)
        sm(2 * p, 0, False)
        qk(2 * p + 2, 0)
        sm(2 * p + 1, 1, False)
        return carry

    n_pairs = (n - 1) // 2
    lax.fori_loop(0, n_pairs, pair, 0)
    rest = n - 2 * n_pairs

    @pl.when(rest == 1)
    def _():
        sm(n - 1, 0, True)

    @pl.when(rest == 2)
    def _():
        qk(n - 1, 1)
        sm(n - 2, 0, False)
        sm(n - 1, 1, True)


def _diff_kernel(q_ref, k_ref, v_ref, dl_ref, subln_ref, o_ref,
                 qq_scr, s_scr, m_scr, l_scr, acc_scr, *, tq, lam_init):
    qi = pl.program_id(2)
    q = q_ref[...]
    lane = lax.broadcasted_iota(I32, q.shape, 1)
    zero = jnp.zeros_like(q)
    qq_scr[0:tq, :] = jnp.where(lane < DIFF_SUB_DIM, q, zero)
    qq_scr[tq:2 * tq, :] = jnp.where(lane >= DIFF_SUB_DIM, q, zero)
    _init_softmax_state(m_scr, l_scr, acc_scr)

    def qk(kj, slot):
        off = pl.multiple_of(kj * tq, tq)
        s_scr[slot] = _dot_nt(k_ref[pl.ds(off, tq), :], qq_scr[...])

    def sm(kj, slot, causal):
        off = pl.multiple_of(kj * tq, tq)
        s_t = s_scr[slot]
        if causal:
            kpos = lax.broadcasted_iota(I32, s_t.shape, 0)
            qpos = lax.broadcasted_iota(I32, s_t.shape, 1)
            qpos = jnp.where(qpos >= tq, qpos - tq, qpos)
            s_t = jnp.where(kpos <= qpos, s_t, NEG_BIG)
        _softmax_step_t([(s_t, v_ref[pl.ds(off, tq), :])], m_scr, l_scr, acc_scr)

    _pipelined_tiles(qi + 1, qk, sm)

    dl = dl_ref[...]
    lam = (jnp.exp(jnp.sum(dl[0:1] * dl[1:2], axis=-1, keepdims=True))
           - jnp.exp(jnp.sum(dl[2:3] * dl[3:4], axis=-1, keepdims=True)) + lam_init)
    o_all = acc_scr[...] / l_scr[...]
    o_t = o_all[:, 0:tq] - lam * o_all[:, tq:2 * tq]
    ms = jnp.mean(o_t * o_t, axis=0, keepdims=True)
    o_t = (o_t * lax.rsqrt(ms + EPS)) * subln_ref[...] * (1.0 - lam_init)
    o_ref[...] = o_t.T.astype(BF16)


def _diff_call(proj, diff_lambda, subln_col, lam_init, batch, seq):
    T = proj.shape[0]
    tq = _pick(seq, 512)
    nq = seq // tq
    hpt = BRANCH_WIDTH // LANES
    kern = functools.partial(_diff_kernel, tq=tq, lam_init=lam_init)
    return pl.pallas_call(
        kern,
        grid=(batch, N_HEADS, nq),
        in_specs=[
            pl.BlockSpec((tq, LANES), lambda b, h, i: (b * nq + i, T_DQ * hpt + h)),
            pl.BlockSpec((seq, LANES), lambda b, h, i: (b, T_DK * hpt + h)),
            pl.BlockSpec((seq, LANES), lambda b, h, i: (b, T_DV * hpt + h)),
            pl.BlockSpec((4, DIFF_SUB_DIM), lambda b, h, i: (0, 0)),
            pl.BlockSpec((HEAD_DIM, 1), lambda b, h, i: (0, 0)),
        ],
        out_specs=pl.BlockSpec((tq, LANES), lambda b, h, i: (b * nq + i, h)),
        out_shape=jax.ShapeDtypeStruct((T, BRANCH_WIDTH), BF16),
        scratch_shapes=[
            pltpu.VMEM((2 * tq, LANES), BF16),
            pltpu.VMEM((2, tq, 2 * tq), F32),
            pltpu.VMEM((1, 2 * tq), F32),
            pltpu.VMEM((1, 2 * tq), F32),
            pltpu.VMEM((HEAD_DIM, 2 * tq), F32),
        ],
        compiler_params=_cparams(("parallel", "parallel", "arbitrary")),
        name="diff_attn",
    )(proj, proj, proj, diff_lambda, subln_col)


def _dsa_select_kernel(iq_ref, ikd_ref, iwt_ref, mask_ref, key_scr, *, tq, tc, nc, topk, idx_bits):
    qi = pl.program_id(1)
    n_ch = (qi * tq + tq + tc - 1) // tc
    iq = iq_ref[...]
    iwt = iwt_ref[...]
    lane = lax.broadcasted_iota(I32, (tq, LANES), 1)
    zero = jnp.zeros((tq, LANES), BF16)
    qm = []
    for h in range(IDX_HEADS):
        blk = iq[:, (h // 2) * LANES:(h // 2 + 1) * LANES]
        keep = (lane < IDX_DIM) if h % 2 == 0 else (lane >= IDX_DIM)
        qm.append(jnp.where(keep, blk, zero))
    idx_scale = IDX_DIM ** -0.5 * IDX_HEADS ** -0.5
    qpos = qi * tq + lax.broadcasted_iota(I32, (tc, tq), 1)
    krow = lax.broadcasted_iota(I32, (tc, tq), 0)
    krow8 = lax.broadcasted_iota(I32, (SLAB, tq), 0)

    def score_chunk(c, carry):
        ik = ikd_ref[pl.ds(pl.multiple_of(c * tc, tc), tc), :]
        sc = jnp.zeros((tc, tq), F32)
        for h in range(IDX_HEADS):
            sc = sc + iwt[h:h + 1, :] * jnp.maximum(_dot_nt(ik, qm[h]), 0.0)
        sc = sc * idx_scale
        sc = jnp.where(c * tc + krow <= qpos, sc, -jnp.inf)
        bits = pltpu.bitcast(sc, I32)
        key_scr[c] = jnp.where(bits < 0, INT_MIN - bits, bits)
        return carry

    lax.fori_loop(0, n_ch, score_chunk, 0)

    def count(pred):
        def body(c, accs):
            accs = list(accs)
            for r in range(tc // SLAB):
                ks = key_scr[c, r * SLAB:(r + 1) * SLAB, :]
                accs[r % N_COUNTERS] = accs[r % N_COUNTERS] + jnp.where(pred(ks, c * tc + r * SLAB), 1, 0)
            return tuple(accs)
        zero = jnp.zeros((SLAB, tq), I32)
        accs = lax.fori_loop(0, n_ch, body, (zero,) * N_COUNTERS)
        return jnp.sum(functools.reduce(lambda a, b: a + b, accs), axis=0, keepdims=True)

    def rows8(x):
        return jnp.broadcast_to(x, (SLAB, tq))

    def count_ge(cand):
        c8 = rows8(cand)
        return count(lambda ks, base: ks >= c8)

    cnt0 = count_ge(jnp.zeros((1, tq), I32))
    nonneg = cnt0 >= topk
    thr0 = jnp.where(nonneg, 0, INT_MIN).astype(I32)
    cnt_thr0 = jnp.where(nonneg, cnt0, n_ch * tc)

    def all_exact(cnt):
        return (jnp.max(jnp.abs(cnt - topk)) == 0).astype(I32)

    def bit_cond(carry):
        it, _, _, done = carry
        return (it < 31) & (done == 0)

    def bit_body(carry):
        it, thr, cnt_thr, _ = carry
        cand = thr + jnp.left_shift(jnp.int32(1), 30 - it)
        cnt = count_ge(cand)
        take = cnt >= topk
        thr = jnp.where(take, cand, thr)
        cnt_thr = jnp.where(take, cnt, cnt_thr)
        return it + 1, thr, cnt_thr, all_exact(cnt_thr)

    _, thr, cnt_ge, _ = lax.while_loop(bit_cond, bit_body, (jnp.int32(0), thr0, cnt_thr0, all_exact(cnt_thr0)))
    thr8 = rows8(thr)
    excess = jnp.max(cnt_ge - topk) > 0

    def tie_search(_):
        need = topk - count(lambda ks, base: ks > thr8)

        def tbit(it, p):
            cand = p + jnp.left_shift(jnp.int32(1), idx_bits - 1 - it)
            cand8 = rows8(cand)
            cnt = count(lambda ks, base: (ks == thr8) & (base + krow8 < cand8))
            return jnp.where(cnt < need, cand, p)
        return lax.fori_loop(0, idx_bits, tbit, jnp.zeros((1, tq), I32))

    last = lax.cond(excess, tie_search, lambda _: jnp.full((1, tq), 2 ** idx_bits, I32), 0)

    def emit(c, carry):
        kc = key_scr[c]
        kpos = c * tc + krow
        sel = (kc > thr) | ((kc == thr) & (kpos <= last))
        sel = sel & (kpos <= qpos)
        mask_ref[0, 0, c] = jnp.where(sel, 1, 0).astype(jnp.int8)
        return carry

    lax.fori_loop(0, n_ch, emit, 0)

    def clear(c, carry):
        mask_ref[0, 0, c] = jnp.zeros((tc, tq), jnp.int8)
        return carry

    lax.fori_loop(n_ch, nc, clear, 0)


def _dsa_tiles(seq):
    tq = _pick(seq, 256)
    tc = _pick(seq, 512)
    return tq, tc


def _dsa_select_call(proj, ikd, iw_t, batch, seq):
    tq, tc = _dsa_tiles(seq)
    nq, nc = seq // tq, seq // tc
    topk = min(TOPK_MAX, seq // 4)
    assert tc >= topk
    idx_bits = max(1, (seq - 1).bit_length())
    kern = functools.partial(_dsa_select_kernel, tq=tq, tc=tc, nc=nc, topk=topk, idx_bits=idx_bits)
    return pl.pallas_call(
        kern,
        grid=(batch, nq),
        in_specs=[
            pl.BlockSpec((tq, BRANCH_WIDTH), lambda b, i: (b * nq + i, T_IQ)),
            pl.BlockSpec((seq, LANES), lambda b, i: (b, 0)),
            pl.BlockSpec((IDX_HEADS, tq), lambda b, i: (0, b * nq + i)),
        ],
        out_specs=pl.BlockSpec((1, 1, nc, tc, tq), lambda b, i: (b, i, 0, 0, 0)),
        out_shape=jax.ShapeDtypeStruct((batch, nq, nc, tc, tq), jnp.int8),
        scratch_shapes=[pltpu.VMEM((nc, tc, tq), I32)],
        compiler_params=_cparams(("parallel", "arbitrary")),
        name="dsa_select",
    )(proj, ikd, iw_t)


def _dsa_attn_kernel(q_ref, k_ref, v_ref, mask_ref, o_ref, s_scr, m_scr, l_scr, acc_scr, *, tq, tc, cpi):
    qi = pl.program_id(2)
    n_ch = (qi * tq + tq + tc - 1) // tc
    q = q_ref[...]
    scale = HEAD_DIM ** -0.5
    _init_softmax_state(m_scr, l_scr, acc_scr)

    def qk(it, slot):
        for u in range(cpi):
            c = it * cpi + u
            off = pl.multiple_of(c * tc, tc)
            s_t = _dot_nt(k_ref[pl.ds(off, tc), :], q)
            sel = mask_ref[0, 0, c].astype(I32) != 0
            s_scr[slot, u] = jnp.where(sel, s_t, NEG_BIG)

    def sm(it, slot, last):
        parts = []
        for u in range(cpi):
            off = pl.multiple_of((it * cpi + u) * tc, tc)
            parts.append((s_scr[slot, u], v_ref[pl.ds(off, tc), :]))
        _softmax_step_t(parts, m_scr, l_scr, acc_scr, scale)

    _pipelined_tiles((n_ch + cpi - 1) // cpi, qk, sm)
    o_ref[...] = (acc_scr[...] / l_scr[...]).T.astype(BF16)


def _dsa_attn_call(proj, mask, batch, seq):
    T = proj.shape[0]
    tq, tc = _dsa_tiles(seq)
    nq, nc = seq // tq, seq // tc
    hpt = BRANCH_WIDTH // LANES
    cpi = 2 if nc % 2 == 0 else 1
    kern = functools.partial(_dsa_attn_kernel, tq=tq, tc=tc, cpi=cpi)
    return pl.pallas_call(
        kern,
        grid=(batch, N_HEADS, nq),
        in_specs=[
            pl.BlockSpec((tq, LANES), lambda b, h, i: (b * nq + i, T_AQ * hpt + h)),
            pl.BlockSpec((seq, LANES), lambda b, h, i: (b, T_AK * hpt + h)),
            pl.BlockSpec((seq, LANES), lambda b, h, i: (b, T_AV * hpt + h)),
            pl.BlockSpec((1, 1, nc, tc, tq), lambda b, h, i: (b, i, 0, 0, 0)),
        ],
        out_specs=pl.BlockSpec((tq, LANES), lambda b, h, i: (b * nq + i, h)),
        out_shape=jax.ShapeDtypeStruct((T, BRANCH_WIDTH), BF16),
        scratch_shapes=[
            pltpu.VMEM((2, cpi, tc, tq), F32),
            pltpu.VMEM((1, tq), F32),
            pltpu.VMEM((1, tq), F32),
            pltpu.VMEM((HEAD_DIM, tq), F32),
        ],
        compiler_params=_cparams(("parallel", "parallel", "arbitrary")),
        name="dsa_attn",
    )(proj, proj, proj, mask)


def _merge_kernel(cb_ref, cc_ref, ch_ref, ccp_ref, chp_ref, su_ref, sv_ref, yc_ref, yd_ref,
                  hn_ref, res_ref, convw_ref, sgun_ref, sguw_ref, sgub_ref,
                  wg0_ref, wg1_ref, wg2_ref, wg3_ref, wb_ref, wo_ref, fng_ref, router_ref,
                  h_ref, hn2_ref, ridx_ref, rw_ref, ycat_scr, acc_scr, *, tm, rows_per_seq, n_j, n_experts):
    i = pl.program_id(0)
    j = pl.program_id(1)

    @pl.when(j == 0)
    def _():
        acc_scr[...] = jnp.zeros(acc_scr.shape, F32)
        z = cc_ref[...].astype(F32) * ch_ref[...].astype(F32)
        first = ((i * tm) % rows_per_seq) == 0
        zp = ccp_ref[...].astype(F32) * chp_ref[...].astype(F32)
        zp = zp * jnp.where(first, 0.0, 1.0).astype(F32)
        row = lax.broadcasted_iota(I32, z.shape, 0)
        z1 = jnp.where(row == 0, zp[7:8, :], pltpu.roll(z, 1, 0))
        z2 = jnp.where(row == 0, zp[6:7, :], jnp.where(row == 1, zp[7:8, :], pltpu.roll(z, 2, 0)))
        cw = convw_ref[...]
        conv = cw[0:1, :] * z2 + cw[1:2, :] * z1 + cw[2:3, :] * z
        ycat_scr[:, 0:BRANCH_WIDTH] = (cb_ref[...].astype(F32) * conv).astype(BF16)
        u = _gelu_tanh(su_ref[...].astype(F32))
        v = _rmsnorm_rows(_gelu_tanh(sv_ref[...].astype(F32)), sgun_ref[...]).astype(BF16)
        tri_r = lax.broadcasted_iota(I32, (CHUNK, CHUNK), 0)
        tri_c = lax.broadcasted_iota(I32, (CHUNK, CHUNK), 1)
        bias = sgub_ref[...]
        for g in range(BRANCH_WIDTH // LANES):
            wg = jnp.where(tri_c <= tri_r, sguw_ref[g], 0.0).astype(BF16)
            cols = slice(g * LANES, (g + 1) * LANES)
            for ck in range(tm // CHUNK):
                rows = slice(ck * CHUNK, (ck + 1) * CHUNK)
                y = _dot(wg, v[rows, cols]) + bias[:, g:g + 1]
                ycat_scr[rows, BRANCH_WIDTH + g * LANES:BRANCH_WIDTH + (g + 1) * LANES] = (
                    u[rows, cols] * y).astype(BF16)
        ycat_scr[:, 2 * BRANCH_WIDTH:3 * BRANCH_WIDTH] = yc_ref[...]
        ycat_scr[:, 3 * BRANCH_WIDTH:4 * BRANCH_WIDTH] = yd_ref[...]

    hn = hn_ref[...]
    merged = None
    for n, wg_ref in enumerate((wg0_ref, wg1_ref, wg2_ref, wg3_ref)):
        gate = jax.nn.sigmoid(_dot(hn, wg_ref[...]))
        br = _dot(ycat_scr[:, n * BRANCH_WIDTH:(n + 1) * BRANCH_WIDTH], wb_ref[n])
        merged = gate * br if merged is None else merged + gate * br
    acc_scr[...] += _dot(merged.astype(BF16), wo_ref[...])

    @pl.when(j == n_j - 1)
    def _():
        h_new = res_ref[...] + acc_scr[...]
        h_ref[...] = h_new
        hn2 = _rmsnorm_rows(h_new, fng_ref[...])
        hb = hn2.astype(BF16)
        hn2_ref[...] = hb
        logits = _dot(hb, router_ref[...])
        lane = lax.broadcasted_iota(I32, logits.shape, 1)
        logits = jnp.where(lane < n_experts, logits, -jnp.inf)
        v1 = jnp.max(logits, axis=-1, keepdims=True)
        i1 = jnp.min(jnp.where(logits == v1, lane, LANES), axis=-1, keepdims=True)
        rest = jnp.where(lane == i1, -jnp.inf, logits)
        v2 = jnp.max(rest, axis=-1, keepdims=True)
        i2 = jnp.min(jnp.where(rest == v2, lane, LANES), axis=-1, keepdims=True)
        e2 = jnp.exp(v2 - v1)
        w1 = 1.0 / (1.0 + e2)
        w2 = e2 / (1.0 + e2)
        ridx_ref[...] = jnp.where(lane == 0, i1, jnp.where(lane == 1, i2, 0))
        rw_ref[...] = jnp.where(lane == 0, w1, jnp.where(lane == 1, w2, 0.0))


def _merge_call(proj, y_c, y_d, hn, h2, conv_w, sgu_norm, sgu_w, sgu_b_t, w_gate, w_branch, w_out,
                ffn_g, router_pad, seq, n_experts):
    T, D = h2.shape
    tm = _pick(seq, 512)
    tn = _pick(D, 256)
    n_j = D // tn
    rb8 = tm // 8
    kern = functools.partial(_merge_kernel, tm=tm, rows_per_seq=seq, n_j=n_j, n_experts=n_experts)
    tile = lambda t: pl.BlockSpec((tm, BRANCH_WIDTH), lambda i, j, t=t: (i, t))
    prev = lambda t: pl.BlockSpec((8, BRANCH_WIDTH), lambda i, j, t=t: (jnp.maximum(i * rb8 - 1, 0), t))
    full = lambda shp: pl.BlockSpec(shp, lambda i, j: (0,) * len(shp))
    gate = lambda n: pl.BlockSpec((D, tn), lambda i, j, n=n: (0, n * n_j + j))
    rows = lambda w: pl.BlockSpec((tm, w), lambda i, j: (i, 0))
    return pl.pallas_call(
        kern,
        grid=(T // tm, n_j),
        in_specs=[
            tile(T_CB), tile(T_CC), tile(T_CH), prev(T_CC), prev(T_CH), tile(T_SU), tile(T_SV),
            rows(BRANCH_WIDTH), rows(BRANCH_WIDTH), rows(D), rows(D),
            full((CONV_K, BRANCH_WIDTH)), full((1, BRANCH_WIDTH)),
            full((BRANCH_WIDTH // LANES, CHUNK, CHUNK)), full((CHUNK, BRANCH_WIDTH // LANES)),
            gate(0), gate(1), gate(2), gate(3),
            pl.BlockSpec((N_BRANCHES, BRANCH_WIDTH, tn), lambda i, j: (0, 0, j)),
            pl.BlockSpec((tn, D), lambda i, j: (j, 0)),
            full((1, D)), full((D, LANES)),
        ],
        out_specs=[rows(D), rows(D), rows(LANES), rows(LANES)],
        out_shape=[
            jax.ShapeDtypeStruct((T, D), F32),
            jax.ShapeDtypeStruct((T, D), BF16),
            jax.ShapeDtypeStruct((T, LANES), I32),
            jax.ShapeDtypeStruct((T, LANES), F32),
        ],
        scratch_shapes=[pltpu.VMEM((tm, N_BRANCHES * BRANCH_WIDTH), BF16), pltpu.VMEM((tm, D), F32)],
        compiler_params=_cparams(("parallel", "arbitrary"), MERGE_VMEM_LIMIT),
        name="merge",
    )(proj, proj, proj, proj, proj, proj, proj, y_c, y_d, hn, h2, conv_w, sgu_norm, sgu_w, sgu_b_t,
      w_gate, w_gate, w_gate, w_gate, w_branch, w_out, ffn_g, router_pad)


def _ffn_kernel(x_ref, comb_ref, w1_ref, w3_ref, w2_ref, res_ref, g_ref, o_ref, acc_e, acc_t,
                *, n_e, n_f, final_norm):
    e = pl.program_id(1)
    f = pl.program_id(2)

    @pl.when((e == 0) & (f == 0))
    def _():
        acc_t[...] = jnp.zeros(acc_t.shape, F32)

    @pl.when(f == 0)
    def _():
        acc_e[...] = jnp.zeros(acc_e.shape, F32)

    x = x_ref[...]
    h1 = _dot(x, w1_ref[...])
    h3 = _dot(x, w3_ref[...])
    a = (h1 * jax.nn.sigmoid(h1)) * h3
    acc_e[...] += _dot(a.astype(BF16), w2_ref[...])

    @pl.when(f == n_f - 1)
    def _():
        comb = comb_ref[...]
        lane = lax.broadcasted_iota(I32, comb.shape, 1)
        c = jnp.sum(jnp.where(lane == e, comb, 0.0), axis=-1, keepdims=True)
        acc_t[...] += c * acc_e[...]

    @pl.when((e == n_e - 1) & (f == n_f - 1))
    def _():
        out = res_ref[...] + acc_t[...]
        if final_norm:
            out = _rmsnorm_rows(out, g_ref[...])
        o_ref[...] = out


def _ffn_call(xb, comb, w1, w3, w2, res, g, final_norm):
    T, D = res.shape
    n_e, _, d_ff = w1.shape
    tm = _pick(T, 512)
    tf = _pick(d_ff, 512)
    n_f = d_ff // tf
    kern = functools.partial(_ffn_kernel, n_e=n_e, n_f=n_f, final_norm=final_norm)
    rows = lambda w: pl.BlockSpec((tm, w), lambda i, e, f: (i, 0))
    return pl.pallas_call(
        kern,
        grid=(T // tm, n_e, n_f),
        in_specs=[
            rows(D), rows(LANES),
            pl.BlockSpec((None, D, tf), lambda i, e, f: (e, 0, f)),
            pl.BlockSpec((None, D, tf), lambda i, e, f: (e, 0, f)),
            pl.BlockSpec((None, tf, D), lambda i, e, f: (e, f, 0)),
            rows(D),
            pl.BlockSpec((1, D), lambda i, e, f: (0, 0)),
        ],
        out_specs=rows(D),
        out_shape=jax.ShapeDtypeStruct((T, D), F32),
        scratch_shapes=[pltpu.VMEM((tm, D), F32), pltpu.VMEM((tm, D), F32)],
        compiler_params=_cparams(("parallel", "arbitrary", "arbitrary")),
        name="ffn",
    )(xb, comb, w1, w3, w2, res, g)


def _row_copy(src_hbm, row, dst_vmem, slot, sem):
    return pltpu.make_async_copy(src_hbm.at[pl.ds(row, 1), :], dst_vmem.at[pl.ds(slot, 1), :], sem)


def _moe_gather_kernel(tok_ref, h_hbm, g_ref, o_ref, buf, sem, *, tg):
    def start(r, c):
        _row_copy(h_hbm, tok_ref[0, 0, r], buf, r, sem).start()
        return c

    lax.fori_loop(0, tg, start, 0)

    def wait(r, c):
        _row_copy(h_hbm, tok_ref[0, 0, r], buf, r, sem).wait()
        return c

    lax.fori_loop(0, tg, wait, 0)
    o_ref[...] = _rmsnorm_rows(buf[...], g_ref[...]).astype(BF16)


def _moe_gather_call(h2, g, row_token, tg):
    T, D = h2.shape
    R = row_token.shape[0]
    kern = functools.partial(_moe_gather_kernel, tg=tg)
    return pl.pallas_call(
        kern,
        grid=(R // tg,),
        in_specs=[
            pl.BlockSpec((1, 1, tg), lambda i: (i, 0, 0), memory_space=pltpu.SMEM),
            pl.BlockSpec(memory_space=pl.ANY),
            pl.BlockSpec((1, D), lambda i: (0, 0)),
        ],
        out_specs=pl.BlockSpec((tg, D), lambda i: (i, 0)),
        out_shape=jax.ShapeDtypeStruct((R, D), BF16),
        scratch_shapes=[pltpu.VMEM((tg, D), F32), pltpu.SemaphoreType.DMA(())],
        compiler_params=_cparams(("arbitrary",)),
        name="moe_gather",
    )(row_token.reshape(R // tg, 1, tg), h2, g)


def _ffn_routed_kernel(te_ref, nu_ref, x_ref, rw_ref, w1_ref, w3_ref, w2_ref, o_ref, acc, *, n_f):
    i = pl.program_id(0)
    f = pl.program_id(1)

    @pl.when(f == 0)
    def _():
        acc[...] = jnp.zeros(acc.shape, F32)

    @pl.when(i < nu_ref[0])
    def _():
        x = x_ref[...]
        h1 = _dot(x, w1_ref[...])
        h3 = _dot(x, w3_ref[...])
        a = (h1 * jax.nn.sigmoid(h1)) * h3
        acc[...] += _dot(a.astype(BF16), w2_ref[...])

    @pl.when(f == n_f - 1)
    def _():
        o_ref[...] = rw_ref[...] * acc[...]


def _ffn_routed_call(xs, row_w, tile_e, n_used, w1, w3, w2, tm):
    R, D = xs.shape
    d_ff = w1.shape[-1]
    tf = _pick(d_ff, 512)
    n_f = d_ff // tf
    kern = functools.partial(_ffn_routed_kernel, n_f=n_f)

    def wmap(i, f, te, nu):
        return te[i], jnp.where(i < nu[0], f, n_f - 1)

    return pl.pallas_call(
        kern,
        grid_spec=pltpu.PrefetchScalarGridSpec(
            num_scalar_prefetch=2,
            grid=(R // tm, n_f),
            in_specs=[
                pl.BlockSpec((tm, D), lambda i, f, te, nu: (i, 0)),
                pl.BlockSpec((tm, 1), lambda i, f, te, nu: (i, 0)),
                pl.BlockSpec((None, D, tf), lambda i, f, te, nu: (wmap(i, f, te, nu)[0], 0, wmap(i, f, te, nu)[1])),
                pl.BlockSpec((None, D, tf), lambda i, f, te, nu: (wmap(i, f, te, nu)[0], 0, wmap(i, f, te, nu)[1])),
                pl.BlockSpec((None, tf, D), lambda i, f, te, nu: (wmap(i, f, te, nu)[0], wmap(i, f, te, nu)[1], 0)),
            ],
            out_specs=pl.BlockSpec((tm, D), lambda i, f, te, nu: (i, 0)),
            scratch_shapes=[pltpu.VMEM((tm, D), F32)],
        ),
        out_shape=jax.ShapeDtypeStruct((R, D), F32),
        compiler_params=_cparams(("arbitrary", "arbitrary")),
        name="ffn_routed",
    )(tile_e, n_used, xs, row_w, w1, w3, w2)


def _moe_combine_kernel(pos_ref, h_ref, y_hbm, g_ref, o_ref, buf0, buf1, sem, *, tc, final_norm):
    def start(r, c):
        _row_copy(y_hbm, pos_ref[0, 0, 2 * r], buf0, r, sem).start()
        _row_copy(y_hbm, pos_ref[0, 0, 2 * r + 1], buf1, r, sem).start()
        return c

    lax.fori_loop(0, tc, start, 0)

    def wait(r, c):
        _row_copy(y_hbm, pos_ref[0, 0, 2 * r], buf0, r, sem).wait()
        _row_copy(y_hbm, pos_ref[0, 0, 2 * r + 1], buf1, r, sem).wait()
        return c

    lax.fori_loop(0, tc, wait, 0)
    out = h_ref[...] + (buf0[...] + buf1[...])
    if final_norm:
        out = _rmsnorm_rows(out, g_ref[...])
    o_ref[...] = out


def _moe_combine_call(h2, ys, pos, g, final_norm):
    T, D = h2.shape
    tc = _pick(T, 256)
    kern = functools.partial(_moe_combine_kernel, tc=tc, final_norm=final_norm)
    return pl.pallas_call(
        kern,
        grid=(T // tc,),
        in_specs=[
            pl.BlockSpec((1, 1, 2 * tc), lambda i: (i, 0, 0), memory_space=pltpu.SMEM),
            pl.BlockSpec((tc, D), lambda i: (i, 0)),
            pl.BlockSpec(memory_space=pl.ANY),
            pl.BlockSpec((1, D), lambda i: (0, 0)),
        ],
        out_specs=pl.BlockSpec((tc, D), lambda i: (i, 0)),
        out_shape=jax.ShapeDtypeStruct((T, D), F32),
        scratch_shapes=[pltpu.VMEM((tc, D), F32), pltpu.VMEM((tc, D), F32), pltpu.SemaphoreType.DMA(())],
        compiler_params=_cparams(("arbitrary",)),
        name="moe_combine",
    )(pos.reshape(T // tc, 1, 2 * tc), h2, ys, g)


def _route(ridx, rw, n_experts, tm):
    T = ridx.shape[0]
    flat_e = ridx[:, :2].reshape(-1)
    onehot = (flat_e[:, None] == jnp.arange(n_experts, dtype=I32)[None, :]).astype(I32)
    cum = jnp.cumsum(onehot, axis=0)
    rank = jnp.take_along_axis(cum, flat_e[:, None], axis=1)[:, 0] - 1
    counts = cum[-1]
    padded = ((counts + tm - 1) // tm) * tm
    gend = jnp.cumsum(padded)
    gstart = gend - padded
    pos = gstart[flat_e] + rank
    R = 2 * T + n_experts * tm
    row_token = jnp.zeros((R,), I32).at[pos].set(jnp.arange(2 * T, dtype=I32) // 2)
    row_w = jnp.zeros((R,), F32).at[pos].set(rw[:, :2].reshape(-1))
    tile_e = jnp.searchsorted(gend, jnp.arange(R // tm, dtype=I32) * tm, side="right")
    tile_e = jnp.minimum(tile_e, n_experts - 1).astype(I32)
    n_used = (gend[-1:] // tm).astype(I32)
    return pos.astype(I32), row_token, row_w[:, None], tile_e, n_used


def _rope_tables(seq):
    pos = jnp.arange(seq, dtype=F32)

    def tab(dim):
        inv = ROPE_THETA ** (-jnp.arange(0, dim, 2, dtype=F32) / dim)
        ang = pos[:, None] * inv[None, :]
        cos = jnp.concatenate([jnp.cos(ang), jnp.cos(ang)], axis=-1)
        sin = jnp.concatenate([-jnp.sin(ang), jnp.sin(ang)], axis=-1)
        return cos, sin

    c64, s64 = tab(IDX_DIM)
    c128, s128 = tab(HEAD_DIM)
    return (jnp.tile(c64, (1, 2)), jnp.tile(s64, (1, 2)), c128, s128)


def kernel(x, attn_norm, w_in, conv_w, sgu_norm, sgu_w, sgu_b, diff_lambda, diff_subln, w_branch, w_out,
           ffn_norm, dense_w1, dense_w3, dense_w2, router, moe_w1, moe_w3, moe_w2, final_norm):
    batch, seq, D = x.shape
    depth = w_in.shape[0]
    n_experts = router.shape[-1]
    T = batch * seq
    rope = _rope_tables(seq)
    h = x.reshape(T, D)
    ik0 = MAIN_WIDTH
    iw0 = ik0 + IDX_DIM
    g0 = iw0 + IDX_HEADS
    ones_comb = jnp.zeros((T, LANES), F32).at[:, 0].set(1.0)

    for layer in range(depth):
        wl = w_in[layer]
        w_main = wl[:, :MAIN_WIDTH].astype(BF16)
        w_small = jnp.concatenate(
            [wl[:, ik0:iw0], wl[:, ik0:iw0], wl[:, iw0:g0], jnp.zeros((D, LANES - IDX_HEADS), F32)],
            axis=1).astype(BF16)
        w_gate = wl[:, g0:].astype(BF16)
        lam_init = 0.8 - 0.6 * math.exp(-0.3 * layer)

        proj, hn, ikd, iw = _proj_call(h, attn_norm[layer][None, :], w_main, w_small, rope, seq)
        y_c = _diff_call(proj, diff_lambda[layer], diff_subln[layer][:, None], lam_init, batch, seq)
        mask = _dsa_select_call(proj, ikd, iw[:, :IDX_HEADS].T, batch, seq)
        y_d = _dsa_attn_call(proj, mask, batch, seq)

        j = layer // 2
        is_moe = layer % 2 == 1
        router_pad = jnp.zeros((D, LANES), F32)
        if is_moe:
            router_pad = router_pad.at[:, :n_experts].set(router[j])
        h, hn2, ridx, rw = _merge_call(
            proj, y_c, y_d, hn, h, conv_w[layer], sgu_norm[layer][None, :], sgu_w[layer],
            sgu_b[layer].T, w_gate, w_branch[layer].astype(BF16), w_out[layer].astype(BF16),
            ffn_norm[layer][None, :], router_pad.astype(BF16), seq, n_experts)

        last = layer == depth - 1
        if is_moe:
            tm = _pick(T, 1024)
            pos, row_token, row_w, tile_e, n_used = _route(ridx, rw, n_experts, tm)
            xs = _moe_gather_call(h, ffn_norm[layer][None, :], row_token, _pick(tm, 512))
            ys = _ffn_routed_call(xs, row_w, tile_e, n_used, moe_w1[j].astype(BF16),
                                  moe_w3[j].astype(BF16), moe_w2[j].astype(BF16), tm)
            h = _moe_combine_call(h, ys, pos, final_norm[None, :], last)
        else:
            h = _ffn_call(hn2, ones_comb, dense_w1[j][None].astype(BF16), dense_w3[j][None].astype(BF16),
                          dense_w2[j][None].astype(BF16), h, final_norm[None, :], last)

    if depth == 0:
        raise ValueError("depth must be positive")
    return h.reshape(batch, seq, D)
```

```python
import functools
import math

import jax
import jax.numpy as jnp
from jax import lax
from jax.experimental import pallas as pl
from jax.experimental.pallas import tpu as pltpu

F32 = jnp.float32
BF16 = jnp.bfloat16
I32 = jnp.int32

LANES = 128
SLAB = 8
N_COUNTERS = 4
PACK = 16
INF_KEY16 = 0x7F80
MIN_NORMAL_KEY16 = 0x0080
MIN_NORMAL_F32 = 1.1754943508222875e-38
HEAD_DIM = 128
BRANCH_WIDTH = 512
N_BRANCHES = 4
N_HEADS = 4
IDX_HEADS = 8
IDX_DIM = 64
DIFF_SUB_DIM = 64
CONV_K = 3
CHUNK = 128
TOPK_MAX = 256
ROPE_THETA = 10000.0
EPS = 1e-6
N_MAIN_TILES = 12
MAIN_WIDTH = N_MAIN_TILES * BRANCH_WIDTH
(T_CB, T_CC, T_CH, T_SU, T_SV, T_DQ, T_DK, T_DV, T_AQ, T_AK, T_AV, T_IQ) = range(12)
NEG_BIG = -1e30
INT_MIN = -(2 ** 31)
VMEM_LIMIT = 56 * 1024 * 1024
MERGE_VMEM_LIMIT = 61 * 1024 * 1024


def _pick(n, pref):
    t = min(pref, n)
    while n % t:
        t //= 2
    return t


def _cparams(sem, vmem_limit=VMEM_LIMIT):
    return pltpu.CompilerParams(dimension_semantics=sem, vmem_limit_bytes=vmem_limit)


def _dot(a, b):
    return jnp.dot(a, b, preferred_element_type=F32)


def _dot_nt(a, b):
    return lax.dot_general(a, b, (((1,), (1,)), ((), ())), preferred_element_type=F32)


def _dot_tn(a, b):
    return lax.dot_general(a, b, (((0,), (0,)), ((), ())), preferred_element_type=F32)


def _rmsnorm_rows(x, g):
    ms = jnp.mean(x * x, axis=-1, keepdims=True)
    return (x * lax.rsqrt(ms + EPS)) * g


def _rope64(x, c, s):
    lane = lax.broadcasted_iota(I32, x.shape, 1)
    rot = jnp.where((lane & 32) == 0, pltpu.roll(x, 96, 1), pltpu.roll(x, 32, 1))
    return x * c + rot * s


def _rope128(x, c, s):
    return x * c + pltpu.roll(x, 64, 1) * s


def _gelu_tanh(x):
    c = math.sqrt(2.0 / math.pi)
    return x * (0.5 * (1.0 + jnp.tanh(c * (x + 0.044715 * (x * x * x)))))


def _proj_kernel(x_ref, g_ref, w_ref, ws_ref, c64_ref, s64_ref, c128_ref, s128_ref,
                 proj_ref, hn_ref, ikd_ref, iw_ref, hn_scr):
    j = pl.program_id(1)

    @pl.when(j == 0)
    def _():
        hb = _rmsnorm_rows(x_ref[...], g_ref[...]).astype(BF16)
        hn_scr[...] = hb
        hn_ref[...] = hb
        small = _dot(hb, ws_ref[...])
        ikd_ref[...] = _rope64(small[:, :LANES], c64_ref[...], s64_ref[...]).astype(BF16)
        iw_ref[...] = small[:, LANES:]

    acc = _dot(hn_scr[...], w_ref[...])
    is64 = (j == T_DQ) | (j == T_DK) | (j == T_IQ)
    is128 = (j == T_AQ) | (j == T_AK)

    @pl.when(jnp.logical_not(is64 | is128))
    def _():
        proj_ref[...] = acc.astype(BF16)

    @pl.when(is64)
    def _():
        scale = jnp.where(j == T_DQ, DIFF_SUB_DIM ** -0.5, 1.0).astype(F32)
        c, s = c64_ref[...], s64_ref[...]
        for cb in range(BRANCH_WIDTH // LANES):
            sl = slice(cb * LANES, (cb + 1) * LANES)
            proj_ref[:, sl] = (_rope64(acc[:, sl], c, s) * scale).astype(BF16)

    @pl.when(is128)
    def _():
        c, s = c128_ref[...], s128_ref[...]
        for cb in range(BRANCH_WIDTH // LANES):
            sl = slice(cb * LANES, (cb + 1) * LANES)
            proj_ref[:, sl] = _rope128(acc[:, sl], c, s).astype(BF16)


def _proj_call(h2, g, w_main, w_small, rope, seq):
    T, D = h2.shape
    tm = _pick(seq, 1024)
    nrb = seq // tm
    rope_spec = pl.BlockSpec((tm, LANES), lambda i, j: (i % nrb, 0))
    row_spec = lambda w: pl.BlockSpec((tm, w), lambda i, j: (i, 0))
    return pl.pallas_call(
        _proj_kernel,
        grid=(T // tm, N_MAIN_TILES),
        in_specs=[
            row_spec(D),
            pl.BlockSpec((1, D), lambda i, j: (0, 0)),
            pl.BlockSpec((D, BRANCH_WIDTH), lambda i, j: (0, j)),
            pl.BlockSpec((D, 2 * LANES), lambda i, j: (0, 0)),
            rope_spec, rope_spec, rope_spec, rope_spec,
        ],
        out_specs=[
            pl.BlockSpec((tm, BRANCH_WIDTH), lambda i, j: (i, j)),
            row_spec(D), row_spec(LANES), row_spec(LANES),
        ],
        out_shape=[
            jax.ShapeDtypeStruct((T, MAIN_WIDTH), BF16),
            jax.ShapeDtypeStruct((T, D), BF16),
            jax.ShapeDtypeStruct((T, LANES), BF16),
            jax.ShapeDtypeStruct((T, LANES), F32),
        ],
        scratch_shapes=[pltpu.VMEM((tm, D), BF16)],
        compiler_params=_cparams(("parallel", "arbitrary")),
        name="proj",
    )(h2, g, w_main, w_small, *rope)


def _softmax_step_t(parts, m_scr, l_scr, acc_scr):
    m_prev = m_scr[...]
    m_new = m_prev
    for s_t, _ in parts:
        m_new = jnp.maximum(m_new, jnp.max(s_t, axis=0, keepdims=True))
    alpha = jnp.exp(m_prev - m_new)
    l_new = alpha * l_scr[...]
    acc = alpha * acc_scr[...]
    for s_t, v in parts:
        p_t = jnp.exp(s_t - m_new)
        l_new = l_new + jnp.sum(p_t, axis=0, keepdims=True)
        acc = acc + _dot_tn(v, p_t.astype(BF16))
    l_scr[...] = l_new
    acc_scr[...] = acc
    m_scr[...] = m_new


def _init_softmax_state(m_scr, l_scr, acc_scr):
    m_scr[...] = jnp.full(m_scr.shape, NEG_BIG, F32)
    l_scr[...] = jnp.zeros(l_scr.shape, F32)
    acc_scr[...] = jnp.zeros(acc_scr.shape, F32)


def _pipelined_tiles(n, qk, sm):
    qk(0, 0)

    def pair(p, carry):
        qk(2 * p + 1, 1)
        sm(2 * p, 0, False)
        qk(2 * p + 2, 0)
        sm(2 * p + 1, 1, False)
        return carry

    n_pairs = (n - 1) // 2
    lax.fori_loop(0, n_pairs, pair, 0)
    rest = n - 2 * n_pairs

    @pl.when(rest == 1)
    def _():
        sm(n - 1, 0, True)

    @pl.when(rest == 2)
    def _():
        qk(n - 1, 1)
        sm(n - 2, 0, False)
        sm(n - 1, 1, True)


def _diff_kernel(q_ref, k_ref, v_ref, dl_ref, subln_ref, o_ref,
                 qq_scr, s_scr, m_scr, l_scr, acc_scr, *, tq, lam_init):
    qi = pl.program_id(2)
    q = q_ref[...]
    lane = lax.broadcasted_iota(I32, q.shape, 1)
    zero = jnp.zeros_like(q)
    qq_scr[0:tq, :] = jnp.where(lane < DIFF_SUB_DIM, q, zero)
    qq_scr[tq:2 * tq, :] = jnp.where(lane >= DIFF_SUB_DIM, q, zero)
    _init_softmax_state(m_scr, l_scr, acc_scr)

    def qk(kj, slot):
        off = pl.multiple_of(kj * tq, tq)
        s_scr[slot] = _dot_nt(k_ref[pl.ds(off, tq), :], qq_scr[...])

    def sm(kj, slot, causal):
        off = pl.multiple_of(kj * tq, tq)
        s_t = s_scr[slot]
        if causal:
            kpos = lax.broadcasted_iota(I32, s_t.shape, 0)
            qpos = lax.broadcasted_iota(I32, s_t.shape, 1)
            qpos = jnp.where(qpos >= tq, qpos - tq, qpos)
            s_t = jnp.where(kpos <= qpos, s_t, NEG_BIG)
        _softmax_step_t([(s_t, v_ref[pl.ds(off, tq), :])], m_scr, l_scr, acc_scr)

    _pipelined_tiles(qi + 1, qk, sm)

    dl = dl_ref[...]
    lam = (jnp.exp(jnp.sum(dl[0:1] * dl[1:2], axis=-1, keepdims=True))
           - jnp.exp(jnp.sum(dl[2:3] * dl[3:4], axis=-1, keepdims=True)) + lam_init)
    o_all = acc_scr[...] / l_scr[...]
    o_t = o_all[:, 0:tq] - lam * o_all[:, tq:2 * tq]
    ms = jnp.mean(o_t * o_t, axis=0, keepdims=True)
    o_t = (o_t * lax.rsqrt(ms + EPS)) * subln_ref[...] * (1.0 - lam_init)
    o_ref[...] = o_t.T.astype(BF16)


def _diff_call(proj, diff_lambda, subln_col, lam_init, batch, seq):
    T = proj.shape[0]
    tq = _pick(seq, 512)
    nq = seq // tq
    hpt = BRANCH_WIDTH // LANES
    kern = functools.partial(_diff_kernel, tq=tq, lam_init=lam_init)
    return pl.pallas_call(
        kern,
        grid=(batch, N_HEADS, nq),
        in_specs=[
            pl.BlockSpec((tq, LANES), lambda b, h, i: (b * nq + i, T_DQ * hpt + h)),
            pl.BlockSpec((seq, LANES), lambda b, h, i: (b, T_DK * hpt + h)),
            pl.BlockSpec((seq, LANES), lambda b, h, i: (b, T_DV * hpt + h)),
            pl.BlockSpec((4, DIFF_SUB_DIM), lambda b, h, i: (0, 0)),
            pl.BlockSpec((HEAD_DIM, 1), lambda b, h, i: (0, 0)),
        ],
        out_specs=pl.BlockSpec((tq, LANES), lambda b, h, i: (b * nq + i, h)),
        out_shape=jax.ShapeDtypeStruct((T, BRANCH_WIDTH), BF16),
        scratch_shapes=[
            pltpu.VMEM((2 * tq, LANES), BF16),
            pltpu.VMEM((2, tq, 2 * tq), F32),
            pltpu.VMEM((1, 2 * tq), F32),
            pltpu.VMEM((1, 2 * tq), F32),
            pltpu.VMEM((HEAD_DIM, 2 * tq), F32),
        ],
        compiler_params=_cparams(("parallel", "parallel", "arbitrary")),
        name="diff_attn",
    )(proj, proj, proj, diff_lambda, subln_col)


def _dsa_select_kernel(iq_ref, ikd_ref, iwt_ref, mask_ref, key_scr, hi_scr, *, tq, tc, nc, topk, idx_bits):
    qi = pl.program_id(1)
    n_ch = (qi * tq + tq + tc - 1) // tc
    iq = iq_ref[...]
    iwt = iwt_ref[...]
    lane = lax.broadcasted_iota(I32, (tq, LANES), 1)
    zero = jnp.zeros((tq, LANES), BF16)
    qm = []
    for h in range(IDX_HEADS):
        blk = iq[:, (h // 2) * LANES:(h // 2 + 1) * LANES]
        keep = (lane < IDX_DIM) if h % 2 == 0 else (lane >= IDX_DIM)
        qm.append(jnp.where(keep, blk, zero))
    idx_scale = IDX_DIM ** -0.5 * IDX_HEADS ** -0.5
    qpos = qi * tq + lax.broadcasted_iota(I32, (tc, tq), 1)
    krow = lax.broadcasted_iota(I32, (tc, tq), 0)
    krow8 = lax.broadcasted_iota(I32, (SLAB, tq), 0)

    def score_chunk(c, carry):
        ik = ikd_ref[pl.ds(pl.multiple_of(c * tc, tc), tc), :]
        sc = jnp.zeros((tc, tq), F32)
        for h in range(IDX_HEADS):
            sc = sc + iwt[h:h + 1, :] * jnp.maximum(_dot_nt(ik, qm[h]), 0.0)
        sc = sc * idx_scale
        sc = jnp.where(jnp.abs(sc) < MIN_NORMAL_F32, 0.0, sc)
        sc = jnp.where(c * tc + krow <= qpos, sc, -jnp.inf)
        bits = pltpu.bitcast(sc, I32)
        key_scr[c] = jnp.where(bits < 0, INT_MIN - bits, bits)
        hi_scr[c] = sc.astype(BF16)
        return carry

    lax.fori_loop(0, n_ch, score_chunk, 0)

    def count(pred):
        def body(c, accs):
            accs = list(accs)
            for r in range(tc // SLAB):
                ks = key_scr[c, r * SLAB:(r + 1) * SLAB, :]
                accs[r % N_COUNTERS] = accs[r % N_COUNTERS] + jnp.where(pred(ks, c * tc + r * SLAB), 1, 0)
            return tuple(accs)
        zero = jnp.zeros((SLAB, tq), I32)
        accs = lax.fori_loop(0, n_ch, body, (zero,) * N_COUNTERS)
        return jnp.sum(functools.reduce(lambda a, b: a + b, accs), axis=0, keepdims=True)

    def rows8(x):
        return jnp.broadcast_to(x, (SLAB, tq))

    def count_ge(cand):
        c8 = rows8(cand)
        return count(lambda ks, base: ks >= c8)

    def count16(cand_val):
        c16 = jnp.broadcast_to(cand_val, (PACK, tq)).astype(BF16)
        one = jnp.ones((PACK, tq), BF16)
        zero16 = jnp.zeros((PACK, tq), BF16)

        def body(c, accs):
            accs = list(accs)
            for r in range(tc // PACK):
                hs = hi_scr[c, r * PACK:(r + 1) * PACK, :]
                accs[r % N_COUNTERS] = accs[r % N_COUNTERS] + jnp.where(hs >= c16, one, zero16)
            return tuple(accs)
        accs = lax.fori_loop(0, n_ch, body, (zero16,) * N_COUNTERS)
        tot = functools.reduce(lambda a, b: a + b, [a.astype(F32) for a in accs])
        return jnp.sum(tot, axis=0, keepdims=True)

    def key16_value(k16):
        k16 = jnp.clip(k16, -INF_KEY16, INF_KEY16)
        k16 = jnp.where((k16 > 0) & (k16 < MIN_NORMAL_KEY16), MIN_NORMAL_KEY16, k16)
        bits16 = jnp.where(k16 >= 0, k16, -32768 - k16)
        return pltpu.bitcast(jnp.left_shift(bits16, 16), F32)

    nonneg = count16(jnp.zeros((1, tq), F32)) >= topk
    thr16 = jnp.where(nonneg, 0, -32768).astype(I32)

    def bit16_body(it, thr16):
        cand = thr16 + jnp.left_shift(jnp.int32(1), 14 - it)
        return jnp.where(count16(key16_value(cand)) >= topk, cand, thr16)

    thr16 = lax.fori_loop(0, 15, bit16_body, thr16)

    thr = jnp.left_shift(thr16, 16) - 2 ** 15

    def bit_body(it, thr):
        cand = thr + jnp.left_shift(jnp.int32(1), 16 - it)
        return jnp.where(count_ge(cand) >= topk, cand, thr)

    thr = lax.fori_loop(0, 17, bit_body, thr)
    thr8 = rows8(thr)
    cnt_ge = count_ge(thr)
    excess = jnp.max(cnt_ge - topk) > 0

    def tie_search(_):
        need = topk - count(lambda ks, base: ks > thr8)

        def tbit(it, p):
            cand = p + jnp.left_shift(jnp.int32(1), idx_bits - 1 - it)
            cand8 = rows8(cand)
            cnt = count(lambda ks, base: (ks == thr8) & (base + krow8 < cand8))
            return jnp.where(cnt < need, cand, p)
        return lax.fori_loop(0, idx_bits, tbit, jnp.zeros((1, tq), I32))

    last = lax.cond(excess, tie_search, lambda _: jnp.full((1, tq), 2 ** idx_bits, I32), 0)

    def emit(c, carry):
        kc = key_scr[c]
        kpos = c * tc + krow
        sel = (kc > thr) | ((kc == thr) & (kpos <= last))
        sel = sel & (kpos <= qpos)
        mask_ref[0, 0, c] = jnp.where(sel, 1, 0).astype(jnp.int8)
        return carry

    lax.fori_loop(0, n_ch, emit, 0)

    def clear(c, carry):
        mask_ref[0, 0, c] = jnp.zeros((tc, tq), jnp.int8)
        return carry

    lax.fori_loop(n_ch, nc, clear, 0)


def _dsa_tiles(seq):
    tq = _pick(seq, 256)
    tc = _pick(seq, 512)
    return tq, tc


def _dsa_select_call(proj, ikd, iw_t, batch, seq):
    tq, tc = _dsa_tiles(seq)
    nq, nc = seq // tq, seq // tc
    topk = min(TOPK_MAX, seq // 4)
    assert tc >= topk
    assert (tc // PACK // N_COUNTERS) * nc <= 256
    idx_bits = max(1, (seq - 1).bit_length())
    kern = functools.partial(_dsa_select_kernel, tq=tq, tc=tc, nc=nc, topk=topk, idx_bits=idx_bits)
    return pl.pallas_call(
        kern,
        grid=(batch, nq),
        in_specs=[
            pl.BlockSpec((tq, BRANCH_WIDTH), lambda b, i: (b * nq + i, T_IQ)),
            pl.BlockSpec((seq, LANES), lambda b, i: (b, 0)),
            pl.BlockSpec((IDX_HEADS, tq), lambda b, i: (0, b * nq + i)),
        ],
        out_specs=pl.BlockSpec((1, 1, nc, tc, tq), lambda b, i: (b, i, 0, 0, 0)),
        out_shape=jax.ShapeDtypeStruct((batch, nq, nc, tc, tq), jnp.int8),
        scratch_shapes=[pltpu.VMEM((nc, tc, tq), I32), pltpu.VMEM((nc, tc, tq), BF16)],
        compiler_params=_cparams(("parallel", "arbitrary")),
        name="dsa_select",
    )(proj, ikd, iw_t)


def _dsa_attn_kernel(q_ref, k_ref, v_ref, mask_ref, o_ref, s_scr, m_scr, l_scr, acc_scr, *, tq, tc, cpi):
    qi = pl.program_id(2)
    n_ch = (qi * tq + tq + tc - 1) // tc
    q = q_ref[...]
    scale = HEAD_DIM ** -0.5
    _init_softmax_state(m_scr, l_scr, acc_scr)

    def qk(it, slot):
        for u in range(cpi):
            c = it * cpi + u
            off = pl.multiple_of(c * tc, tc)
            s_t = _dot_nt(k_ref[pl.ds(off, tc), :], q) * scale
            sel = mask_ref[0, 0, c].astype(I32) != 0
            s_scr[slot, u] = jnp.where(sel, s_t, NEG_BIG)

    def sm(it, slot, last):
        parts = []
        for u in range(cpi):
            off = pl.multiple_of((it * cpi + u) * tc, tc)
            parts.append((s_scr[slot, u], v_ref[pl.ds(off, tc), :]))
        _softmax_step_t(parts, m_scr, l_scr, acc_scr)

    _pipelined_tiles((n_ch + cpi - 1) // cpi, qk, sm)
    o_ref[...] = (acc_scr[...] / l_scr[...]).T.astype(BF16)


def _dsa_attn_call(proj, mask, batch, seq):
    T = proj.shape[0]
    tq, tc = _dsa_tiles(seq)
    nq, nc = seq // tq, seq // tc
    hpt = BRANCH_WIDTH // LANES
    cpi = 2 if nc % 2 == 0 else 1
    kern = functools.partial(_dsa_attn_kernel, tq=tq, tc=tc, cpi=cpi)
    return pl.pallas_call(
        kern,
        grid=(batch, N_HEADS, nq),
        in_specs=[
            pl.BlockSpec((tq, LANES), lambda b, h, i: (b * nq + i, T_AQ * hpt + h)),
            pl.BlockSpec((seq, LANES), lambda b, h, i: (b, T_AK * hpt + h)),
            pl.BlockSpec((seq, LANES), lambda b, h, i: (b, T_AV * hpt + h)),
            pl.BlockSpec((1, 1, nc, tc, tq), lambda b, h, i: (b, i, 0, 0, 0)),
        ],
        out_specs=pl.BlockSpec((tq, LANES), lambda b, h, i: (b * nq + i, h)),
        out_shape=jax.ShapeDtypeStruct((T, BRANCH_WIDTH), BF16),
        scratch_shapes=[
            pltpu.VMEM((2, cpi, tc, tq), F32),
            pltpu.VMEM((1, tq), F32),
            pltpu.VMEM((1, tq), F32),
            pltpu.VMEM((HEAD_DIM, tq), F32),
        ],
        compiler_params=_cparams(("parallel", "parallel", "arbitrary")),
        name="dsa_attn",
    )(proj, proj, proj, mask)


def _merge_kernel(cb_ref, cc_ref, ch_ref, ccp_ref, chp_ref, su_ref, sv_ref, yc_ref, yd_ref,
                  hn_ref, res_ref, convw_ref, sgun_ref, sguw_ref, sgub_ref,
                  wg0_ref, wg1_ref, wg2_ref, wg3_ref, wb_ref, wo_ref, fng_ref, router_ref,
                  h_ref, hn2_ref, ridx_ref, rw_ref, ycat_scr, acc_scr, *, tm, rows_per_seq, n_j, n_experts):
    i = pl.program_id(0)
    j = pl.program_id(1)

    @pl.when(j == 0)
    def _():
        acc_scr[...] = jnp.zeros(acc_scr.shape, F32)
        z = cc_ref[...].astype(F32) * ch_ref[...].astype(F32)
        first = ((i * tm) % rows_per_seq) == 0
        zp = ccp_ref[...].astype(F32) * chp_ref[...].astype(F32)
        zp = zp * jnp.where(first, 0.0, 1.0).astype(F32)
        row = lax.broadcasted_iota(I32, z.shape, 0)
        z1 = jnp.where(row == 0, zp[7:8, :], pltpu.roll(z, 1, 0))
        z2 = jnp.where(row == 0, zp[6:7, :], jnp.where(row == 1, zp[7:8, :], pltpu.roll(z, 2, 0)))
        cw = convw_ref[...]
        conv = cw[0:1, :] * z2 + cw[1:2, :] * z1 + cw[2:3, :] * z
        ycat_scr[:, 0:BRANCH_WIDTH] = (cb_ref[...].astype(F32) * conv).astype(BF16)
        u = _gelu_tanh(su_ref[...].astype(F32))
        v = _rmsnorm_rows(_gelu_tanh(sv_ref[...].astype(F32)), sgun_ref[...]).astype(BF16)
        tri_r = lax.broadcasted_iota(I32, (CHUNK, CHUNK), 0)
        tri_c = lax.broadcasted_iota(I32, (CHUNK, CHUNK), 1)
        bias = sgub_ref[...]
        for g in range(BRANCH_WIDTH // LANES):
            wg = jnp.where(tri_c <= tri_r, sguw_ref[g], 0.0).astype(BF16)
            cols = slice(g * LANES, (g + 1) * LANES)
            for ck in range(tm // CHUNK):
                rows = slice(ck * CHUNK, (ck + 1) * CHUNK)
                y = _dot(wg, v[rows, cols]) + bias[:, g:g + 1]
                ycat_scr[rows, BRANCH_WIDTH + g * LANES:BRANCH_WIDTH + (g + 1) * LANES] = (
                    u[rows, cols] * y).astype(BF16)
        ycat_scr[:, 2 * BRANCH_WIDTH:3 * BRANCH_WIDTH] = yc_ref[...]
        ycat_scr[:, 3 * BRANCH_WIDTH:4 * BRANCH_WIDTH] = yd_ref[...]

    hn = hn_ref[...]
    merged = None
    for n, wg_ref in enumerate((wg0_ref, wg1_ref, wg2_ref, wg3_ref)):
        gate = jax.nn.sigmoid(_dot(hn, wg_ref[...]))
        br = _dot(ycat_scr[:, n * BRANCH_WIDTH:(n + 1) * BRANCH_WIDTH], wb_ref[n])
        merged = gate * br if merged is None else merged + gate * br
    acc_scr[...] += _dot(merged.astype(BF16), wo_ref[...])

    @pl.when(j == n_j - 1)
    def _():
        h_new = res_ref[...] + acc_scr[...]
        h_ref[...] = h_new
        hn2 = _rmsnorm_rows(h_new, fng_ref[...])
        hb = hn2.astype(BF16)
        hn2_ref[...] = hb
        logits = _dot(hb, router_ref[...])
        lane = lax.broadcasted_iota(I32, logits.shape, 1)
        logits = jnp.where(lane < n_experts, logits, -jnp.inf)
        v1 = jnp.max(logits, axis=-1, keepdims=True)
        i1 = jnp.min(jnp.where(logits == v1, lane, LANES), axis=-1, keepdims=True)
        rest = jnp.where(lane == i1, -jnp.inf, logits)
        v2 = jnp.max(rest, axis=-1, keepdims=True)
        i2 = jnp.min(jnp.where(rest == v2, lane, LANES), axis=-1, keepdims=True)
        e2 = jnp.exp(v2 - v1)
        w1 = 1.0 / (1.0 + e2)
        w2 = e2 / (1.0 + e2)
        ridx_ref[...] = jnp.where(lane == 0, i1, jnp.where(lane == 1, i2, 0))
        rw_ref[...] = jnp.where(lane == 0, w1, jnp.where(lane == 1, w2, 0.0))


def _merge_call(proj, y_c, y_d, hn, h2, conv_w, sgu_norm, sgu_w, sgu_b_t, w_gate, w_branch, w_out,
                ffn_g, router_pad, seq, n_experts):
    T, D = h2.shape
    tm = _pick(seq, 512)
    tn = _pick(D, 256)
    n_j = D // tn
    rb8 = tm // 8
    kern = functools.partial(_merge_kernel, tm=tm, rows_per_seq=seq, n_j=n_j, n_experts=n_experts)
    tile = lambda t: pl.BlockSpec((tm, BRANCH_WIDTH), lambda i, j, t=t: (i, t))
    prev = lambda t: pl.BlockSpec((8, BRANCH_WIDTH), lambda i, j, t=t: (jnp.maximum(i * rb8 - 1, 0), t))
    full = lambda shp: pl.BlockSpec(shp, lambda i, j: (0,) * len(shp))
    gate = lambda n: pl.BlockSpec((D, tn), lambda i, j, n=n: (0, n * n_j + j))
    rows = lambda w: pl.BlockSpec((tm, w), lambda i, j: (i, 0))
    return pl.pallas_call(
        kern,
        grid=(T // tm, n_j),
        in_specs=[
            tile(T_CB), tile(T_CC), tile(T_CH), prev(T_CC), prev(T_CH), tile(T_SU), tile(T_SV),
            rows(BRANCH_WIDTH), rows(BRANCH_WIDTH), rows(D), rows(D),
            full((CONV_K, BRANCH_WIDTH)), full((1, BRANCH_WIDTH)),
            full((BRANCH_WIDTH // LANES, CHUNK, CHUNK)), full((CHUNK, BRANCH_WIDTH // LANES)),
            gate(0), gate(1), gate(2), gate(3),
            pl.BlockSpec((N_BRANCHES, BRANCH_WIDTH, tn), lambda i, j: (0, 0, j)),
            pl.BlockSpec((tn, D), lambda i, j: (j, 0)),
            full((1, D)), full((D, LANES)),
        ],
        out_specs=[rows(D), rows(D), rows(LANES), rows(LANES)],
        out_shape=[
            jax.ShapeDtypeStruct((T, D), F32),
            jax.ShapeDtypeStruct((T, D), BF16),
            jax.ShapeDtypeStruct((T, LANES), I32),
            jax.ShapeDtypeStruct((T, LANES), F32),
        ],
        scratch_shapes=[pltpu.VMEM((tm, N_BRANCHES * BRANCH_WIDTH), BF16), pltpu.VMEM((tm, D), F32)],
        compiler_params=_cparams(("parallel", "arbitrary"), MERGE_VMEM_LIMIT),
        name="merge",
    )(proj, proj, proj, proj, proj, proj, proj, y_c, y_d, hn, h2, conv_w, sgu_norm, sgu_w, sgu_b_t,
      w_gate, w_gate, w_gate, w_gate, w_branch, w_out, ffn_g, router_pad)


def _ffn_kernel(x_ref, comb_ref, w1_ref, w3_ref, w2_ref, res_ref, g_ref, o_ref, acc_e, acc_t,
                *, n_e, n_f, final_norm):
    e = pl.program_id(1)
    f = pl.program_id(2)

    @pl.when((e == 0) & (f == 0))
    def _():
        acc_t[...] = jnp.zeros(acc_t.shape, F32)

    @pl.when(f == 0)
    def _():
        acc_e[...] = jnp.zeros(acc_e.shape, F32)

    x = x_ref[...]
    h1 = _dot(x, w1_ref[...])
    h3 = _dot(x, w3_ref[...])
    a = (h1 * jax.nn.sigmoid(h1)) * h3
    acc_e[...] += _dot(a.astype(BF16), w2_ref[...])

    @pl.when(f == n_f - 1)
    def _():
        comb = comb_ref[...]
        lane = lax.broadcasted_iota(I32, comb.shape, 1)
        c = jnp.sum(jnp.where(lane == e, comb, 0.0), axis=-1, keepdims=True)
        acc_t[...] += c * acc_e[...]

    @pl.when((e == n_e - 1) & (f == n_f - 1))
    def _():
        out = res_ref[...] + acc_t[...]
        if final_norm:
            out = _rmsnorm_rows(out, g_ref[...])
        o_ref[...] = out


def _ffn_call(xb, comb, w1, w3, w2, res, g, final_norm):
    T, D = res.shape
    n_e, _, d_ff = w1.shape
    tm = _pick(T, 512)
    tf = _pick(d_ff, 512)
    n_f = d_ff // tf
    kern = functools.partial(_ffn_kernel, n_e=n_e, n_f=n_f, final_norm=final_norm)
    rows = lambda w: pl.BlockSpec((tm, w), lambda i, e, f: (i, 0))
    return pl.pallas_call(
        kern,
        grid=(T // tm, n_e, n_f),
        in_specs=[
            rows(D), rows(LANES),
            pl.BlockSpec((None, D, tf), lambda i, e, f: (e, 0, f)),
            pl.BlockSpec((None, D, tf), lambda i, e, f: (e, 0, f)),
            pl.BlockSpec((None, tf, D), lambda i, e, f: (e, f, 0)),
            rows(D),
            pl.BlockSpec((1, D), lambda i, e, f: (0, 0)),
        ],
        out_specs=rows(D),
        out_shape=jax.ShapeDtypeStruct((T, D), F32),
        scratch_shapes=[pltpu.VMEM((tm, D), F32), pltpu.VMEM((tm, D), F32)],
        compiler_params=_cparams(("parallel", "arbitrary", "arbitrary")),
        name="ffn",
    )(xb, comb, w1, w3, w2, res, g)


def _row_copy(src_hbm, row, dst_vmem, slot, sem):
    return pltpu.make_async_copy(src_hbm.at[pl.ds(row, 1), :], dst_vmem.at[pl.ds(slot, 1), :], sem)


def _moe_gather_kernel(tok_ref, h_hbm, g_ref, o_ref, buf, sem, *, tg):
    def start(r, c):
        _row_copy(h_hbm, tok_ref[0, 0, r], buf, r, sem).start()
        return c

    lax.fori_loop(0, tg, start, 0)

    def wait(r, c):
        _row_copy(h_hbm, tok_ref[0, 0, r], buf, r, sem).wait()
        return c

    lax.fori_loop(0, tg, wait, 0)
    o_ref[...] = _rmsnorm_rows(buf[...], g_ref[...]).astype(BF16)


def _moe_gather_call(h2, g, row_token, tg):
    T, D = h2.shape
    R = row_token.shape[0]
    kern = functools.partial(_moe_gather_kernel, tg=tg)
    return pl.pallas_call(
        kern,
        grid=(R // tg,),
        in_specs=[
            pl.BlockSpec((1, 1, tg), lambda i: (i, 0, 0), memory_space=pltpu.SMEM),
            pl.BlockSpec(memory_space=pl.ANY),
            pl.BlockSpec((1, D), lambda i: (0, 0)),
        ],
        out_specs=pl.BlockSpec((tg, D), lambda i: (i, 0)),
        out_shape=jax.ShapeDtypeStruct((R, D), BF16),
        scratch_shapes=[pltpu.VMEM((tg, D), F32), pltpu.SemaphoreType.DMA(())],
        compiler_params=_cparams(("arbitrary",)),
        name="moe_gather",
    )(row_token.reshape(R // tg, 1, tg), h2, g)


def _ffn_routed_kernel(te_ref, nu_ref, x_ref, rw_ref, w1_ref, w3_ref, w2_ref, o_ref, acc, *, n_f):
    i = pl.program_id(0)
    f = pl.program_id(1)

    @pl.when(f == 0)
    def _():
        acc[...] = jnp.zeros(acc.shape, F32)

    @pl.when(i < nu_ref[0])
    def _():
        x = x_ref[...]
        h1 = _dot(x, w1_ref[...])
        h3 = _dot(x, w3_ref[...])
        a = (h1 * jax.nn.sigmoid(h1)) * h3
        acc[...] += _dot(a.astype(BF16), w2_ref[...])

    @pl.when(f == n_f - 1)
    def _():
        o_ref[...] = rw_ref[...] * acc[...]


def _ffn_routed_call(xs, row_w, tile_e, n_used, w1, w3, w2, tm):
    R, D = xs.shape
    d_ff = w1.shape[-1]
    tf = _pick(d_ff, 512)
    n_f = d_ff // tf
    kern = functools.partial(_ffn_routed_kernel, n_f=n_f)

    def wmap(i, f, te, nu):
        return te[i], jnp.where(i < nu[0], f, n_f - 1)

    return pl.pallas_call(
        kern,
        grid_spec=pltpu.PrefetchScalarGridSpec(
            num_scalar_prefetch=2,
            grid=(R // tm, n_f),
            in_specs=[
                pl.BlockSpec((tm, D), lambda i, f, te, nu: (i, 0)),
                pl.BlockSpec((tm, 1), lambda i, f, te, nu: (i, 0)),
                pl.BlockSpec((None, D, tf), lambda i, f, te, nu: (wmap(i, f, te, nu)[0], 0, wmap(i, f, te, nu)[1])),
                pl.BlockSpec((None, D, tf), lambda i, f, te, nu: (wmap(i, f, te, nu)[0], 0, wmap(i, f, te, nu)[1])),
                pl.BlockSpec((None, tf, D), lambda i, f, te, nu: (wmap(i, f, te, nu)[0], wmap(i, f, te, nu)[1], 0)),
            ],
            out_specs=pl.BlockSpec((tm, D), lambda i, f, te, nu: (i, 0)),
            scratch_shapes=[pltpu.VMEM((tm, D), F32)],
        ),
        out_shape=jax.ShapeDtypeStruct((R, D), F32),
        compiler_params=_cparams(("arbitrary", "arbitrary")),
        name="ffn_routed",
    )(tile_e, n_used, xs, row_w, w1, w3, w2)


def _moe_combine_kernel(pos_ref, h_ref, y_hbm, g_ref, o_ref, buf0, buf1, sem, *, tc, final_norm):
    def start(r, c):
        _row_copy(y_hbm, pos_ref[0, 0, 2 * r], buf0, r, sem).start()
        _row_copy(y_hbm, pos_ref[0, 0, 2 * r + 1], buf1, r, sem).start()
        return c

    lax.fori_loop(0, tc, start, 0)

    def wait(r, c):
        _row_copy(y_hbm, pos_ref[0, 0, 2 * r], buf0, r, sem).wait()
        _row_copy(y_hbm, pos_ref[0, 0, 2 * r + 1], buf1, r, sem).wait()
        return c

    lax.fori_loop(0, tc, wait, 0)
    out = h_ref[...] + (buf0[...] + buf1[...])
    if final_norm:
        out = _rmsnorm_rows(out, g_ref[...])
    o_ref[...] = out


def _moe_combine_call(h2, ys, pos, g, final_norm):
    T, D = h2.shape
    tc = _pick(T, 256)
    kern = functools.partial(_moe_combine_kernel, tc=tc, final_norm=final_norm)
    return pl.pallas_call(
        kern,
        grid=(T // tc,),
        in_specs=[
            pl.BlockSpec((1, 1, 2 * tc), lambda i: (i, 0, 0), memory_space=pltpu.SMEM),
            pl.BlockSpec((tc, D), lambda i: (i, 0)),
            pl.BlockSpec(memory_space=pl.ANY),
            pl.BlockSpec((1, D), lambda i: (0, 0)),
        ],
        out_specs=pl.BlockSpec((tc, D), lambda i: (i, 0)),
        out_shape=jax.ShapeDtypeStruct((T, D), F32),
        scratch_shapes=[pltpu.VMEM((tc, D), F32), pltpu.VMEM((tc, D), F32), pltpu.SemaphoreType.DMA(())],
        compiler_params=_cparams(("arbitrary",)),
        name="moe_combine",
    )(pos.reshape(T // tc, 1, 2 * tc), h2, ys, g)


def _route(ridx, rw, n_experts, tm):
    T = ridx.shape[0]
    flat_e = ridx[:, :2].reshape(-1)
    onehot = (flat_e[:, None] == jnp.arange(n_experts, dtype=I32)[None, :]).astype(I32)
    cum = jnp.cumsum(onehot, axis=0)
    rank = jnp.take_along_axis(cum, flat_e[:, None], axis=1)[:, 0] - 1
    counts = cum[-1]
    padded = ((counts + tm - 1) // tm) * tm
    gend = jnp.cumsum(padded)
    gstart = gend - padded
    pos = gstart[flat_e] + rank
    R = 2 * T + n_experts * tm
    row_token = jnp.zeros((R,), I32).at[pos].set(jnp.arange(2 * T, dtype=I32) // 2)
    row_w = jnp.zeros((R,), F32).at[pos].set(rw[:, :2].reshape(-1))
    tile_e = jnp.searchsorted(gend, jnp.arange(R // tm, dtype=I32) * tm, side="right")
    tile_e = jnp.minimum(tile_e, n_experts - 1).astype(I32)
    n_used = (gend[-1:] // tm).astype(I32)
    return pos.astype(I32), row_token, row_w[:, None], tile_e, n_used


def _rope_tables(seq):
    pos = jnp.arange(seq, dtype=F32)

    def tab(dim):
        inv = ROPE_THETA ** (-jnp.arange(0, dim, 2, dtype=F32) / dim)
        ang = pos[:, None] * inv[None, :]
        cos = jnp.concatenate([jnp.cos(ang), jnp.cos(ang)], axis=-1)
        sin = jnp.concatenate([-jnp.sin(ang), jnp.sin(ang)], axis=-1)
        return cos, sin

    c64, s64 = tab(IDX_DIM)
    c128, s128 = tab(HEAD_DIM)
    return (jnp.tile(c64, (1, 2)), jnp.tile(s64, (1, 2)), c128, s128)


def kernel(x, attn_norm, w_in, conv_w, sgu_norm, sgu_w, sgu_b, diff_lambda, diff_subln, w_branch, w_out,
           ffn_norm, dense_w1, dense_w3, dense_w2, router, moe_w1, moe_w3, moe_w2, final_norm):
    batch, seq, D = x.shape
    depth = w_in.shape[0]
    n_experts = router.shape[-1]
    T = batch * seq
    rope = _rope_tables(seq)
    h = x.reshape(T, D)
    ik0 = MAIN_WIDTH
    iw0 = ik0 + IDX_DIM
    g0 = iw0 + IDX_HEADS
    ones_comb = jnp.zeros((T, LANES), F32).at[:, 0].set(1.0)

    for layer in range(depth):
        wl = w_in[layer]
        w_main = wl[:, :MAIN_WIDTH].astype(BF16)
        w_small = jnp.concatenate(
            [wl[:, ik0:iw0], wl[:, ik0:iw0], wl[:, iw0:g0], jnp.zeros((D, LANES - IDX_HEADS), F32)],
            axis=1).astype(BF16)
        w_gate = wl[:, g0:].astype(BF16)
        lam_init = 0.8 - 0.6 * math.exp(-0.3 * layer)

        proj, hn, ikd, iw = _proj_call(h, attn_norm[layer][None, :], w_main, w_small, rope, seq)
        y_c = _diff_call(proj, diff_lambda[layer], diff_subln[layer][:, None], lam_init, batch, seq)
        mask = _dsa_select_call(proj, ikd, iw[:, :IDX_HEADS].T, batch, seq)
        y_d = _dsa_attn_call(proj, mask, batch, seq)

        j = layer // 2
        is_moe = layer % 2 == 1
        router_pad = jnp.zeros((D, LANES), F32)
        if is_moe:
            router_pad = router_pad.at[:, :n_experts].set(router[j])
        h, hn2, ridx, rw = _merge_call(
            proj, y_c, y_d, hn, h, conv_w[layer], sgu_norm[layer][None, :], sgu_w[layer],
            sgu_b[layer].T, w_gate, w_branch[layer].astype(BF16), w_out[layer].astype(BF16),
            ffn_norm[layer][None, :], router_pad.astype(BF16), seq, n_experts)

        last = layer == depth - 1
        if is_moe:
            tm = _pick(T, 1024)
            pos, row_token, row_w, tile_e, n_used = _route(ridx, rw, n_experts, tm)
            xs = _moe_gather_call(h, ffn_norm[layer][None, :], row_token, _pick(tm, 512))
            ys = _ffn_routed_call(xs, row_w, tile_e, n_used, moe_w1[j].astype(BF16),
                                  moe_w3[j].astype(BF16), moe_w2[j].astype(BF16), tm)
            h = _moe_combine_call(h, ys, pos, final_norm[None, :], last)
        else:
            h = _ffn_call(hn2, ones_comb, dense_w1[j][None].astype(BF16), dense_w3[j][None].astype(BF16),
                          dense_w2[j][None].astype(BF16), h, final_norm[None, :], last)

    if depth == 0:
        raise ValueError("depth must be positive")
    return h.reshape(batch, seq, D)
```

```python
import functools
import math

import jax
import jax.numpy as jnp
from jax import lax
from jax.experimental import pallas as pl
from jax.experimental.pallas import tpu as pltpu

F32 = jnp.float32
BF16 = jnp.bfloat16
I32 = jnp.int32

LANES = 128
SLAB = 8
N_COUNTERS = 4
PACK = 16
INF_KEY16 = 0x7F80
MIN_NORMAL_KEY16 = 0x0080
MIN_NORMAL_F32 = 1.1754943508222875e-38
HEAD_DIM = 128
BRANCH_WIDTH = 512
N_BRANCHES = 4
N_HEADS = 4
IDX_HEADS = 8
IDX_DIM = 64
DIFF_SUB_DIM = 64
CONV_K = 3
CHUNK = 128
TOPK_MAX = 256
ROPE_THETA = 10000.0
EPS = 1e-6
N_MAIN_TILES = 12
MAIN_WIDTH = N_MAIN_TILES * BRANCH_WIDTH
(T_CB, T_CC, T_CH, T_SU, T_SV, T_DQ, T_DK, T_DV, T_AQ, T_AK, T_AV, T_IQ) = range(12)
NEG_BIG = -1e30
INT_MIN = -(2 ** 31)
VMEM_LIMIT = 56 * 1024 * 1024
MERGE_VMEM_LIMIT = 61 * 1024 * 1024


def _pick(n, pref):
    t = min(pref, n)
    while n % t:
        t //= 2
    return t


def _cparams(sem, vmem_limit=VMEM_LIMIT):
    return pltpu.CompilerParams(dimension_semantics=sem, vmem_limit_bytes=vmem_limit)


def _dot(a, b):
    return jnp.dot(a, b, preferred_element_type=F32)


def _dot_nt(a, b):
    return lax.dot_general(a, b, (((1,), (1,)), ((), ())), preferred_element_type=F32)


def _dot_tn(a, b):
    return lax.dot_general(a, b, (((0,), (0,)), ((), ())), preferred_element_type=F32)


def _rmsnorm_rows(x, g):
    ms = jnp.mean(x * x, axis=-1, keepdims=True)
    return (x * lax.rsqrt(ms + EPS)) * g


def _rope64(x, c, s):
    lane = lax.broadcasted_iota(I32, x.shape, 1)
    rot = jnp.where((lane & 32) == 0, pltpu.roll(x, 96, 1), pltpu.roll(x, 32, 1))
    return x * c + rot * s


def _rope128(x, c, s):
    return x * c + pltpu.roll(x, 64, 1) * s


def _gelu_tanh(x):
    c = math.sqrt(2.0 / math.pi)
    return x * (0.5 * (1.0 + jnp.tanh(c * (x + 0.044715 * (x * x * x)))))


def _proj_kernel(x_ref, g_ref, w_ref, ws_ref, c64_ref, s64_ref, c128_ref, s128_ref,
                 proj_ref, hn_ref, ikd_ref, iw_ref, hn_scr):
    j = pl.program_id(1)

    @pl.when(j == 0)
    def _():
        hb = _rmsnorm_rows(x_ref[...], g_ref[...]).astype(BF16)
        hn_scr[...] = hb
        hn_ref[...] = hb
        small = _dot(hb, ws_ref[...])
        ikd_ref[...] = _rope64(small[:, :LANES], c64_ref[...], s64_ref[...]).astype(BF16)
        iw_ref[...] = small[:, LANES:]

    acc = _dot(hn_scr[...], w_ref[...])
    is64 = (j == T_DQ) | (j == T_DK) | (j == T_IQ)
    is128 = (j == T_AQ) | (j == T_AK)

    @pl.when(jnp.logical_not(is64 | is128))
    def _():
        proj_ref[...] = acc.astype(BF16)

    @pl.when(is64)
    def _():
        scale = jnp.where(j == T_DQ, DIFF_SUB_DIM ** -0.5, 1.0).astype(F32)
        c, s = c64_ref[...], s64_ref[...]
        for cb in range(BRANCH_WIDTH // LANES):
            sl = slice(cb * LANES, (cb + 1) * LANES)
            proj_ref[:, sl] = (_rope64(acc[:, sl], c, s) * scale).astype(BF16)

    @pl.when(is128)
    def _():
        c, s = c128_ref[...], s128_ref[...]
        for cb in range(BRANCH_WIDTH // LANES):
            sl = slice(cb * LANES, (cb + 1) * LANES)
            proj_ref[:, sl] = _rope128(acc[:, sl], c, s).astype(BF16)


def _proj_call(h2, g, w_main, w_small, rope, seq):
    T, D = h2.shape
    tm = _pick(seq, 1024)
    nrb = seq // tm
    rope_spec = pl.BlockSpec((tm, LANES), lambda i, j: (i % nrb, 0))
    row_spec = lambda w: pl.BlockSpec((tm, w), lambda i, j: (i, 0))
    return pl.pallas_call(
        _proj_kernel,
        grid=(T // tm, N_MAIN_TILES),
        in_specs=[
            row_spec(D),
            pl.BlockSpec((1, D), lambda i, j: (0, 0)),
            pl.BlockSpec((D, BRANCH_WIDTH), lambda i, j: (0, j)),
            pl.BlockSpec((D, 2 * LANES), lambda i, j: (0, 0)),
            rope_spec, rope_spec, rope_spec, rope_spec,
        ],
        out_specs=[
            pl.BlockSpec((tm, BRANCH_WIDTH), lambda i, j: (i, j)),
            row_spec(D), row_spec(LANES), row_spec(LANES),
        ],
        out_shape=[
            jax.ShapeDtypeStruct((T, MAIN_WIDTH), BF16),
            jax.ShapeDtypeStruct((T, D), BF16),
            jax.ShapeDtypeStruct((T, LANES), BF16),
            jax.ShapeDtypeStruct((T, LANES), F32),
        ],
        scratch_shapes=[pltpu.VMEM((tm, D), BF16)],
        compiler_params=_cparams(("parallel", "arbitrary")),
        name="proj",
    )(h2, g, w_main, w_small, *rope)


def _softmax_step_t(parts, m_scr, l_scr, acc_scr):
    m_prev = m_scr[...]
    m_new = m_prev
    for s_t, _ in parts:
        m_new = jnp.maximum(m_new, jnp.max(s_t, axis=0, keepdims=True))
    alpha = jnp.exp(m_prev - m_new)
    l_new = alpha * l_scr[...]
    acc = alpha * acc_scr[...]
    for s_t, v in parts:
        p_t = jnp.exp(s_t - m_new)
        l_new = l_new + jnp.sum(p_t, axis=0, keepdims=True)
        acc = acc + _dot_tn(v, p_t.astype(BF16))
    l_scr[...] = l_new
    acc_scr[...] = acc
    m_scr[...] = m_new


def _init_softmax_state(m_scr, l_scr, acc_scr):
    m_scr[...] = jnp.full(m_scr.shape, NEG_BIG, F32)
    l_scr[...] = jnp.zeros(l_scr.shape, F32)
    acc_scr[...] = jnp.zeros(acc_scr.shape, F32)


def _pipelined_tiles(n, qk, sm):
    qk(0, 0)

    def pair(p, carry):
        qk(2 * p + 1, 1)
        sm(2 * p, 0, False)
        qk(2 * p + 2, 0)
        sm(2 * p + 1, 1, False)
        return carry

    n_pairs = (n - 1) // 2
    lax.fori_loop(0, n_pairs, pair, 0)
    rest = n - 2 * n_pairs

    @pl.when(rest == 1)
    def _():
        sm(n - 1, 0, True)

    @pl.when(rest == 2)
    def _():
        qk(n - 1, 1)
        sm(n - 2, 0, False)
        sm(n - 1, 1, True)


def _diff_kernel(q_ref, k_ref, v_ref, dl_ref, subln_ref, o_ref,
                 qq_scr, s_scr, m_scr, l_scr, acc_scr, *, tq, lam_init):
    qi = pl.program_id(2)
    q = q_ref[...]
    lane = lax.broadcasted_iota(I32, q.shape, 1)
    zero = jnp.zeros_like(q)
    qq_scr[0:tq, :] = jnp.where(lane < DIFF_SUB_DIM, q, zero)
    qq_scr[tq:2 * tq, :] = jnp.where(lane >= DIFF_SUB_DIM, q, zero)
    _init_softmax_state(m_scr, l_scr, acc_scr)

    def qk(kj, slot):
        off = pl.multiple_of(kj * tq, tq)
        s_scr[slot] = _dot_nt(k_ref[pl.ds(off, tq), :], qq_scr[...])

    def sm(kj, slot, causal):
        off = pl.multiple_of(kj * tq, tq)
        s_t = s_scr[slot]
        if causal:
            kpos = lax.broadcasted_iota(I32, s_t.shape, 0)
            qpos = lax.broadcasted_iota(I32, s_t.shape, 1)
            qpos = jnp.where(qpos >= tq, qpos - tq, qpos)
            s_t = jnp.where(kpos <= qpos, s_t, NEG_BIG)
        _softmax_step_t([(s_t, v_ref[pl.ds(off, tq), :])], m_scr, l_scr, acc_scr)

    _pipelined_tiles(qi + 1, qk, sm)

    dl = dl_ref[...]
    lam = (jnp.exp(jnp.sum(dl[0:1] * dl[1:2], axis=-1, keepdims=True))
           - jnp.exp(jnp.sum(dl[2:3] * dl[3:4], axis=-1, keepdims=True)) + lam_init)
    o_all = acc_scr[...] / l_scr[...]
    o_t = o_all[:, 0:tq] - lam * o_all[:, tq:2 * tq]
    ms = jnp.mean(o_t * o_t, axis=0, keepdims=True)
    o_t = (o_t * lax.rsqrt(ms + EPS)) * subln_ref[...] * (1.0 - lam_init)
    o_ref[...] = o_t.T.astype(BF16)


def _diff_call(proj, diff_lambda, subln_col, lam_init, batch, seq):
    T = proj.shape[0]
    tq = _pick(seq, 512)
    nq = seq // tq
    hpt = BRANCH_WIDTH // LANES
    kern = functools.partial(_diff_kernel, tq=tq, lam_init=lam_init)
    return pl.pallas_call(
        kern,
        grid=(batch, N_HEADS, nq),
        in_specs=[
            pl.BlockSpec((tq, LANES), lambda b, h, i: (b * nq + i, T_DQ * hpt + h)),
            pl.BlockSpec((seq, LANES), lambda b, h, i: (b, T_DK * hpt + h)),
            pl.BlockSpec((seq, LANES), lambda b, h, i: (b, T_DV * hpt + h)),
            pl.BlockSpec((4, DIFF_SUB_DIM), lambda b, h, i: (0, 0)),
            pl.BlockSpec((HEAD_DIM, 1), lambda b, h, i: (0, 0)),
        ],
        out_specs=pl.BlockSpec((tq, LANES), lambda b, h, i: (b * nq + i, h)),
        out_shape=jax.ShapeDtypeStruct((T, BRANCH_WIDTH), BF16),
        scratch_shapes=[
            pltpu.VMEM((2 * tq, LANES), BF16),
            pltpu.VMEM((2, tq, 2 * tq), F32),
            pltpu.VMEM((1, 2 * tq), F32),
            pltpu.VMEM((1, 2 * tq), F32),
            pltpu.VMEM((HEAD_DIM, 2 * tq), F32),
        ],
        compiler_params=_cparams(("parallel", "parallel", "arbitrary")),
        name="diff_attn",
    )(proj, proj, proj, diff_lambda, subln_col)


def _dsa_select_kernel(iq_ref, ikd_ref, iwt_ref, mask_ref, key_scr, hi_scr, *, tq, tc, nc, topk, idx_bits):
    qi = pl.program_id(1)
    n_ch = (qi * tq + tq + tc - 1) // tc
    iq = iq_ref[...]
    iwt = iwt_ref[...]
    lane = lax.broadcasted_iota(I32, (tq, LANES), 1)
    zero = jnp.zeros((tq, LANES), BF16)
    qm = []
    for h in range(IDX_HEADS):
        blk = iq[:, (h // 2) * LANES:(h // 2 + 1) * LANES]
        keep = (lane < IDX_DIM) if h % 2 == 0 else (lane >= IDX_DIM)
        qm.append(jnp.where(keep, blk, zero))
    idx_scale = IDX_DIM ** -0.5 * IDX_HEADS ** -0.5
    qpos = qi * tq + lax.broadcasted_iota(I32, (tc, tq), 1)
    krow = lax.broadcasted_iota(I32, (tc, tq), 0)
    krow8 = lax.broadcasted_iota(I32, (SLAB, tq), 0)

    def score_chunk(c, carry):
        ik = ikd_ref[pl.ds(pl.multiple_of(c * tc, tc), tc), :]
        sc = jnp.zeros((tc, tq), F32)
        for h in range(IDX_HEADS):
            sc = sc + iwt[h:h + 1, :] * jnp.maximum(_dot_nt(ik, qm[h]), 0.0)
        sc = sc * idx_scale
        sc = jnp.where(jnp.abs(sc) < MIN_NORMAL_F32, 0.0, sc)
        sc = jnp.where(c * tc + krow <= qpos, sc, -jnp.inf)
        bits = pltpu.bitcast(sc, I32)
        key_scr[c] = jnp.where(bits < 0, INT_MIN - bits, bits)
        hi_scr[c] = sc.astype(BF16)
        return carry

    lax.fori_loop(0, n_ch, score_chunk, 0)

    def count(pred):
        def body(c, accs):
            accs = list(accs)
            for r in range(tc // SLAB):
                ks = key_scr[c, r * SLAB:(r + 1) * SLAB, :]
                accs[r % N_COUNTERS] = accs[r % N_COUNTERS] + jnp.where(pred(ks, c * tc + r * SLAB), 1, 0)
            return tuple(accs)
        zero = jnp.zeros((SLAB, tq), I32)
        accs = lax.fori_loop(0, n_ch, body, (zero,) * N_COUNTERS)
        return jnp.sum(functools.reduce(lambda a, b: a + b, accs), axis=0, keepdims=True)

    def rows8(x):
        return jnp.broadcast_to(x, (SLAB, tq))

    def count_ge(cand):
        c8 = rows8(cand)
        return count(lambda ks, base: ks >= c8)

    def count16(cand_val):
        c16 = jnp.broadcast_to(cand_val, (PACK, tq)).astype(BF16)
        one = jnp.ones((PACK, tq), BF16)
        zero16 = jnp.zeros((PACK, tq), BF16)

        def body(c, accs):
            accs = list(accs)
            for r in range(tc // PACK):
                hs = hi_scr[c, r * PACK:(r + 1) * PACK, :]
                accs[r % N_COUNTERS] = accs[r % N_COUNTERS] + jnp.where(hs >= c16, one, zero16)
            return tuple(accs)
        accs = lax.fori_loop(0, n_ch, body, (zero16,) * N_COUNTERS)
        tot = functools.reduce(lambda a, b: a + b, [a.astype(F32) for a in accs])
        return jnp.sum(tot, axis=0, keepdims=True)

    def key16_value(k16):
        k16 = jnp.clip(k16, -INF_KEY16, INF_KEY16)
        k16 = jnp.where((k16 > 0) & (k16 < MIN_NORMAL_KEY16), MIN_NORMAL_KEY16, k16)
        bits16 = jnp.where(k16 >= 0, k16, -32768 - k16)
        return pltpu.bitcast(jnp.left_shift(bits16, 16), F32)

    nonneg = count16(jnp.zeros((1, tq), F32)) >= topk
    thr16 = jnp.where(nonneg, 0, -32768).astype(I32)

    def bit16_body(it, thr16):
        cand = thr16 + jnp.left_shift(jnp.int32(1), 14 - it)
        return jnp.where(count16(key16_value(cand)) >= topk, cand, thr16)

    thr16 = lax.fori_loop(0, 15, bit16_body, thr16)

    thr = jnp.left_shift(thr16, 16) - 2 ** 15

    def bit_body(it, thr):
        cand = thr + jnp.left_shift(jnp.int32(1), 16 - it)
        return jnp.where(count_ge(cand) >= topk, cand, thr)

    thr = lax.fori_loop(0, 17, bit_body, thr)
    thr8 = rows8(thr)
    cnt_ge = count_ge(thr)
    excess = jnp.max(cnt_ge - topk) > 0

    def tie_search(_):
        need = topk - count(lambda ks, base: ks > thr8)

        def tbit(it, p):
            cand = p + jnp.left_shift(jnp.int32(1), idx_bits - 1 - it)
            cand8 = rows8(cand)
            cnt = count(lambda ks, base: (ks == thr8) & (base + krow8 < cand8))
            return jnp.where(cnt < need, cand, p)
        return lax.fori_loop(0, idx_bits, tbit, jnp.zeros((1, tq), I32))

    last = lax.cond(excess, tie_search, lambda _: jnp.full((1, tq), 2 ** idx_bits, I32), 0)

    def emit(c, carry):
        kc = key_scr[c]
        kpos = c * tc + krow
        sel = (kc > thr) | ((kc == thr) & (kpos <= last))
        sel = sel & (kpos <= qpos)
        mask_ref[0, 0, c] = jnp.where(sel, 1, 0).astype(jnp.int8)
        return carry

    lax.fori_loop(0, n_ch, emit, 0)

    def clear(c, carry):
        mask_ref[0, 0, c] = jnp.zeros((tc, tq), jnp.int8)
        return carry

    lax.fori_loop(n_ch, nc, clear, 0)


def _dsa_tiles(seq):
    tq = _pick(seq, 256)
    tc = _pick(seq, 512)
    return tq, tc


def _dsa_select_call(proj, ikd, iw_t, batch, seq):
    tq, tc = _dsa_tiles(seq)
    nq, nc = seq // tq, seq // tc
    topk = min(TOPK_MAX, seq // 4)
    assert tc >= topk
    assert (tc // PACK // N_COUNTERS) * nc <= 256
    idx_bits = max(1, (seq - 1).bit_length())
    kern = functools.partial(_dsa_select_kernel, tq=tq, tc=tc, nc=nc, topk=topk, idx_bits=idx_bits)
    return pl.pallas_call(
        kern,
        grid=(batch, nq),
        in_specs=[
            pl.BlockSpec((tq, BRANCH_WIDTH), lambda b, i: (b * nq + i, T_IQ)),
            pl.BlockSpec((seq, LANES), lambda b, i: (b, 0)),
            pl.BlockSpec((IDX_HEADS, tq), lambda b, i: (0, b * nq + i)),
        ],
        out_specs=pl.BlockSpec((1, 1, nc, tc, tq), lambda b, i: (b, i, 0, 0, 0)),
        out_shape=jax.ShapeDtypeStruct((batch, nq, nc, tc, tq), jnp.int8),
        scratch_shapes=[pltpu.VMEM((nc, tc, tq), I32), pltpu.VMEM((nc, tc, tq), BF16)],
        compiler_params=_cparams(("parallel", "arbitrary")),
        name="dsa_select",
    )(proj, ikd, iw_t)


def _dsa_attn_kernel(q_ref, k_ref, v_ref, mask_ref, o_ref, s_scr, m_scr, l_scr, acc_scr, *, tq, tc, cpi):
    qi = pl.program_id(2)
    n_ch = (qi * tq + tq + tc - 1) // tc
    q = q_ref[...]
    scale = HEAD_DIM ** -0.5
    _init_softmax_state(m_scr, l_scr, acc_scr)

    def qk(it, slot):
        for u in range(cpi):
            c = it * cpi + u
            off = pl.multiple_of(c * tc, tc)
            s_t = _dot_nt(k_ref[pl.ds(off, tc), :], q) * scale
            sel = mask_ref[0, 0, c].astype(I32) != 0
            s_scr[slot, u] = jnp.where(sel, s_t, NEG_BIG)

    def sm(it, slot, last):
        parts = []
        for u in range(cpi):
            off = pl.multiple_of((it * cpi + u) * tc, tc)
            parts.append((s_scr[slot, u], v_ref[pl.ds(off, tc), :]))
        _softmax_step_t(parts, m_scr, l_scr, acc_scr)

    _pipelined_tiles((n_ch + cpi - 1) // cpi, qk, sm)
    o_ref[...] = (acc_scr[...] / l_scr[...]).T.astype(BF16)


def _dsa_attn_call(proj, mask, batch, seq):
    T = proj.shape[0]
    tq, tc = _dsa_tiles(seq)
    nq, nc = seq // tq, seq // tc
    hpt = BRANCH_WIDTH // LANES
    cpi = 2 if nc % 2 == 0 else 1
    kern = functools.partial(_dsa_attn_kernel, tq=tq, tc=tc, cpi=cpi)
    return pl.pallas_call(
        kern,
        grid=(batch, N_HEADS, nq),
        in_specs=[
            pl.BlockSpec((tq, LANES), lambda b, h, i: (b * nq + i, T_AQ * hpt + h)),
            pl.BlockSpec((seq, LANES), lambda b, h, i: (b, T_AK * hpt + h)),
            pl.BlockSpec((seq, LANES), lambda b, h, i: (b, T_AV * hpt + h)),
            pl.BlockSpec((1, 1, nc, tc, tq), lambda b, h, i: (b, i, 0, 0, 0)),
        ],
        out_specs=pl.BlockSpec((tq, LANES), lambda b, h, i: (b * nq + i, h)),
        out_shape=jax.ShapeDtypeStruct((T, BRANCH_WIDTH), BF16),
        scratch_shapes=[
            pltpu.VMEM((2, cpi, tc, tq), F32),
            pltpu.VMEM((1, tq), F32),
            pltpu.VMEM((1, tq), F32),
            pltpu.VMEM((HEAD_DIM, tq), F32),
        ],
        compiler_params=_cparams(("parallel", "parallel", "arbitrary")),
        name="dsa_attn",
    )(proj, proj, proj, mask)


def _merge_kernel(cb_ref, cc_ref, ch_ref, ccp_ref, chp_ref, su_ref, sv_ref, yc_ref, yd_ref,
                  hn_ref, res_ref, convw_ref, sgun_ref, sguw_ref, sgub_ref,
                  wg0_ref, wg1_ref, wg2_ref, wg3_ref, wb_ref, wo_ref, fng_ref, router_ref,
                  h_ref, hn2_ref, ridx_ref, rw_ref, ycat_scr, acc_scr, *, tm, rows_per_seq, n_j, n_experts):
    i = pl.program_id(0)
    j = pl.program_id(1)

    @pl.when(j == 0)
    def _():
        acc_scr[...] = jnp.zeros(acc_scr.shape, F32)
        z = cc_ref[...].astype(F32) * ch_ref[...].astype(F32)
        first = ((i * tm) % rows_per_seq) == 0
        zp = ccp_ref[...].astype(F32) * chp_ref[...].astype(F32)
        zp = zp * jnp.where(first, 0.0, 1.0).astype(F32)
        row = lax.broadcasted_iota(I32, z.shape, 0)
        z1 = jnp.where(row == 0, zp[7:8, :], pltpu.roll(z, 1, 0))
        z2 = jnp.where(row == 0, zp[6:7, :], jnp.where(row == 1, zp[7:8, :], pltpu.roll(z, 2, 0)))
        cw = convw_ref[...]
        conv = cw[0:1, :] * z2 + cw[1:2, :] * z1 + cw[2:3, :] * z
        ycat_scr[:, 0:BRANCH_WIDTH] = (cb_ref[...].astype(F32) * conv).astype(BF16)
        u = _gelu_tanh(su_ref[...].astype(F32))
        v = _rmsnorm_rows(_gelu_tanh(sv_ref[...].astype(F32)), sgun_ref[...]).astype(BF16)
        tri_r = lax.broadcasted_iota(I32, (CHUNK, CHUNK), 0)
        tri_c = lax.broadcasted_iota(I32, (CHUNK, CHUNK), 1)
        bias = sgub_ref[...]
        for g in range(BRANCH_WIDTH // LANES):
            wg = jnp.where(tri_c <= tri_r, sguw_ref[g], 0.0).astype(BF16)
            cols = slice(g * LANES, (g + 1) * LANES)
            for ck in range(tm // CHUNK):
                rows = slice(ck * CHUNK, (ck + 1) * CHUNK)
                y = _dot(wg, v[rows, cols]) + bias[:, g:g + 1]
                ycat_scr[rows, BRANCH_WIDTH + g * LANES:BRANCH_WIDTH + (g + 1) * LANES] = (
                    u[rows, cols] * y).astype(BF16)
        ycat_scr[:, 2 * BRANCH_WIDTH:3 * BRANCH_WIDTH] = yc_ref[...]
        ycat_scr[:, 3 * BRANCH_WIDTH:4 * BRANCH_WIDTH] = yd_ref[...]

    hn = hn_ref[...]
    merged = None
    for n, wg_ref in enumerate((wg0_ref, wg1_ref, wg2_ref, wg3_ref)):
        gate = jax.nn.sigmoid(_dot(hn, wg_ref[...]))
        br = _dot(ycat_scr[:, n * BRANCH_WIDTH:(n + 1) * BRANCH_WIDTH], wb_ref[n])
        merged = gate * br if merged is None else merged + gate * br
    acc_scr[...] += _dot(merged.astype(BF16), wo_ref[...])

    @pl.when(j == n_j - 1)
    def _():
        h_new = res_ref[...] + acc_scr[...]
        h_ref[...] = h_new
        hn2 = _rmsnorm_rows(h_new, fng_ref[...])
        hb = hn2.astype(BF16)
        hn2_ref[...] = hb
        logits = _dot(hb, router_ref[...])
        lane = lax.broadcasted_iota(I32, logits.shape, 1)
        logits = jnp.where(lane < n_experts, logits, -jnp.inf)
        v1 = jnp.max(logits, axis=-1, keepdims=True)
        i1 = jnp.min(jnp.where(logits == v1, lane, LANES), axis=-1, keepdims=True)
        rest = jnp.where(lane == i1, -jnp.inf, logits)
        v2 = jnp.max(rest, axis=-1, keepdims=True)
        i2 = jnp.min(jnp.where(rest == v2, lane, LANES), axis=-1, keepdims=True)
        e2 = jnp.exp(v2 - v1)
        w1 = 1.0 / (1.0 + e2)
        w2 = e2 / (1.0 + e2)
        ridx_ref[...] = jnp.where(lane == 0, i1, jnp.where(lane == 1, i2, 0))
        rw_ref[...] = jnp.where(lane == 0, w1, jnp.where(lane == 1, w2, 0.0))


def _merge_call(proj, y_c, y_d, hn, h2, conv_w, sgu_norm, sgu_w, sgu_b_t, w_gate, w_branch, w_out,
                ffn_g, router_pad, seq, n_experts):
    T, D = h2.shape
    tm = _pick(seq, 512)
    tn = _pick(D, 256)
    n_j = D // tn
    rb8 = tm // 8
    kern = functools.partial(_merge_kernel, tm=tm, rows_per_seq=seq, n_j=n_j, n_experts=n_experts)
    tile = lambda t: pl.BlockSpec((tm, BRANCH_WIDTH), lambda i, j, t=t: (i, t))
    prev = lambda t: pl.BlockSpec((8, BRANCH_WIDTH), lambda i, j, t=t: (jnp.maximum(i * rb8 - 1, 0), t))
    full = lambda shp: pl.BlockSpec(shp, lambda i, j: (0,) * len(shp))
    gate = lambda n: pl.BlockSpec((D, tn), lambda i, j, n=n: (0, n * n_j + j))
    rows = lambda w: pl.BlockSpec((tm, w), lambda i, j: (i, 0))
    return pl.pallas_call(
        kern,
        grid=(T // tm, n_j),
        in_specs=[
            tile(T_CB), tile(T_CC), tile(T_CH), prev(T_CC), prev(T_CH), tile(T_SU), tile(T_SV),
            rows(BRANCH_WIDTH), rows(BRANCH_WIDTH), rows(D), rows(D),
            full((CONV_K, BRANCH_WIDTH)), full((1, BRANCH_WIDTH)),
            full((BRANCH_WIDTH // LANES, CHUNK, CHUNK)), full((CHUNK, BRANCH_WIDTH // LANES)),
            gate(0), gate(1), gate(2), gate(3),
            pl.BlockSpec((N_BRANCHES, BRANCH_WIDTH, tn), lambda i, j: (0, 0, j)),
            pl.BlockSpec((tn, D), lambda i, j: (j, 0)),
            full((1, D)), full((D, LANES)),
        ],
        out_specs=[rows(D), rows(D), rows(LANES), rows(LANES)],
        out_shape=[
            jax.ShapeDtypeStruct((T, D), F32),
            jax.ShapeDtypeStruct((T, D), BF16),
            jax.ShapeDtypeStruct((T, LANES), I32),
            jax.ShapeDtypeStruct((T, LANES), F32),
        ],
        scratch_shapes=[pltpu.VMEM((tm, N_BRANCHES * BRANCH_WIDTH), BF16), pltpu.VMEM((tm, D), F32)],
        compiler_params=_cparams(("parallel", "arbitrary"), MERGE_VMEM_LIMIT),
        name="merge",
    )(proj, proj, proj, proj, proj, proj, proj, y_c, y_d, hn, h2, conv_w, sgu_norm, sgu_w, sgu_b_t,
      w_gate, w_gate, w_gate, w_gate, w_branch, w_out, ffn_g, router_pad)


def _ffn_kernel(x_ref, comb_ref, w1_ref, w3_ref, w2_ref, res_ref, g_ref, o_ref, acc_e, acc_t,
                *, n_e, n_f, final_norm):
    e = pl.program_id(1)
    f = pl.program_id(2)

    @pl.when((e == 0) & (f == 0))
    def _():
        acc_t[...] = jnp.zeros(acc_t.shape, F32)

    @pl.when(f == 0)
    def _():
        acc_e[...] = jnp.zeros(acc_e.shape, F32)

    x = x_ref[...]
    h1 = _dot(x, w1_ref[...])
    h3 = _dot(x, w3_ref[...])
    a = (h1 * jax.nn.sigmoid(h1)) * h3
    acc_e[...] += _dot(a.astype(BF16), w2_ref[...])

    @pl.when(f == n_f - 1)
    def _():
        comb = comb_ref[...]
        lane = lax.broadcasted_iota(I32, comb.shape, 1)
        c = jnp.sum(jnp.where(lane == e, comb, 0.0), axis=-1, keepdims=True)
        acc_t[...] += c * acc_e[...]

    @pl.when((e == n_e - 1) & (f == n_f - 1))
    def _():
        out = res_ref[...] + acc_t[...]
        if final_norm:
            out = _rmsnorm_rows(out, g_ref[...])
        o_ref[...] = out


def _ffn_call(xb, comb, w1, w3, w2, res, g, final_norm):
    T, D = res.shape
    n_e, _, d_ff = w1.shape
    tm = _pick(T, 512)
    tf = _pick(d_ff, 512)
    n_f = d_ff // tf
    kern = functools.partial(_ffn_kernel, n_e=n_e, n_f=n_f, final_norm=final_norm)
    rows = lambda w: pl.BlockSpec((tm, w), lambda i, e, f: (i, 0))
    return pl.pallas_call(
        kern,
        grid=(T // tm, n_e, n_f),
        in_specs=[
            rows(D), rows(LANES),
            pl.BlockSpec((None, D, tf), lambda i, e, f: (e, 0, f)),
            pl.BlockSpec((None, D, tf), lambda i, e, f: (e, 0, f)),
            pl.BlockSpec((None, tf, D), lambda i, e, f: (e, f, 0)),
            rows(D),
            pl.BlockSpec((1, D), lambda i, e, f: (0, 0)),
        ],
        out_specs=rows(D),
        out_shape=jax.ShapeDtypeStruct((T, D), F32),
        scratch_shapes=[pltpu.VMEM((tm, D), F32), pltpu.VMEM((tm, D), F32)],
        compiler_params=_cparams(("parallel", "arbitrary", "arbitrary")),
        name="ffn",
    )(xb, comb, w1, w3, w2, res, g)


def _row_copy(src_hbm, row, dst_vmem, slot, sem):
    return pltpu.make_async_copy(src_hbm.at[pl.ds(row, 1), :], dst_vmem.at[pl.ds(slot, 1), :], sem)


ROW_BLOCK = 256


def _ffn_routed_kernel(te_ref, nu_ref, tok_ref, h_hbm, g_ref, w1_ref, w3_ref, w2_ref, o_ref,
                       buf, x_scr, acc, sem, *, n_f, tm):
    i = pl.program_id(0)
    f = pl.program_id(1)
    live = i < nu_ref[0]

    @pl.when(f == 0)
    def _():
        acc[...] = jnp.zeros(acc.shape, F32)

    @pl.when((f == 0) & live)
    def _():
        def start(r, c):
            _row_copy(h_hbm, tok_ref[0, 0, r], buf, r, sem).start()
            return c

        lax.fori_loop(0, tm, start, 0)

        def wait(r, c):
            _row_copy(h_hbm, tok_ref[0, 0, r], buf, r, sem).wait()
            return c

        lax.fori_loop(0, tm, wait, 0)
        rblk = min(ROW_BLOCK, tm)
        for rb in range(tm // rblk):
            rows = slice(rb * rblk, (rb + 1) * rblk)
            x_scr[rows, :] = _rmsnorm_rows(buf[rows, :], g_ref[...]).astype(BF16)

    @pl.when(live)
    def _():
        x = x_scr[...]
        h1 = _dot(x, w1_ref[...])
        h3 = _dot(x, w3_ref[...])
        a = (h1 * jax.nn.sigmoid(h1)) * h3
        acc[...] += _dot(a.astype(BF16), w2_ref[...])

    @pl.when(f == n_f - 1)
    def _():
        o_ref[...] = acc[...]


def _ffn_routed_call(h2, g, row_token, tile_e, n_used, w1, w3, w2, tm):
    T, D = h2.shape
    R = row_token.shape[0]
    d_ff = w1.shape[-1]
    tf = _pick(d_ff, 512)
    n_f = d_ff // tf
    assert tm % min(ROW_BLOCK, tm) == 0
    kern = functools.partial(_ffn_routed_kernel, n_f=n_f, tm=tm)

    def wmap(i, f, te, nu):
        return te[i], jnp.where(i < nu[0], f, n_f - 1)

    return pl.pallas_call(
        kern,
        grid_spec=pltpu.PrefetchScalarGridSpec(
            num_scalar_prefetch=2,
            grid=(R // tm, n_f),
            in_specs=[
                pl.BlockSpec((1, 1, tm), lambda i, f, te, nu: (i, 0, 0), memory_space=pltpu.SMEM),
                pl.BlockSpec(memory_space=pl.ANY),
                pl.BlockSpec((1, D), lambda i, f, te, nu: (0, 0)),
                pl.BlockSpec((None, D, tf), lambda i, f, te, nu: (wmap(i, f, te, nu)[0], 0, wmap(i, f, te, nu)[1])),
                pl.BlockSpec((None, D, tf), lambda i, f, te, nu: (wmap(i, f, te, nu)[0], 0, wmap(i, f, te, nu)[1])),
                pl.BlockSpec((None, tf, D), lambda i, f, te, nu: (wmap(i, f, te, nu)[0], wmap(i, f, te, nu)[1], 0)),
            ],
            out_specs=pl.BlockSpec((tm, D), lambda i, f, te, nu: (i, 0)),
            scratch_shapes=[pltpu.VMEM((tm, D), F32), pltpu.VMEM((tm, D), BF16), pltpu.VMEM((tm, D), F32),
                            pltpu.SemaphoreType.DMA(())],
        ),
        out_shape=jax.ShapeDtypeStruct((R, D), F32),
        compiler_params=_cparams(("arbitrary", "arbitrary")),
        name="ffn_routed",
    )(tile_e, n_used, row_token.reshape(R // tm, 1, tm), h2, g, w1, w3, w2)


def _moe_combine_kernel(pos_ref, h_ref, rw_ref, y_hbm, g_ref, o_ref, buf0, buf1, sem, *, tc, final_norm):
    def start(r, c):
        _row_copy(y_hbm, pos_ref[0, 0, 2 * r], buf0, r, sem).start()
        _row_copy(y_hbm, pos_ref[0, 0, 2 * r + 1], buf1, r, sem).start()
        return c

    lax.fori_loop(0, tc, start, 0)

    def wait(r, c):
        _row_copy(y_hbm, pos_ref[0, 0, 2 * r], buf0, r, sem).wait()
        _row_copy(y_hbm, pos_ref[0, 0, 2 * r + 1], buf1, r, sem).wait()
        return c

    lax.fori_loop(0, tc, wait, 0)
    rw = rw_ref[...]
    out = h_ref[...] + (rw[:, 0:1] * buf0[...] + rw[:, 1:2] * buf1[...])
    if final_norm:
        out = _rmsnorm_rows(out, g_ref[...])
    o_ref[...] = out


def _moe_combine_call(h2, rw, ys, pos, g, final_norm):
    T, D = h2.shape
    tc = _pick(T, 256)
    kern = functools.partial(_moe_combine_kernel, tc=tc, final_norm=final_norm)
    return pl.pallas_call(
        kern,
        grid=(T // tc,),
        in_specs=[
            pl.BlockSpec((1, 1, 2 * tc), lambda i: (i, 0, 0), memory_space=pltpu.SMEM),
            pl.BlockSpec((tc, D), lambda i: (i, 0)),
            pl.BlockSpec((tc, LANES), lambda i: (i, 0)),
            pl.BlockSpec(memory_space=pl.ANY),
            pl.BlockSpec((1, D), lambda i: (0, 0)),
        ],
        out_specs=pl.BlockSpec((tc, D), lambda i: (i, 0)),
        out_shape=jax.ShapeDtypeStruct((T, D), F32),
        scratch_shapes=[pltpu.VMEM((tc, D), F32), pltpu.VMEM((tc, D), F32), pltpu.SemaphoreType.DMA(())],
        compiler_params=_cparams(("arbitrary",)),
        name="moe_combine",
    )(pos.reshape(T // tc, 1, 2 * tc), h2, rw, ys, g)


def _route(ridx, n_experts, tm):
    T = ridx.shape[0]
    flat_e = ridx[:, :2].reshape(-1)
    onehot = (flat_e[:, None] == jnp.arange(n_experts, dtype=I32)[None, :]).astype(I32)
    cum = jnp.cumsum(onehot, axis=0)
    rank = jnp.take_along_axis(cum, flat_e[:, None], axis=1)[:, 0] - 1
    counts = cum[-1]
    padded = ((counts + tm - 1) // tm) * tm
    gend = jnp.cumsum(padded)
    gstart = gend - padded
    pos = gstart[flat_e] + rank
    R = 2 * T + n_experts * tm
    row_token = jnp.zeros((R,), I32).at[pos].set(jnp.arange(2 * T, dtype=I32) // 2)
    tile_e = jnp.searchsorted(gend, jnp.arange(R // tm, dtype=I32) * tm, side="right")
    tile_e = jnp.minimum(tile_e, n_experts - 1).astype(I32)
    n_used = (gend[-1:] // tm).astype(I32)
    return pos.astype(I32), row_token, tile_e, n_used


def _rope_tables(seq):
    pos = jnp.arange(seq, dtype=F32)

    def tab(dim):
        inv = ROPE_THETA ** (-jnp.arange(0, dim, 2, dtype=F32) / dim)
        ang = pos[:, None] * inv[None, :]
        cos = jnp.concatenate([jnp.cos(ang), jnp.cos(ang)], axis=-1)
        sin = jnp.concatenate([-jnp.sin(ang), jnp.sin(ang)], axis=-1)
        return cos, sin

    c64, s64 = tab(IDX_DIM)
    c128, s128 = tab(HEAD_DIM)
    return (jnp.tile(c64, (1, 2)), jnp.tile(s64, (1, 2)), c128, s128)


def kernel(x, attn_norm, w_in, conv_w, sgu_norm, sgu_w, sgu_b, diff_lambda, diff_subln, w_branch, w_out,
           ffn_norm, dense_w1, dense_w3, dense_w2, router, moe_w1, moe_w3, moe_w2, final_norm):
    batch, seq, D = x.shape
    depth = w_in.shape[0]
    n_experts = router.shape[-1]
    T = batch * seq
    rope = _rope_tables(seq)
    h = x.reshape(T, D)
    ik0 = MAIN_WIDTH
    iw0 = ik0 + IDX_DIM
    g0 = iw0 + IDX_HEADS
    ones_comb = jnp.zeros((T, LANES), F32).at[:, 0].set(1.0)

    for layer in range(depth):
        wl = w_in[layer]
        w_main = wl[:, :MAIN_WIDTH].astype(BF16)
        w_small = jnp.concatenate(
            [wl[:, ik0:iw0], wl[:, ik0:iw0], wl[:, iw0:g0], jnp.zeros((D, LANES - IDX_HEADS), F32)],
            axis=1).astype(BF16)
        w_gate = wl[:, g0:].astype(BF16)
        lam_init = 0.8 - 0.6 * math.exp(-0.3 * layer)

        proj, hn, ikd, iw = _proj_call(h, attn_norm[layer][None, :], w_main, w_small, rope, seq)
        y_c = _diff_call(proj, diff_lambda[layer], diff_subln[layer][:, None], lam_init, batch, seq)
        mask = _dsa_select_call(proj, ikd, iw[:, :IDX_HEADS].T, batch, seq)
        y_d = _dsa_attn_call(proj, mask, batch, seq)

        j = layer // 2
        is_moe = layer % 2 == 1
        router_pad = jnp.zeros((D, LANES), F32)
        if is_moe:
            router_pad = router_pad.at[:, :n_experts].set(router[j])
        h, hn2, ridx, rw = _merge_call(
            proj, y_c, y_d, hn, h, conv_w[layer], sgu_norm[layer][None, :], sgu_w[layer],
            sgu_b[layer].T, w_gate, w_branch[layer].astype(BF16), w_out[layer].astype(BF16),
            ffn_norm[layer][None, :], router_pad.astype(BF16), seq, n_experts)

        last = layer == depth - 1
        if is_moe:
            tm = _pick(T, 1024)
            pos, row_token, tile_e, n_used = _route(ridx, n_experts, tm)
            ys = _ffn_routed_call(h, ffn_norm[layer][None, :], row_token, tile_e, n_used,
                                  moe_w1[j].astype(BF16), moe_w3[j].astype(BF16), moe_w2[j].astype(BF16), tm)
            h = _moe_combine_call(h, rw, ys, pos, final_norm[None, :], last)
        else:
            h = _ffn_call(hn2, ones_comb, dense_w1[j][None].astype(BF16), dense_w3[j][None].astype(BF16),
                          dense_w2[j][None].astype(BF16), h, final_norm[None, :], last)

    if depth == 0:
        raise ValueError("depth must be positive")
    return h.reshape(batch, seq, D)
```

```python
import functools
import math

import jax
import jax.numpy as jnp
from jax import lax
from jax.experimental import pallas as pl
from jax.experimental.pallas import tpu as pltpu

F32 = jnp.float32
BF16 = jnp.bfloat16
I32 = jnp.int32

LANES = 128
SLAB = 8
N_COUNTERS = 4
PACK = 16
INF_KEY16 = 0x7F80
MIN_NORMAL_KEY16 = 0x0080
MIN_NORMAL_F32 = 1.1754943508222875e-38
HEAD_DIM = 128
BRANCH_WIDTH = 512
N_BRANCHES = 4
N_HEADS = 4
IDX_HEADS = 8
IDX_DIM = 64
DIFF_SUB_DIM = 64
CONV_K = 3
CHUNK = 128
TOPK_MAX = 256
ROPE_THETA = 10000.0
EPS = 1e-6
N_MAIN_TILES = 12
MAIN_WIDTH = N_MAIN_TILES * BRANCH_WIDTH
(T_CB, T_CC, T_CH, T_SU, T_SV, T_DQ, T_DK, T_DV, T_AQ, T_AK, T_AV, T_IQ) = range(12)
NEG_BIG = -1e30
INT_MIN = -(2 ** 31)
VMEM_LIMIT = 56 * 1024 * 1024
MERGE_VMEM_LIMIT = 61 * 1024 * 1024


def _pick(n, pref):
    t = min(pref, n)
    while n % t:
        t //= 2
    return t


def _cparams(sem, vmem_limit=VMEM_LIMIT):
    return pltpu.CompilerParams(dimension_semantics=sem, vmem_limit_bytes=vmem_limit)


def _dot(a, b):
    return jnp.dot(a, b, preferred_element_type=F32)


def _dot_nt(a, b):
    return lax.dot_general(a, b, (((1,), (1,)), ((), ())), preferred_element_type=F32)


def _dot_tn(a, b):
    return lax.dot_general(a, b, (((0,), (0,)), ((), ())), preferred_element_type=F32)


def _rmsnorm_rows(x, g):
    ms = jnp.mean(x * x, axis=-1, keepdims=True)
    return (x * lax.rsqrt(ms + EPS)) * g


def _rope64(x, c, s):
    lane = lax.broadcasted_iota(I32, x.shape, 1)
    rot = jnp.where((lane & 32) == 0, pltpu.roll(x, 96, 1), pltpu.roll(x, 32, 1))
    return x * c + rot * s


def _rope128(x, c, s):
    return x * c + pltpu.roll(x, 64, 1) * s


def _gelu_tanh(x):
    c = math.sqrt(2.0 / math.pi)
    return x * (0.5 * (1.0 + jnp.tanh(c * (x + 0.044715 * (x * x * x)))))


def _proj_kernel(x_ref, g_ref, w_ref, ws_ref, c64_ref, s64_ref, c128_ref, s128_ref,
                 proj_ref, hn_ref, ikd_ref, iw_ref, hn_scr):
    j = pl.program_id(1)

    @pl.when(j == 0)
    def _():
        hb = _rmsnorm_rows(x_ref[...], g_ref[...]).astype(BF16)
        hn_scr[...] = hb
        hn_ref[...] = hb
        small = _dot(hb, ws_ref[...])
        ikd_ref[...] = _rope64(small[:, :LANES], c64_ref[...], s64_ref[...]).astype(BF16)
        iw_ref[...] = small[:, LANES:]

    acc = _dot(hn_scr[...], w_ref[...])
    is64 = (j == T_DQ) | (j == T_DK) | (j == T_IQ)
    is128 = (j == T_AQ) | (j == T_AK)

    @pl.when(jnp.logical_not(is64 | is128))
    def _():
        proj_ref[...] = acc.astype(BF16)

    @pl.when(is64)
    def _():
        scale = jnp.where(j == T_DQ, DIFF_SUB_DIM ** -0.5, 1.0).astype(F32)
        c, s = c64_ref[...], s64_ref[...]
        for cb in range(BRANCH_WIDTH // LANES):
            sl = slice(cb * LANES, (cb + 1) * LANES)
            proj_ref[:, sl] = (_rope64(acc[:, sl], c, s) * scale).astype(BF16)

    @pl.when(is128)
    def _():
        c, s = c128_ref[...], s128_ref[...]
        for cb in range(BRANCH_WIDTH // LANES):
            sl = slice(cb * LANES, (cb + 1) * LANES)
            proj_ref[:, sl] = _rope128(acc[:, sl], c, s).astype(BF16)


def _proj_call(h2, g, w_main, w_small, rope, seq):
    T, D = h2.shape
    tm = _pick(seq, 1024)
    nrb = seq // tm
    rope_spec = pl.BlockSpec((tm, LANES), lambda i, j: (i % nrb, 0))
    row_spec = lambda w: pl.BlockSpec((tm, w), lambda i, j: (i, 0))
    return pl.pallas_call(
        _proj_kernel,
        grid=(T // tm, N_MAIN_TILES),
        in_specs=[
            row_spec(D),
            pl.BlockSpec((1, D), lambda i, j: (0, 0)),
            pl.BlockSpec((D, BRANCH_WIDTH), lambda i, j: (0, j)),
            pl.BlockSpec((D, 2 * LANES), lambda i, j: (0, 0)),
            rope_spec, rope_spec, rope_spec, rope_spec,
        ],
        out_specs=[
            pl.BlockSpec((tm, BRANCH_WIDTH), lambda i, j: (i, j)),
            row_spec(D), row_spec(LANES), row_spec(LANES),
        ],
        out_shape=[
            jax.ShapeDtypeStruct((T, MAIN_WIDTH), BF16),
            jax.ShapeDtypeStruct((T, D), BF16),
            jax.ShapeDtypeStruct((T, LANES), BF16),
            jax.ShapeDtypeStruct((T, LANES), F32),
        ],
        scratch_shapes=[pltpu.VMEM((tm, D), BF16)],
        compiler_params=_cparams(("parallel", "arbitrary")),
        name="proj",
    )(h2, g, w_main, w_small, *rope)


def _softmax_step_t(parts, m_scr, l_scr, acc_scr):
    m_prev = m_scr[...]
    m_new = m_prev
    for s_t, _ in parts:
        m_new = jnp.maximum(m_new, jnp.max(s_t, axis=0, keepdims=True))
    alpha = jnp.exp(m_prev - m_new)
    l_new = alpha * l_scr[...]
    acc = alpha * acc_scr[...]
    for s_t, v in parts:
        p_t = jnp.exp(s_t - m_new)
        l_new = l_new + jnp.sum(p_t, axis=0, keepdims=True)
        acc = acc + _dot_tn(v, p_t.astype(BF16))
    l_scr[...] = l_new
    acc_scr[...] = acc
    m_scr[...] = m_new


def _init_softmax_state(m_scr, l_scr, acc_scr):
    m_scr[...] = jnp.full(m_scr.shape, NEG_BIG, F32)
    l_scr[...] = jnp.zeros(l_scr.shape, F32)
    acc_scr[...] = jnp.zeros(acc_scr.shape, F32)


def _pipelined_tiles(n, qk, sm):
    qk(0, 0)

    def pair(p, carry):
        qk(2 * p + 1, 1)
        sm(2 * p, 0, False)
        qk(2 * p + 2, 0)
        sm(2 * p + 1, 1, False)
        return carry

    n_pairs = (n - 1) // 2
    lax.fori_loop(0, n_pairs, pair, 0)
    rest = n - 2 * n_pairs

    @pl.when(rest == 1)
    def _():
        sm(n - 1, 0, True)

    @pl.when(rest == 2)
    def _():
        qk(n - 1, 1)
        sm(n - 2, 0, False)
        sm(n - 1, 1, True)


def _diff_kernel(q_ref, k_ref, v_ref, dl_ref, subln_ref, o_ref,
                 qq_scr, s_scr, m_scr, l_scr, acc_scr, *, tq, lam_init):
    qi = pl.program_id(2)
    q = q_ref[...]
    lane = lax.broadcasted_iota(I32, q.shape, 1)
    zero = jnp.zeros_like(q)
    qq_scr[0:tq, :] = jnp.where(lane < DIFF_SUB_DIM, q, zero)
    qq_scr[tq:2 * tq, :] = jnp.where(lane >= DIFF_SUB_DIM, q, zero)
    _init_softmax_state(m_scr, l_scr, acc_scr)

    def qk(kj, slot):
        off = pl.multiple_of(kj * tq, tq)
        s_scr[slot] = _dot_nt(k_ref[pl.ds(off, tq), :], qq_scr[...])

    def sm(kj, slot, causal):
        off = pl.multiple_of(kj * tq, tq)
        s_t = s_scr[slot]
        if causal:
            kpos = lax.broadcasted_iota(I32, s_t.shape, 0)
            qpos = lax.broadcasted_iota(I32, s_t.shape, 1)
            qpos = jnp.where(qpos >= tq, qpos - tq, qpos)
            s_t = jnp.where(kpos <= qpos, s_t, NEG_BIG)
        _softmax_step_t([(s_t, v_ref[pl.ds(off, tq), :])], m_scr, l_scr, acc_scr)

    _pipelined_tiles(qi + 1, qk, sm)

    dl = dl_ref[...]
    lam = (jnp.exp(jnp.sum(dl[0:1] * dl[1:2], axis=-1, keepdims=True))
           - jnp.exp(jnp.sum(dl[2:3] * dl[3:4], axis=-1, keepdims=True)) + lam_init)
    o_all = acc_scr[...] / l_scr[...]
    o_t = o_all[:, 0:tq] - lam * o_all[:, tq:2 * tq]
    ms = jnp.mean(o_t * o_t, axis=0, keepdims=True)
    o_t = (o_t * lax.rsqrt(ms + EPS)) * subln_ref[...] * (1.0 - lam_init)
    o_ref[...] = o_t.T.astype(BF16)


def _diff_call(proj, diff_lambda, subln_col, lam_init, batch, seq):
    T = proj.shape[0]
    tq = _pick(seq, 512)
    nq = seq // tq
    hpt = BRANCH_WIDTH // LANES
    kern = functools.partial(_diff_kernel, tq=tq, lam_init=lam_init)
    return pl.pallas_call(
        kern,
        grid=(batch, N_HEADS, nq),
        in_specs=[
            pl.BlockSpec((tq, LANES), lambda b, h, i: (b * nq + i, T_DQ * hpt + h)),
            pl.BlockSpec((seq, LANES), lambda b, h, i: (b, T_DK * hpt + h)),
            pl.BlockSpec((seq, LANES), lambda b, h, i: (b, T_DV * hpt + h)),
            pl.BlockSpec((4, DIFF_SUB_DIM), lambda b, h, i: (0, 0)),
            pl.BlockSpec((HEAD_DIM, 1), lambda b, h, i: (0, 0)),
        ],
        out_specs=pl.BlockSpec((tq, LANES), lambda b, h, i: (b * nq + i, h)),
        out_shape=jax.ShapeDtypeStruct((T, BRANCH_WIDTH), BF16),
        scratch_shapes=[
            pltpu.VMEM((2 * tq, LANES), BF16),
            pltpu.VMEM((2, tq, 2 * tq), F32),
            pltpu.VMEM((1, 2 * tq), F32),
            pltpu.VMEM((1, 2 * tq), F32),
            pltpu.VMEM((HEAD_DIM, 2 * tq), F32),
        ],
        compiler_params=_cparams(("parallel", "parallel", "arbitrary")),
        name="diff_attn",
    )(proj, proj, proj, diff_lambda, subln_col)


def _dsa_select_kernel(iq_ref, ikd_ref, iwt_ref, mask_ref, key_scr, hi_scr, *, tq, tc, nc, topk, idx_bits):
    qi = pl.program_id(1)
    n_ch = (qi * tq + tq + tc - 1) // tc
    iq = iq_ref[...]
    iwt = iwt_ref[...]
    lane = lax.broadcasted_iota(I32, (tq, LANES), 1)
    zero = jnp.zeros((tq, LANES), BF16)
    qm = []
    for h in range(IDX_HEADS):
        blk = iq[:, (h // 2) * LANES:(h // 2 + 1) * LANES]
        keep = (lane < IDX_DIM) if h % 2 == 0 else (lane >= IDX_DIM)
        qm.append(jnp.where(keep, blk, zero))
    idx_scale = IDX_DIM ** -0.5 * IDX_HEADS ** -0.5
    qpos = qi * tq + lax.broadcasted_iota(I32, (tc, tq), 1)
    krow = lax.broadcasted_iota(I32, (tc, tq), 0)
    krow8 = lax.broadcasted_iota(I32, (SLAB, tq), 0)

    def score_chunk(c, carry):
        ik = ikd_ref[pl.ds(pl.multiple_of(c * tc, tc), tc), :]
        sc = jnp.zeros((tc, tq), F32)
        for h in range(IDX_HEADS):
            sc = sc + iwt[h:h + 1, :] * jnp.maximum(_dot_nt(ik, qm[h]), 0.0)
        sc = sc * idx_scale
        sc = jnp.where(jnp.abs(sc) < MIN_NORMAL_F32, 0.0, sc)
        sc = jnp.where(c * tc + krow <= qpos, sc, -jnp.inf)
        bits = pltpu.bitcast(sc, I32)
        key_scr[c] = jnp.where(bits < 0, INT_MIN - bits, bits)
        hi_scr[c] = sc.astype(BF16)
        return carry

    lax.fori_loop(0, n_ch, score_chunk, 0)

    def count(pred):
        def body(c, accs):
            accs = list(accs)
            for r in range(tc // SLAB):
                ks = key_scr[c, r * SLAB:(r + 1) * SLAB, :]
                accs[r % N_COUNTERS] = accs[r % N_COUNTERS] + jnp.where(pred(ks, c * tc + r * SLAB), 1, 0)
            return tuple(accs)
        zero = jnp.zeros((SLAB, tq), I32)
        accs = lax.fori_loop(0, n_ch, body, (zero,) * N_COUNTERS)
        return jnp.sum(functools.reduce(lambda a, b: a + b, accs), axis=0, keepdims=True)

    def rows8(x):
        return jnp.broadcast_to(x, (SLAB, tq))

    def count_ge(cand):
        c8 = rows8(cand)
        return count(lambda ks, base: ks >= c8)

    def count16(cand_val):
        c16 = jnp.broadcast_to(cand_val, (PACK, tq)).astype(BF16)
        one = jnp.ones((PACK, tq), BF16)
        zero16 = jnp.zeros((PACK, tq), BF16)

        def body(c, accs):
            accs = list(accs)
            for r in range(tc // PACK):
                hs = hi_scr[c, r * PACK:(r + 1) * PACK, :]
                accs[r % N_COUNTERS] = accs[r % N_COUNTERS] + jnp.where(hs >= c16, one, zero16)
            return tuple(accs)
        accs = lax.fori_loop(0, n_ch, body, (zero16,) * N_COUNTERS)
        tot = functools.reduce(lambda a, b: a + b, [a.astype(F32) for a in accs])
        return jnp.sum(tot, axis=0, keepdims=True)

    def key16_value(k16):
        k16 = jnp.clip(k16, -INF_KEY16, INF_KEY16)
        k16 = jnp.where((k16 > 0) & (k16 < MIN_NORMAL_KEY16), MIN_NORMAL_KEY16, k16)
        bits16 = jnp.where(k16 >= 0, k16, -32768 - k16)
        return pltpu.bitcast(jnp.left_shift(bits16, 16), F32)

    nonneg = count16(jnp.zeros((1, tq), F32)) >= topk
    thr16 = jnp.where(nonneg, 0, -32768).astype(I32)

    def bit16_body(it, thr16):
        cand = thr16 + jnp.left_shift(jnp.int32(1), 14 - it)
        return jnp.where(count16(key16_value(cand)) >= topk, cand, thr16)

    thr16 = lax.fori_loop(0, 15, bit16_body, thr16)

    thr = jnp.left_shift(thr16, 16) - 2 ** 15

    def bit_body(it, thr):
        cand = thr + jnp.left_shift(jnp.int32(1), 16 - it)
        return jnp.where(count_ge(cand) >= topk, cand, thr)

    thr = lax.fori_loop(0, 17, bit_body, thr)
    thr8 = rows8(thr)
    cnt_ge = count_ge(thr)
    excess = jnp.max(cnt_ge - topk) > 0

    def tie_search(_):
        need = topk - count(lambda ks, base: ks > thr8)

        def tbit(it, p):
            cand = p + jnp.left_shift(jnp.int32(1), idx_bits - 1 - it)
            cand8 = rows8(cand)
            cnt = count(lambda ks, base: (ks == thr8) & (base + krow8 < cand8))
            return jnp.where(cnt < need, cand, p)
        return lax.fori_loop(0, idx_bits, tbit, jnp.zeros((1, tq), I32))

    last = lax.cond(excess, tie_search, lambda _: jnp.full((1, tq), 2 ** idx_bits, I32), 0)

    def emit(c, carry):
        kc = key_scr[c]
        kpos = c * tc + krow
        sel = (kc > thr) | ((kc == thr) & (kpos <= last))
        sel = sel & (kpos <= qpos)
        mask_ref[0, 0, c] = jnp.where(sel, 1, 0).astype(jnp.int8)
        return carry

    lax.fori_loop(0, n_ch, emit, 0)

    def clear(c, carry):
        mask_ref[0, 0, c] = jnp.zeros((tc, tq), jnp.int8)
        return carry

    lax.fori_loop(n_ch, nc, clear, 0)


def _dsa_tiles(seq):
    tq = _pick(seq, 256)
    tc = _pick(seq, 512)
    return tq, tc


def _dsa_select_call(proj, ikd, iw_t, batch, seq):
    tq, tc = _dsa_tiles(seq)
    nq, nc = seq // tq, seq // tc
    topk = min(TOPK_MAX, seq // 4)
    assert tc >= topk
    assert (tc // PACK // N_COUNTERS) * nc <= 256
    idx_bits = max(1, (seq - 1).bit_length())
    kern = functools.partial(_dsa_select_kernel, tq=tq, tc=tc, nc=nc, topk=topk, idx_bits=idx_bits)
    return pl.pallas_call(
        kern,
        grid=(batch, nq),
        in_specs=[
            pl.BlockSpec((tq, BRANCH_WIDTH), lambda b, i: (b * nq + i, T_IQ)),
            pl.BlockSpec((seq, LANES), lambda b, i: (b, 0)),
            pl.BlockSpec((IDX_HEADS, tq), lambda b, i: (0, b * nq + i)),
        ],
        out_specs=pl.BlockSpec((1, 1, nc, tc, tq), lambda b, i: (b, i, 0, 0, 0)),
        out_shape=jax.ShapeDtypeStruct((batch, nq, nc, tc, tq), jnp.int8),
        scratch_shapes=[pltpu.VMEM((nc, tc, tq), I32), pltpu.VMEM((nc, tc, tq), BF16)],
        compiler_params=_cparams(("parallel", "arbitrary")),
        name="dsa_select",
    )(proj, ikd, iw_t)


def _dsa_attn_kernel(q_ref, k_ref, v_ref, mask_ref, o_ref, s_scr, m_scr, l_scr, acc_scr, *, tq, tc, cpi):
    qi = pl.program_id(2)
    n_ch = (qi * tq + tq + tc - 1) // tc
    q = q_ref[...]
    scale = HEAD_DIM ** -0.5
    _init_softmax_state(m_scr, l_scr, acc_scr)

    def qk(it, slot):
        for u in range(cpi):
            c = it * cpi + u
            off = pl.multiple_of(c * tc, tc)
            s_t = _dot_nt(k_ref[pl.ds(off, tc), :], q) * scale
            sel = mask_ref[0, 0, c].astype(I32) != 0
            s_scr[slot, u] = jnp.where(sel, s_t, NEG_BIG)

    def sm(it, slot, last):
        parts = []
        for u in range(cpi):
            off = pl.multiple_of((it * cpi + u) * tc, tc)
            parts.append((s_scr[slot, u], v_ref[pl.ds(off, tc), :]))
        _softmax_step_t(parts, m_scr, l_scr, acc_scr)

    _pipelined_tiles((n_ch + cpi - 1) // cpi, qk, sm)
    o_ref[...] = (acc_scr[...] / l_scr[...]).T.astype(BF16)


def _dsa_attn_call(proj, mask, batch, seq):
    T = proj.shape[0]
    tq, tc = _dsa_tiles(seq)
    nq, nc = seq // tq, seq // tc
    hpt = BRANCH_WIDTH // LANES
    cpi = 2 if nc % 2 == 0 else 1
    kern = functools.partial(_dsa_attn_kernel, tq=tq, tc=tc, cpi=cpi)
    return pl.pallas_call(
        kern,
        grid=(batch, N_HEADS, nq),
        in_specs=[
            pl.BlockSpec((tq, LANES), lambda b, h, i: (b * nq + i, T_AQ * hpt + h)),
            pl.BlockSpec((seq, LANES), lambda b, h, i: (b, T_AK * hpt + h)),
            pl.BlockSpec((seq, LANES), lambda b, h, i: (b, T_AV * hpt + h)),
            pl.BlockSpec((1, 1, nc, tc, tq), lambda b, h, i: (b, i, 0, 0, 0)),
        ],
        out_specs=pl.BlockSpec((tq, LANES), lambda b, h, i: (b * nq + i, h)),
        out_shape=jax.ShapeDtypeStruct((T, BRANCH_WIDTH), BF16),
        scratch_shapes=[
            pltpu.VMEM((2, cpi, tc, tq), F32),
            pltpu.VMEM((1, tq), F32),
            pltpu.VMEM((1, tq), F32),
            pltpu.VMEM((HEAD_DIM, tq), F32),
        ],
        compiler_params=_cparams(("parallel", "parallel", "arbitrary")),
        name="dsa_attn",
    )(proj, proj, proj, mask)


def _merge_kernel(cb_ref, cc_ref, ch_ref, ccp_ref, chp_ref, su_ref, sv_ref, yc_ref, yd_ref,
                  hn_ref, res_ref, convw_ref, sgun_ref, sguw_ref, sgub_ref,
                  wg0_ref, wg1_ref, wg2_ref, wg3_ref, wb_ref, wo_ref, fng_ref, router_ref,
                  h_ref, hn2_ref, ridx_ref, rw_ref, ycat_scr, acc_scr, *, tm, rows_per_seq, n_j, n_experts):
    i = pl.program_id(0)
    j = pl.program_id(1)

    @pl.when(j == 0)
    def _():
        acc_scr[...] = jnp.zeros(acc_scr.shape, F32)
        z = cc_ref[...].astype(F32) * ch_ref[...].astype(F32)
        first = ((i * tm) % rows_per_seq) == 0
        zp = ccp_ref[...].astype(F32) * chp_ref[...].astype(F32)
        zp = zp * jnp.where(first, 0.0, 1.0).astype(F32)
        row = lax.broadcasted_iota(I32, z.shape, 0)
        z1 = jnp.where(row == 0, zp[7:8, :], pltpu.roll(z, 1, 0))
        z2 = jnp.where(row == 0, zp[6:7, :], jnp.where(row == 1, zp[7:8, :], pltpu.roll(z, 2, 0)))
        cw = convw_ref[...]
        conv = cw[0:1, :] * z2 + cw[1:2, :] * z1 + cw[2:3, :] * z
        ycat_scr[:, 0:BRANCH_WIDTH] = (cb_ref[...].astype(F32) * conv).astype(BF16)
        u = _gelu_tanh(su_ref[...].astype(F32))
        v = _rmsnorm_rows(_gelu_tanh(sv_ref[...].astype(F32)), sgun_ref[...]).astype(BF16)
        tri_r = lax.broadcasted_iota(I32, (CHUNK, CHUNK), 0)
        tri_c = lax.broadcasted_iota(I32, (CHUNK, CHUNK), 1)
        bias = sgub_ref[...]
        for g in range(BRANCH_WIDTH // LANES):
            wg = jnp.where(tri_c <= tri_r, sguw_ref[g], 0.0).astype(BF16)
            cols = slice(g * LANES, (g + 1) * LANES)
            for ck in range(tm // CHUNK):
                rows = slice(ck * CHUNK, (ck + 1) * CHUNK)
                y = _dot(wg, v[rows, cols]) + bias[:, g:g + 1]
                ycat_scr[rows, BRANCH_WIDTH + g * LANES:BRANCH_WIDTH + (g + 1) * LANES] = (
                    u[rows, cols] * y).astype(BF16)
        ycat_scr[:, 2 * BRANCH_WIDTH:3 * BRANCH_WIDTH] = yc_ref[...]
        ycat_scr[:, 3 * BRANCH_WIDTH:4 * BRANCH_WIDTH] = yd_ref[...]

    hn = hn_ref[...]
    merged = None
    for n, wg_ref in enumerate((wg0_ref, wg1_ref, wg2_ref, wg3_ref)):
        gate = jax.nn.sigmoid(_dot(hn, wg_ref[...]))
        br = _dot(ycat_scr[:, n * BRANCH_WIDTH:(n + 1) * BRANCH_WIDTH], wb_ref[n])
        merged = gate * br if merged is None else merged + gate * br
    acc_scr[...] += _dot(merged.astype(BF16), wo_ref[...])

    @pl.when(j == n_j - 1)
    def _():
        h_new = res_ref[...] + acc_scr[...]
        h_ref[...] = h_new
        hn2 = _rmsnorm_rows(h_new, fng_ref[...])
        hb = hn2.astype(BF16)
        hn2_ref[...] = hb
        logits = _dot(hb, router_ref[...])
        lane = lax.broadcasted_iota(I32, logits.shape, 1)
        logits = jnp.where(lane < n_experts, logits, -jnp.inf)
        v1 = jnp.max(logits, axis=-1, keepdims=True)
        i1 = jnp.min(jnp.where(logits == v1, lane, LANES), axis=-1, keepdims=True)
        rest = jnp.where(lane == i1, -jnp.inf, logits)
        v2 = jnp.max(rest, axis=-1, keepdims=True)
        i2 = jnp.min(jnp.where(rest == v2, lane, LANES), axis=-1, keepdims=True)
        e2 = jnp.exp(v2 - v1)
        w1 = 1.0 / (1.0 + e2)
        w2 = e2 / (1.0 + e2)
        ridx_ref[...] = jnp.where(lane == 0, i1, jnp.where(lane == 1, i2, 0))
        rw_ref[...] = jnp.where(lane == 0, w1, jnp.where(lane == 1, w2, 0.0))


def _merge_call(proj, y_c, y_d, hn, h2, conv_w, sgu_norm, sgu_w, sgu_b_t, w_gate, w_branch, w_out,
                ffn_g, router_pad, seq, n_experts):
    T, D = h2.shape
    tm = _pick(seq, 512)
    tn = _pick(D, 256)
    n_j = D // tn
    rb8 = tm // 8
    kern = functools.partial(_merge_kernel, tm=tm, rows_per_seq=seq, n_j=n_j, n_experts=n_experts)
    tile = lambda t: pl.BlockSpec((tm, BRANCH_WIDTH), lambda i, j, t=t: (i, t))
    prev = lambda t: pl.BlockSpec((8, BRANCH_WIDTH), lambda i, j, t=t: (jnp.maximum(i * rb8 - 1, 0), t))
    full = lambda shp: pl.BlockSpec(shp, lambda i, j: (0,) * len(shp))
    gate = lambda n: pl.BlockSpec((D, tn), lambda i, j, n=n: (0, n * n_j + j))
    rows = lambda w: pl.BlockSpec((tm, w), lambda i, j: (i, 0))
    return pl.pallas_call(
        kern,
        grid=(T // tm, n_j),
        in_specs=[
            tile(T_CB), tile(T_CC), tile(T_CH), prev(T_CC), prev(T_CH), tile(T_SU), tile(T_SV),
            rows(BRANCH_WIDTH), rows(BRANCH_WIDTH), rows(D), rows(D),
            full((CONV_K, BRANCH_WIDTH)), full((1, BRANCH_WIDTH)),
            full((BRANCH_WIDTH // LANES, CHUNK, CHUNK)), full((CHUNK, BRANCH_WIDTH // LANES)),
            gate(0), gate(1), gate(2), gate(3),
            pl.BlockSpec((N_BRANCHES, BRANCH_WIDTH, tn), lambda i, j: (0, 0, j)),
            pl.BlockSpec((tn, D), lambda i, j: (j, 0)),
            full((1, D)), full((D, LANES)),
        ],
        out_specs=[rows(D), rows(D), rows(LANES), rows(LANES)],
        out_shape=[
            jax.ShapeDtypeStruct((T, D), F32),
            jax.ShapeDtypeStruct((T, D), BF16),
            jax.ShapeDtypeStruct((T, LANES), I32),
            jax.ShapeDtypeStruct((T, LANES), F32),
        ],
        scratch_shapes=[pltpu.VMEM((tm, N_BRANCHES * BRANCH_WIDTH), BF16), pltpu.VMEM((tm, D), F32)],
        compiler_params=_cparams(("parallel", "arbitrary"), MERGE_VMEM_LIMIT),
        name="merge",
    )(proj, proj, proj, proj, proj, proj, proj, y_c, y_d, hn, h2, conv_w, sgu_norm, sgu_w, sgu_b_t,
      w_gate, w_gate, w_gate, w_gate, w_branch, w_out, ffn_g, router_pad)


def _ffn_kernel(x_ref, comb_ref, w1_ref, w3_ref, w2_ref, res_ref, g_ref, o_ref, acc_e, acc_t,
                *, n_e, n_f, final_norm):
    e = pl.program_id(1)
    f = pl.program_id(2)

    @pl.when((e == 0) & (f == 0))
    def _():
        acc_t[...] = jnp.zeros(acc_t.shape, F32)

    @pl.when(f == 0)
    def _():
        acc_e[...] = jnp.zeros(acc_e.shape, F32)

    x = x_ref[...]
    h1 = _dot(x, w1_ref[...])
    h3 = _dot(x, w3_ref[...])
    a = (h1 * jax.nn.sigmoid(h1)) * h3
    acc_e[...] += _dot(a.astype(BF16), w2_ref[...])

    @pl.when(f == n_f - 1)
    def _():
        comb = comb_ref[...]
        lane = lax.broadcasted_iota(I32, comb.shape, 1)
        c = jnp.sum(jnp.where(lane == e, comb, 0.0), axis=-1, keepdims=True)
        acc_t[...] += c * acc_e[...]

    @pl.when((e == n_e - 1) & (f == n_f - 1))
    def _():
        out = res_ref[...] + acc_t[...]
        if final_norm:
            out = _rmsnorm_rows(out, g_ref[...])
        o_ref[...] = out


def _ffn_call(xb, comb, w1, w3, w2, res, g, final_norm):
    T, D = res.shape
    n_e, _, d_ff = w1.shape
    tm = _pick(T, 512)
    tf = _pick(d_ff, 512)
    n_f = d_ff // tf
    kern = functools.partial(_ffn_kernel, n_e=n_e, n_f=n_f, final_norm=final_norm)
    rows = lambda w: pl.BlockSpec((tm, w), lambda i, e, f: (i, 0))
    return pl.pallas_call(
        kern,
        grid=(T // tm, n_e, n_f),
        in_specs=[
            rows(D), rows(LANES),
            pl.BlockSpec((None, D, tf), lambda i, e, f: (e, 0, f)),
            pl.BlockSpec((None, D, tf), lambda i, e, f: (e, 0, f)),
            pl.BlockSpec((None, tf, D), lambda i, e, f: (e, f, 0)),
            rows(D),
            pl.BlockSpec((1, D), lambda i, e, f: (0, 0)),
        ],
        out_specs=rows(D),
        out_shape=jax.ShapeDtypeStruct((T, D), F32),
        scratch_shapes=[pltpu.VMEM((tm, D), F32), pltpu.VMEM((tm, D), F32)],
        compiler_params=_cparams(("parallel", "arbitrary", "arbitrary")),
        name="ffn",
    )(xb, comb, w1, w3, w2, res, g)


def _row_copy(src_hbm, row, dst_vmem, slot, sem):
    return pltpu.make_async_copy(src_hbm.at[pl.ds(row, 1), :], dst_vmem.at[pl.ds(slot, 1), :], sem)


ROW_BLOCK = 256


def _ffn_routed_kernel(te_ref, nu_ref, tok_ref, tokn_ref, h_hbm, g_ref, w1_ref, w3_ref, w2_ref, o_ref,
                       buf, x_scr, acc, sem, *, n_f, tm):
    i = pl.program_id(0)
    f = pl.program_id(1)
    n_used = nu_ref[0]
    live = i < n_used

    def start_rows(tok):
        def body(r, c):
            _row_copy(h_hbm, tok[0, 0, r], buf, r, sem).start()
            return c
        lax.fori_loop(0, tm, body, 0, unroll=8)

    @pl.when(f == 0)
    def _():
        acc[...] = jnp.zeros(acc.shape, F32)

    @pl.when((f == 0) & (i == 0) & live)
    def _():
        start_rows(tok_ref)

    @pl.when((f == 0) & live)
    def _():
        def wait(r, c):
            _row_copy(h_hbm, tok_ref[0, 0, r], buf, r, sem).wait()
            return c

        lax.fori_loop(0, tm, wait, 0, unroll=8)
        rblk = min(ROW_BLOCK, tm)
        for rb in range(tm // rblk):
            rows = slice(rb * rblk, (rb + 1) * rblk)
            x_scr[rows, :] = _rmsnorm_rows(buf[rows, :], g_ref[...]).astype(BF16)

    @pl.when((f == min(1, n_f - 1)) & (i + 1 < n_used))
    def _():
        start_rows(tokn_ref)

    @pl.when(live)
    def _():
        x = x_scr[...]
        h1 = _dot(x, w1_ref[...])
        h3 = _dot(x, w3_ref[...])
        a = (h1 * jax.nn.sigmoid(h1)) * h3
        acc[...] += _dot(a.astype(BF16), w2_ref[...])

    @pl.when(f == n_f - 1)
    def _():
        o_ref[...] = acc[...]


def _ffn_routed_call(h2, g, row_token, tile_e, n_used, w1, w3, w2, tm):
    T, D = h2.shape
    R = row_token.shape[0]
    d_ff = w1.shape[-1]
    tf = _pick(d_ff, 512)
    n_f = d_ff // tf
    assert tm % min(ROW_BLOCK, tm) == 0
    n_tiles = R // tm
    tok = row_token.reshape(n_tiles, 1, tm)
    kern = functools.partial(_ffn_routed_kernel, n_f=n_f, tm=tm)

    def wmap(i, f, te, nu):
        return te[i], jnp.where(i < nu[0], f, n_f - 1)

    return pl.pallas_call(
        kern,
        grid_spec=pltpu.PrefetchScalarGridSpec(
            num_scalar_prefetch=2,
            grid=(R // tm, n_f),
            in_specs=[
                pl.BlockSpec((1, 1, tm), lambda i, f, te, nu: (i, 0, 0), memory_space=pltpu.SMEM),
                pl.BlockSpec((1, 1, tm), lambda i, f, te, nu: (jnp.minimum(i + 1, n_tiles - 1), 0, 0),
                             memory_space=pltpu.SMEM),
                pl.BlockSpec(memory_space=pl.ANY),
                pl.BlockSpec((1, D), lambda i, f, te, nu: (0, 0)),
                pl.BlockSpec((None, D, tf), lambda i, f, te, nu: (wmap(i, f, te, nu)[0], 0, wmap(i, f, te, nu)[1])),
                pl.BlockSpec((None, D, tf), lambda i, f, te, nu: (wmap(i, f, te, nu)[0], 0, wmap(i, f, te, nu)[1])),
                pl.BlockSpec((None, tf, D), lambda i, f, te, nu: (wmap(i, f, te, nu)[0], wmap(i, f, te, nu)[1], 0)),
            ],
            out_specs=pl.BlockSpec((tm, D), lambda i, f, te, nu: (i, 0)),
            scratch_shapes=[pltpu.VMEM((tm, D), F32), pltpu.VMEM((tm, D), BF16), pltpu.VMEM((tm, D), F32),
                            pltpu.SemaphoreType.DMA(())],
        ),
        out_shape=jax.ShapeDtypeStruct((R, D), F32),
        compiler_params=_cparams(("arbitrary", "arbitrary")),
        name="ffn_routed",
    )(tile_e, n_used, tok, tok, h2, g, w1, w3, w2)


def _moe_combine_kernel(pos_ref, h_ref, rw_ref, y_hbm, g_ref, o_ref, buf0, buf1, sem, *, tc, final_norm):
    def start(r, c):
        _row_copy(y_hbm, pos_ref[0, 0, 2 * r], buf0, r, sem).start()
        _row_copy(y_hbm, pos_ref[0, 0, 2 * r + 1], buf1, r, sem).start()
        return c

    lax.fori_loop(0, tc, start, 0)

    def wait(r, c):
        _row_copy(y_hbm, pos_ref[0, 0, 2 * r], buf0, r, sem).wait()
        _row_copy(y_hbm, pos_ref[0, 0, 2 * r + 1], buf1, r, sem).wait()
        return c

    lax.fori_loop(0, tc, wait, 0)
    rw = rw_ref[...]
    out = h_ref[...] + (rw[:, 0:1] * buf0[...] + rw[:, 1:2] * buf1[...])
    if final_norm:
        out = _rmsnorm_rows(out, g_ref[...])
    o_ref[...] = out


def _moe_combine_call(h2, rw, ys, pos, g, final_norm):
    T, D = h2.shape
    tc = _pick(T, 256)
    kern = functools.partial(_moe_combine_kernel, tc=tc, final_norm=final_norm)
    return pl.pallas_call(
        kern,
        grid=(T // tc,),
        in_specs=[
            pl.BlockSpec((1, 1, 2 * tc), lambda i: (i, 0, 0), memory_space=pltpu.SMEM),
            pl.BlockSpec((tc, D), lambda i: (i, 0)),
            pl.BlockSpec((tc, LANES), lambda i: (i, 0)),
            pl.BlockSpec(memory_space=pl.ANY),
            pl.BlockSpec((1, D), lambda i: (0, 0)),
        ],
        out_specs=pl.BlockSpec((tc, D), lambda i: (i, 0)),
        out_shape=jax.ShapeDtypeStruct((T, D), F32),
        scratch_shapes=[pltpu.VMEM((tc, D), F32), pltpu.VMEM((tc, D), F32), pltpu.SemaphoreType.DMA(())],
        compiler_params=_cparams(("arbitrary",)),
        name="moe_combine",
    )(pos.reshape(T // tc, 1, 2 * tc), h2, rw, ys, g)


def _route(ridx, n_experts, tm):
    T = ridx.shape[0]
    flat_e = ridx[:, :2].reshape(-1)
    onehot = (flat_e[:, None] == jnp.arange(n_experts, dtype=I32)[None, :]).astype(I32)
    cum = jnp.cumsum(onehot, axis=0)
    rank = jnp.take_along_axis(cum, flat_e[:, None], axis=1)[:, 0] - 1
    counts = cum[-1]
    padded = ((counts + tm - 1) // tm) * tm
    gend = jnp.cumsum(padded)
    gstart = gend - padded
    pos = gstart[flat_e] + rank
    R = 2 * T + n_experts * tm
    row_token = jnp.zeros((R,), I32).at[pos].set(jnp.arange(2 * T, dtype=I32) // 2)
    tile_e = jnp.searchsorted(gend, jnp.arange(R // tm, dtype=I32) * tm, side="right")
    tile_e = jnp.minimum(tile_e, n_experts - 1).astype(I32)
    n_used = (gend[-1:] // tm).astype(I32)
    return pos.astype(I32), row_token, tile_e, n_used


def _rope_tables(seq):
    pos = jnp.arange(seq, dtype=F32)

    def tab(dim):
        inv = ROPE_THETA ** (-jnp.arange(0, dim, 2, dtype=F32) / dim)
        ang = pos[:, None] * inv[None, :]
        cos = jnp.concatenate([jnp.cos(ang), jnp.cos(ang)], axis=-1)
        sin = jnp.concatenate([-jnp.sin(ang), jnp.sin(ang)], axis=-1)
        return cos, sin

    c64, s64 = tab(IDX_DIM)
    c128, s128 = tab(HEAD_DIM)
    return (jnp.tile(c64, (1, 2)), jnp.tile(s64, (1, 2)), c128, s128)


def kernel(x, attn_norm, w_in, conv_w, sgu_norm, sgu_w, sgu_b, diff_lambda, diff_subln, w_branch, w_out,
           ffn_norm, dense_w1, dense_w3, dense_w2, router, moe_w1, moe_w3, moe_w2, final_norm):
    batch, seq, D = x.shape
    depth = w_in.shape[0]
    n_experts = router.shape[-1]
    T = batch * seq
    rope = _rope_tables(seq)
    h = x.reshape(T, D)
    ik0 = MAIN_WIDTH
    iw0 = ik0 + IDX_DIM
    g0 = iw0 + IDX_HEADS
    ones_comb = jnp.zeros((T, LANES), F32).at[:, 0].set(1.0)

    for layer in range(depth):
        wl = w_in[layer]
        w_main = wl[:, :MAIN_WIDTH].astype(BF16)
        w_small = jnp.concatenate(
            [wl[:, ik0:iw0], wl[:, ik0:iw0], wl[:, iw0:g0], jnp.zeros((D, LANES - IDX_HEADS), F32)],
            axis=1).astype(BF16)
        w_gate = wl[:, g0:].astype(BF16)
        lam_init = 0.8 - 0.6 * math.exp(-0.3 * layer)

        proj, hn, ikd, iw = _proj_call(h, attn_norm[layer][None, :], w_main, w_small, rope, seq)
        y_c = _diff_call(proj, diff_lambda[layer], diff_subln[layer][:, None], lam_init, batch, seq)
        mask = _dsa_select_call(proj, ikd, iw[:, :IDX_HEADS].T, batch, seq)
        y_d = _dsa_attn_call(proj, mask, batch, seq)

        j = layer // 2
        is_moe = layer % 2 == 1
        router_pad = jnp.zeros((D, LANES), F32)
        if is_moe:
            router_pad = router_pad.at[:, :n_experts].set(router[j])
        h, hn2, ridx, rw = _merge_call(
            proj, y_c, y_d, hn, h, conv_w[layer], sgu_norm[layer][None, :], sgu_w[layer],
            sgu_b[layer].T, w_gate, w_branch[layer].astype(BF16), w_out[layer].astype(BF16),
            ffn_norm[layer][None, :], router_pad.astype(BF16), seq, n_experts)

        last = layer == depth - 1
        if is_moe:
            tm = _pick(T, 1024)
            pos, row_token, tile_e, n_used = _route(ridx, n_experts, tm)
            ys = _ffn_routed_call(h, ffn_norm[layer][None, :], row_token, tile_e, n_used,
                                  moe_w1[j].astype(BF16), moe_w3[j].astype(BF16), moe_w2[j].astype(BF16), tm)
            h = _moe_combine_call(h, rw, ys, pos, final_norm[None, :], last)
        else:
            h = _ffn_call(hn2, ones_comb, dense_w1[j][None].astype(BF16), dense_w3[j][None].astype(BF16),
                          dense_w2[j][None].astype(BF16), h, final_norm[None, :], last)

    if depth == 0:
        raise ValueError("depth must be positive")
    return h.reshape(batch, seq, D)
```

```python
import functools
import math

import jax
import jax.numpy as jnp
from jax import lax
from jax.experimental import pallas as pl
from jax.experimental.pallas import tpu as pltpu

F32 = jnp.float32
BF16 = jnp.bfloat16
I32 = jnp.int32

LANES = 128
SLAB = 8
N_COUNTERS = 4
PACK = 16
INF_KEY16 = 0x7F80
MIN_NORMAL_KEY16 = 0x0080
MIN_NORMAL_F32 = 1.1754943508222875e-38
HEAD_DIM = 128
BRANCH_WIDTH = 512
N_BRANCHES = 4
N_HEADS = 4
IDX_HEADS = 8
IDX_DIM = 64
DIFF_SUB_DIM = 64
CONV_K = 3
CHUNK = 128
TOPK_MAX = 256
ROPE_THETA = 10000.0
EPS = 1e-6
N_MAIN_TILES = 12
MAIN_WIDTH = N_MAIN_TILES * BRANCH_WIDTH
(T_CB, T_CC, T_CH, T_SU, T_SV, T_DQ, T_DK, T_DV, T_AQ, T_AK, T_AV, T_IQ) = range(12)
NEG_BIG = -1e30
INT_MIN = -(2 ** 31)
VMEM_LIMIT = 56 * 1024 * 1024
MERGE_VMEM_LIMIT = 61 * 1024 * 1024


def _pick(n, pref):
    t = min(pref, n)
    while n % t:
        t //= 2
    return t


def _cparams(sem, vmem_limit=VMEM_LIMIT):
    return pltpu.CompilerParams(dimension_semantics=sem, vmem_limit_bytes=vmem_limit)


def _dot(a, b):
    return jnp.dot(a, b, preferred_element_type=F32)


def _dot_nt(a, b):
    return lax.dot_general(a, b, (((1,), (1,)), ((), ())), preferred_element_type=F32)


def _dot_tn(a, b):
    return lax.dot_general(a, b, (((0,), (0,)), ((), ())), preferred_element_type=F32)


def _rmsnorm_rows(x, g):
    ms = jnp.mean(x * x, axis=-1, keepdims=True)
    return (x * lax.rsqrt(ms + EPS)) * g


def _rope64(x, c, s):
    lane = lax.broadcasted_iota(I32, x.shape, 1)
    rot = jnp.where((lane & 32) == 0, pltpu.roll(x, 96, 1), pltpu.roll(x, 32, 1))
    return x * c + rot * s


def _rope128(x, c, s):
    return x * c + pltpu.roll(x, 64, 1) * s


def _gelu_tanh(x):
    c = math.sqrt(2.0 / math.pi)
    return x * (0.5 * (1.0 + jnp.tanh(c * (x + 0.044715 * (x * x * x)))))


def _proj_kernel(x_ref, g_ref, w_ref, ws_ref, c64_ref, s64_ref, c128_ref, s128_ref,
                 proj_ref, hn_ref, ikd_ref, iw_ref, hn_scr):
    j = pl.program_id(1)

    @pl.when(j == 0)
    def _():
        hb = _rmsnorm_rows(x_ref[...], g_ref[...]).astype(BF16)
        hn_scr[...] = hb
        hn_ref[...] = hb
        small = _dot(hb, ws_ref[...])
        ikd_ref[...] = _rope64(small[:, :LANES], c64_ref[...], s64_ref[...]).astype(BF16)
        iw_ref[...] = small[:, LANES:]

    acc = _dot(hn_scr[...], w_ref[...])
    is64 = (j == T_DQ) | (j == T_DK) | (j == T_IQ)
    is128 = (j == T_AQ) | (j == T_AK)

    @pl.when(jnp.logical_not(is64 | is128))
    def _():
        proj_ref[...] = acc.astype(BF16)

    @pl.when(is64)
    def _():
        scale = jnp.where(j == T_DQ, DIFF_SUB_DIM ** -0.5, 1.0).astype(F32)
        c, s = c64_ref[...], s64_ref[...]
        for cb in range(BRANCH_WIDTH // LANES):
            sl = slice(cb * LANES, (cb + 1) * LANES)
            proj_ref[:, sl] = (_rope64(acc[:, sl], c, s) * scale).astype(BF16)

    @pl.when(is128)
    def _():
        c, s = c128_ref[...], s128_ref[...]
        for cb in range(BRANCH_WIDTH // LANES):
            sl = slice(cb * LANES, (cb + 1) * LANES)
            proj_ref[:, sl] = _rope128(acc[:, sl], c, s).astype(BF16)


def _proj_call(h2, g, w_main, w_small, rope, seq):
    T, D = h2.shape
    tm = _pick(seq, 1024)
    nrb = seq // tm
    rope_spec = pl.BlockSpec((tm, LANES), lambda i, j: (i % nrb, 0))
    row_spec = lambda w: pl.BlockSpec((tm, w), lambda i, j: (i, 0))
    return pl.pallas_call(
        _proj_kernel,
        grid=(T // tm, N_MAIN_TILES),
        in_specs=[
            row_spec(D),
            pl.BlockSpec((1, D), lambda i, j: (0, 0)),
            pl.BlockSpec((D, BRANCH_WIDTH), lambda i, j: (0, j)),
            pl.BlockSpec((D, 2 * LANES), lambda i, j: (0, 0)),
            rope_spec, rope_spec, rope_spec, rope_spec,
        ],
        out_specs=[
            pl.BlockSpec((tm, BRANCH_WIDTH), lambda i, j: (i, j)),
            row_spec(D), row_spec(LANES), row_spec(LANES),
        ],
        out_shape=[
            jax.ShapeDtypeStruct((T, MAIN_WIDTH), BF16),
            jax.ShapeDtypeStruct((T, D), BF16),
            jax.ShapeDtypeStruct((T, LANES), BF16),
            jax.ShapeDtypeStruct((T, LANES), F32),
        ],
        scratch_shapes=[pltpu.VMEM((tm, D), BF16)],
        compiler_params=_cparams(("parallel", "arbitrary")),
        name="proj",
    )(h2, g, w_main, w_small, *rope)


def _softmax_step_t(parts, m_scr, l_scr, acc_scr):
    m_prev = m_scr[...]
    m_new = m_prev
    for s_t, _ in parts:
        m_new = jnp.maximum(m_new, jnp.max(s_t, axis=0, keepdims=True))
    alpha = jnp.exp(m_prev - m_new)
    l_new = alpha * l_scr[...]
    acc = alpha * acc_scr[...]
    for s_t, v in parts:
        p_t = jnp.exp(s_t - m_new)
        l_new = l_new + jnp.sum(p_t, axis=0, keepdims=True)
        acc = acc + _dot_tn(v, p_t.astype(BF16))
    l_scr[...] = l_new
    acc_scr[...] = acc
    m_scr[...] = m_new


def _init_softmax_state(m_scr, l_scr, acc_scr):
    m_scr[...] = jnp.full(m_scr.shape, NEG_BIG, F32)
    l_scr[...] = jnp.zeros(l_scr.shape, F32)
    acc_scr[...] = jnp.zeros(acc_scr.shape, F32)


def _pipelined_tiles(n, qk, sm):
    qk(0, 0)

    def pair(p, carry):
        qk(2 * p + 1, 1)
        sm(2 * p, 0, False)
        qk(2 * p + 2, 0)
        sm(2 * p + 1, 1, False)
        return carry

    n_pairs = (n - 1) // 2
    lax.fori_loop(0, n_pairs, pair, 0)
    rest = n - 2 * n_pairs

    @pl.when(rest == 1)
    def _():
        sm(n - 1, 0, True)

    @pl.when(rest == 2)
    def _():
        qk(n - 1, 1)
        sm(n - 2, 0, False)
        sm(n - 1, 1, True)


def _diff_kernel(q_ref, k_ref, v_ref, dl_ref, subln_ref, o_ref,
                 qq_scr, s_scr, m_scr, l_scr, acc_scr, *, tq, lam_init):
    qi = pl.program_id(2)
    q = q_ref[...]
    lane = lax.broadcasted_iota(I32, q.shape, 1)
    zero = jnp.zeros_like(q)
    qq_scr[0:tq, :] = jnp.where(lane < DIFF_SUB_DIM, q, zero)
    qq_scr[tq:2 * tq, :] = jnp.where(lane >= DIFF_SUB_DIM, q, zero)
    _init_softmax_state(m_scr, l_scr, acc_scr)

    def qk(kj, slot):
        off = pl.multiple_of(kj * tq, tq)
        s_scr[slot] = _dot_nt(k_ref[pl.ds(off, tq), :], qq_scr[...])

    def sm(kj, slot, causal):
        off = pl.multiple_of(kj * tq, tq)
        s_t = s_scr[slot]
        if causal:
            kpos = lax.broadcasted_iota(I32, s_t.shape, 0)
            qpos = lax.broadcasted_iota(I32, s_t.shape, 1)
            qpos = jnp.where(qpos >= tq, qpos - tq, qpos)
            s_t = jnp.where(kpos <= qpos, s_t, NEG_BIG)
        _softmax_step_t([(s_t, v_ref[pl.ds(off, tq), :])], m_scr, l_scr, acc_scr)

    _pipelined_tiles(qi + 1, qk, sm)

    dl = dl_ref[...]
    lam = (jnp.exp(jnp.sum(dl[0:1] * dl[1:2], axis=-1, keepdims=True))
           - jnp.exp(jnp.sum(dl[2:3] * dl[3:4], axis=-1, keepdims=True)) + lam_init)
    o_all = acc_scr[...] / l_scr[...]
    o_t = o_all[:, 0:tq] - lam * o_all[:, tq:2 * tq]
    ms = jnp.mean(o_t * o_t, axis=0, keepdims=True)
    o_t = (o_t * lax.rsqrt(ms + EPS)) * subln_ref[...] * (1.0 - lam_init)
    o_ref[...] = o_t.T.astype(BF16)


def _diff_call(proj, diff_lambda, subln_col, lam_init, batch, seq):
    T = proj.shape[0]
    tq = _pick(seq, 512)
    nq = seq // tq
    hpt = BRANCH_WIDTH // LANES
    kern = functools.partial(_diff_kernel, tq=tq, lam_init=lam_init)
    return pl.pallas_call(
        kern,
        grid=(batch, N_HEADS, nq),
        in_specs=[
            pl.BlockSpec((tq, LANES), lambda b, h, i: (b * nq + i, T_DQ * hpt + h)),
            pl.BlockSpec((seq, LANES), lambda b, h, i: (b, T_DK * hpt + h)),
            pl.BlockSpec((seq, LANES), lambda b, h, i: (b, T_DV * hpt + h)),
            pl.BlockSpec((4, DIFF_SUB_DIM), lambda b, h, i: (0, 0)),
            pl.BlockSpec((HEAD_DIM, 1), lambda b, h, i: (0, 0)),
        ],
        out_specs=pl.BlockSpec((tq, LANES), lambda b, h, i: (b * nq + i, h)),
        out_shape=jax.ShapeDtypeStruct((T, BRANCH_WIDTH), BF16),
        scratch_shapes=[
            pltpu.VMEM((2 * tq, LANES), BF16),
            pltpu.VMEM((2, tq, 2 * tq), F32),
            pltpu.VMEM((1, 2 * tq), F32),
            pltpu.VMEM((1, 2 * tq), F32),
            pltpu.VMEM((HEAD_DIM, 2 * tq), F32),
        ],
        compiler_params=_cparams(("parallel", "parallel", "arbitrary")),
        name="diff_attn",
    )(proj, proj, proj, diff_lambda, subln_col)


def _dsa_select_kernel(iq_ref, ikd_ref, iwt_ref, mask_ref, key_scr, hi_scr, *, tq, tc, nc, topk, idx_bits):
    qi = pl.program_id(1)
    n_ch = (qi * tq + tq + tc - 1) // tc
    iq = iq_ref[...]
    iwt = iwt_ref[...]
    lane = lax.broadcasted_iota(I32, (tq, LANES), 1)
    zero = jnp.zeros((tq, LANES), BF16)
    qm = []
    for h in range(IDX_HEADS):
        blk = iq[:, (h // 2) * LANES:(h // 2 + 1) * LANES]
        keep = (lane < IDX_DIM) if h % 2 == 0 else (lane >= IDX_DIM)
        qm.append(jnp.where(keep, blk, zero))
    idx_scale = IDX_DIM ** -0.5 * IDX_HEADS ** -0.5
    qpos = qi * tq + lax.broadcasted_iota(I32, (tc, tq), 1)
    krow = lax.broadcasted_iota(I32, (tc, tq), 0)
    krow8 = lax.broadcasted_iota(I32, (SLAB, tq), 0)

    def score_chunk(c, carry):
        ik = ikd_ref[pl.ds(pl.multiple_of(c * tc, tc), tc), :]
        sc = jnp.zeros((tc, tq), F32)
        for h in range(IDX_HEADS):
            sc = sc + iwt[h:h + 1, :] * jnp.maximum(_dot_nt(ik, qm[h]), 0.0)
        sc = sc * idx_scale
        sc = jnp.where(jnp.abs(sc) < MIN_NORMAL_F32, 0.0, sc)
        sc = jnp.where(c * tc + krow <= qpos, sc, -jnp.inf)
        bits = pltpu.bitcast(sc, I32)
        key_scr[c] = jnp.where(bits < 0, INT_MIN - bits, bits)
        hi_scr[c] = sc.astype(BF16)
        return carry

    lax.fori_loop(0, n_ch, score_chunk, 0)

    def count(pred):
        def body(c, accs):
            accs = list(accs)
            for r in range(tc // SLAB):
                ks = key_scr[c, r * SLAB:(r + 1) * SLAB, :]
                accs[r % N_COUNTERS] = accs[r % N_COUNTERS] + jnp.where(pred(ks, c * tc + r * SLAB), 1, 0)
            return tuple(accs)
        zero = jnp.zeros((SLAB, tq), I32)
        accs = lax.fori_loop(0, n_ch, body, (zero,) * N_COUNTERS)
        return jnp.sum(functools.reduce(lambda a, b: a + b, accs), axis=0, keepdims=True)

    def rows8(x):
        return jnp.broadcast_to(x, (SLAB, tq))

    def count_ge(cand):
        c8 = rows8(cand)
        return count(lambda ks, base: ks >= c8)

    def count16(cand_val):
        c16 = jnp.broadcast_to(cand_val, (PACK, tq)).astype(BF16)
        one = jnp.ones((PACK, tq), BF16)
        zero16 = jnp.zeros((PACK, tq), BF16)

        def body(c, accs):
            accs = list(accs)
            for r in range(tc // PACK):
                hs = hi_scr[c, r * PACK:(r + 1) * PACK, :]
                accs[r % N_COUNTERS] = accs[r % N_COUNTERS] + jnp.where(hs >= c16, one, zero16)
            return tuple(accs)
        accs = lax.fori_loop(0, n_ch, body, (zero16,) * N_COUNTERS)
        tot = functools.reduce(lambda a, b: a + b, [a.astype(F32) for a in accs])
        return jnp.sum(tot, axis=0, keepdims=True)

    def key16_value(k16):
        k16 = jnp.clip(k16, -INF_KEY16, INF_KEY16)
        k16 = jnp.where((k16 > 0) & (k16 < MIN_NORMAL_KEY16), MIN_NORMAL_KEY16, k16)
        bits16 = jnp.where(k16 >= 0, k16, -32768 - k16)
        return pltpu.bitcast(jnp.left_shift(bits16, 16), F32)

    nonneg = count16(jnp.zeros((1, tq), F32)) >= topk
    thr16 = jnp.where(nonneg, 0, -32768).astype(I32)

    def bit16_body(it, thr16):
        cand = thr16 + jnp.left_shift(jnp.int32(1), 14 - it)
        return jnp.where(count16(key16_value(cand)) >= topk, cand, thr16)

    thr16 = lax.fori_loop(0, 15, bit16_body, thr16)

    thr = jnp.left_shift(thr16, 16) - 2 ** 15

    def bit_body(it, thr):
        cand = thr + jnp.left_shift(jnp.int32(1), 16 - it)
        return jnp.where(count_ge(cand) >= topk, cand, thr)

    thr = lax.fori_loop(0, 17, bit_body, thr)
    thr8 = rows8(thr)
    cnt_ge = count_ge(thr)
    excess = jnp.max(cnt_ge - topk) > 0

    def tie_search(_):
        need = topk - count(lambda ks, base: ks > thr8)

        def tbit(it, p):
            cand = p + jnp.left_shift(jnp.int32(1), idx_bits - 1 - it)
            cand8 = rows8(cand)
            cnt = count(lambda ks, base: (ks == thr8) & (base + krow8 < cand8))
            return jnp.where(cnt < need, cand, p)
        return lax.fori_loop(0, idx_bits, tbit, jnp.zeros((1, tq), I32))

    last = lax.cond(excess, tie_search, lambda _: jnp.full((1, tq), 2 ** idx_bits, I32), 0)

    def emit(c, carry):
        kc = key_scr[c]
        kpos = c * tc + krow
        sel = (kc > thr) | ((kc == thr) & (kpos <= last))
        sel = sel & (kpos <= qpos)
        mask_ref[0, 0, c] = jnp.where(sel, 1, 0).astype(jnp.int8)
        return carry

    lax.fori_loop(0, n_ch, emit, 0)

    def clear(c, carry):
        mask_ref[0, 0, c] = jnp.zeros((tc, tq), jnp.int8)
        return carry

    lax.fori_loop(n_ch, nc, clear, 0)


def _dsa_tiles(seq):
    tq = _pick(seq, 256)
    tc = _pick(seq, 512)
    return tq, tc


def _dsa_select_call(proj, ikd, iw_t, batch, seq):
    tq, tc = _dsa_tiles(seq)
    nq, nc = seq // tq, seq // tc
    topk = min(TOPK_MAX, seq // 4)
    assert tc >= topk
    assert (tc // PACK // N_COUNTERS) * nc <= 256
    idx_bits = max(1, (seq - 1).bit_length())
    kern = functools.partial(_dsa_select_kernel, tq=tq, tc=tc, nc=nc, topk=topk, idx_bits=idx_bits)
    return pl.pallas_call(
        kern,
        grid=(batch, nq),
        in_specs=[
            pl.BlockSpec((tq, BRANCH_WIDTH), lambda b, i: (b * nq + i, T_IQ)),
            pl.BlockSpec((seq, LANES), lambda b, i: (b, 0)),
            pl.BlockSpec((IDX_HEADS, tq), lambda b, i: (0, b * nq + i)),
        ],
        out_specs=pl.BlockSpec((1, 1, nc, tc, tq), lambda b, i: (b, i, 0, 0, 0)),
        out_shape=jax.ShapeDtypeStruct((batch, nq, nc, tc, tq), jnp.int8),
        scratch_shapes=[pltpu.VMEM((nc, tc, tq), I32), pltpu.VMEM((nc, tc, tq), BF16)],
        compiler_params=_cparams(("parallel", "arbitrary")),
        name="dsa_select",
    )(proj, ikd, iw_t)


def _dsa_attn_kernel(q_ref, k_ref, v_ref, mask_ref, o_ref, s_scr, m_scr, l_scr, acc_scr, *, tq, tc, cpi):
    qi = pl.program_id(2)
    n_ch = (qi * tq + tq + tc - 1) // tc
    q = q_ref[...]
    scale = HEAD_DIM ** -0.5
    _init_softmax_state(m_scr, l_scr, acc_scr)

    def qk(it, slot):
        for u in range(cpi):
            c = it * cpi + u
            off = pl.multiple_of(c * tc, tc)
            s_t = _dot_nt(k_ref[pl.ds(off, tc), :], q) * scale
            sel = mask_ref[0, 0, c].astype(I32) != 0
            s_scr[slot, u] = jnp.where(sel, s_t, NEG_BIG)

    def sm(it, slot, last):
        parts = []
        for u in range(cpi):
            off = pl.multiple_of((it * cpi + u) * tc, tc)
            parts.append((s_scr[slot, u], v_ref[pl.ds(off, tc), :]))
        _softmax_step_t(parts, m_scr, l_scr, acc_scr)

    _pipelined_tiles((n_ch + cpi - 1) // cpi, qk, sm)
    o_ref[...] = (acc_scr[...] / l_scr[...]).T.astype(BF16)


def _dsa_attn_call(proj, mask, batch, seq):
    T = proj.shape[0]
    tq, tc = _dsa_tiles(seq)
    nq, nc = seq // tq, seq // tc
    hpt = BRANCH_WIDTH // LANES
    cpi = 2 if nc % 2 == 0 else 1
    kern = functools.partial(_dsa_attn_kernel, tq=tq, tc=tc, cpi=cpi)
    return pl.pallas_call(
        kern,
        grid=(batch, N_HEADS, nq),
        in_specs=[
            pl.BlockSpec((tq, LANES), lambda b, h, i: (b * nq + i, T_AQ * hpt + h)),
            pl.BlockSpec((seq, LANES), lambda b, h, i: (b, T_AK * hpt + h)),
            pl.BlockSpec((seq, LANES), lambda b, h, i: (b, T_AV * hpt + h)),
            pl.BlockSpec((1, 1, nc, tc, tq), lambda b, h, i: (b, i, 0, 0, 0)),
        ],
        out_specs=pl.BlockSpec((tq, LANES), lambda b, h, i: (b * nq + i, h)),
        out_shape=jax.ShapeDtypeStruct((T, BRANCH_WIDTH), BF16),
        scratch_shapes=[
            pltpu.VMEM((2, cpi, tc, tq), F32),
            pltpu.VMEM((1, tq), F32),
            pltpu.VMEM((1, tq), F32),
            pltpu.VMEM((HEAD_DIM, tq), F32),
        ],
        compiler_params=_cparams(("parallel", "parallel", "arbitrary")),
        name="dsa_attn",
    )(proj, proj, proj, mask)


def _merge_kernel(cb_ref, cc_ref, ch_ref, ccp_ref, chp_ref, su_ref, sv_ref, yc_ref, yd_ref,
                  hn_ref, res_ref, convw_ref, sgun_ref, sguw_ref, sgub_ref,
                  wg0_ref, wg1_ref, wg2_ref, wg3_ref, wb_ref, wo_ref, fng_ref, router_ref,
                  h_ref, hn2_ref, ridx_ref, rw_ref, ycat_scr, acc_scr, *, tm, rows_per_seq, n_j, n_experts):
    i = pl.program_id(0)
    j = pl.program_id(1)

    @pl.when(j == 0)
    def _():
        acc_scr[...] = jnp.zeros(acc_scr.shape, F32)
        z = cc_ref[...].astype(F32) * ch_ref[...].astype(F32)
        first = ((i * tm) % rows_per_seq) == 0
        zp = ccp_ref[...].astype(F32) * chp_ref[...].astype(F32)
        zp = zp * jnp.where(first, 0.0, 1.0).astype(F32)
        row = lax.broadcasted_iota(I32, z.shape, 0)
        z1 = jnp.where(row == 0, zp[7:8, :], pltpu.roll(z, 1, 0))
        z2 = jnp.where(row == 0, zp[6:7, :], jnp.where(row == 1, zp[7:8, :], pltpu.roll(z, 2, 0)))
        cw = convw_ref[...]
        conv = cw[0:1, :] * z2 + cw[1:2, :] * z1 + cw[2:3, :] * z
        ycat_scr[:, 0:BRANCH_WIDTH] = (cb_ref[...].astype(F32) * conv).astype(BF16)
        u = _gelu_tanh(su_ref[...].astype(F32))
        v = _rmsnorm_rows(_gelu_tanh(sv_ref[...].astype(F32)), sgun_ref[...]).astype(BF16)
        tri_r = lax.broadcasted_iota(I32, (CHUNK, CHUNK), 0)
        tri_c = lax.broadcasted_iota(I32, (CHUNK, CHUNK), 1)
        bias = sgub_ref[...]
        for g in range(BRANCH_WIDTH // LANES):
            wg = jnp.where(tri_c <= tri_r, sguw_ref[g], 0.0).astype(BF16)
            cols = slice(g * LANES, (g + 1) * LANES)
            for ck in range(tm // CHUNK):
                rows = slice(ck * CHUNK, (ck + 1) * CHUNK)
                y = _dot(wg, v[rows, cols]) + bias[:, g:g + 1]
                ycat_scr[rows, BRANCH_WIDTH + g * LANES:BRANCH_WIDTH + (g + 1) * LANES] = (
                    u[rows, cols] * y).astype(BF16)
        ycat_scr[:, 2 * BRANCH_WIDTH:3 * BRANCH_WIDTH] = yc_ref[...]
        ycat_scr[:, 3 * BRANCH_WIDTH:4 * BRANCH_WIDTH] = yd_ref[...]

    hn = hn_ref[...]
    merged = None
    for n, wg_ref in enumerate((wg0_ref, wg1_ref, wg2_ref, wg3_ref)):
        gate = jax.nn.sigmoid(_dot(hn, wg_ref[...]))
        br = _dot(ycat_scr[:, n * BRANCH_WIDTH:(n + 1) * BRANCH_WIDTH], wb_ref[n])
        merged = gate * br if merged is None else merged + gate * br
    acc_scr[...] += _dot(merged.astype(BF16), wo_ref[...])

    @pl.when(j == n_j - 1)
    def _():
        h_new = res_ref[...] + acc_scr[...]
        h_ref[...] = h_new
        hn2 = _rmsnorm_rows(h_new, fng_ref[...])
        hb = hn2.astype(BF16)
        hn2_ref[...] = hb
        logits = _dot(hb, router_ref[...])
        lane = lax.broadcasted_iota(I32, logits.shape, 1)
        logits = jnp.where(lane < n_experts, logits, -jnp.inf)
        v1 = jnp.max(logits, axis=-1, keepdims=True)
        i1 = jnp.min(jnp.where(logits == v1, lane, LANES), axis=-1, keepdims=True)
        rest = jnp.where(lane == i1, -jnp.inf, logits)
        v2 = jnp.max(rest, axis=-1, keepdims=True)
        i2 = jnp.min(jnp.where(rest == v2, lane, LANES), axis=-1, keepdims=True)
        e2 = jnp.exp(v2 - v1)
        w1 = 1.0 / (1.0 + e2)
        w2 = e2 / (1.0 + e2)
        ridx_ref[...] = jnp.where(lane == 0, i1, jnp.where(lane == 1, i2, 0))
        rw_ref[...] = jnp.where(lane == 0, w1, jnp.where(lane == 1, w2, 0.0))


def _merge_call(proj, y_c, y_d, hn, h2, conv_w, sgu_norm, sgu_w, sgu_b_t, w_gate, w_branch, w_out,
                ffn_g, router_pad, seq, n_experts):
    T, D = h2.shape
    tm = _pick(seq, 512)
    tn = _pick(D, 256)
    n_j = D // tn
    rb8 = tm // 8
    kern = functools.partial(_merge_kernel, tm=tm, rows_per_seq=seq, n_j=n_j, n_experts=n_experts)
    tile = lambda t: pl.BlockSpec((tm, BRANCH_WIDTH), lambda i, j, t=t: (i, t))
    prev = lambda t: pl.BlockSpec((8, BRANCH_WIDTH), lambda i, j, t=t: (jnp.maximum(i * rb8 - 1, 0), t))
    full = lambda shp: pl.BlockSpec(shp, lambda i, j: (0,) * len(shp))
    gate = lambda n: pl.BlockSpec((D, tn), lambda i, j, n=n: (0, n * n_j + j))
    rows = lambda w: pl.BlockSpec((tm, w), lambda i, j: (i, 0))
    return pl.pallas_call(
        kern,
        grid=(T // tm, n_j),
        in_specs=[
            tile(T_CB), tile(T_CC), tile(T_CH), prev(T_CC), prev(T_CH), tile(T_SU), tile(T_SV),
            rows(BRANCH_WIDTH), rows(BRANCH_WIDTH), rows(D), rows(D),
            full((CONV_K, BRANCH_WIDTH)), full((1, BRANCH_WIDTH)),
            full((BRANCH_WIDTH // LANES, CHUNK, CHUNK)), full((CHUNK, BRANCH_WIDTH // LANES)),
            gate(0), gate(1), gate(2), gate(3),
            pl.BlockSpec((N_BRANCHES, BRANCH_WIDTH, tn), lambda i, j: (0, 0, j)),
            pl.BlockSpec((tn, D), lambda i, j: (j, 0)),
            full((1, D)), full((D, LANES)),
        ],
        out_specs=[rows(D), rows(D), rows(LANES), rows(LANES)],
        out_shape=[
            jax.ShapeDtypeStruct((T, D), F32),
            jax.ShapeDtypeStruct((T, D), BF16),
            jax.ShapeDtypeStruct((T, LANES), I32),
            jax.ShapeDtypeStruct((T, LANES), F32),
        ],
        scratch_shapes=[pltpu.VMEM((tm, N_BRANCHES * BRANCH_WIDTH), BF16), pltpu.VMEM((tm, D), F32)],
        compiler_params=_cparams(("parallel", "arbitrary"), MERGE_VMEM_LIMIT),
        name="merge",
    )(proj, proj, proj, proj, proj, proj, proj, y_c, y_d, hn, h2, conv_w, sgu_norm, sgu_w, sgu_b_t,
      w_gate, w_gate, w_gate, w_gate, w_branch, w_out, ffn_g, router_pad)


def _ffn_dense_kernel(x_ref, w1_ref, w3_ref, w2_ref, res_ref, g_ref, o_ref, *, n_f, final_norm):
    f = pl.program_id(1)

    @pl.when(f == 0)
    def _():
        o_ref[...] = res_ref[...]

    x = x_ref[...]
    h1 = _dot(x, w1_ref[...])
    h3 = _dot(x, w3_ref[...])
    a = (h1 * jax.nn.sigmoid(h1)) * h3
    o_ref[...] += _dot(a.astype(BF16), w2_ref[...])

    if final_norm:
        @pl.when(f == n_f - 1)
        def _():
            o_ref[...] = _rmsnorm_rows(o_ref[...], g_ref[...])


def _ffn_dense_call(xb, w1, w3, w2, res, g, final_norm):
    T, D = res.shape
    d_ff = w1.shape[-1]
    tm = _pick(T, 1024)
    tf = _pick(d_ff, 512)
    n_f = d_ff // tf
    kern = functools.partial(_ffn_dense_kernel, n_f=n_f, final_norm=final_norm)
    rows = lambda w: pl.BlockSpec((tm, w), lambda i, f: (i, 0))
    return pl.pallas_call(
        kern,
        grid=(T // tm, n_f),
        in_specs=[
            rows(D),
            pl.BlockSpec((D, tf), lambda i, f: (0, f)),
            pl.BlockSpec((D, tf), lambda i, f: (0, f)),
            pl.BlockSpec((tf, D), lambda i, f: (f, 0)),
            rows(D),
            pl.BlockSpec((1, D), lambda i, f: (0, 0)),
        ],
        out_specs=rows(D),
        out_shape=jax.ShapeDtypeStruct((T, D), F32),
        compiler_params=_cparams(("parallel", "arbitrary"), MERGE_VMEM_LIMIT),
        name="ffn",
    )(xb, w1, w3, w2, res, g)


def _row_copy(src_hbm, row, dst_vmem, slot, sem):
    return pltpu.make_async_copy(src_hbm.at[pl.ds(row, 1), :], dst_vmem.at[pl.ds(slot, 1), :], sem)


ROW_BLOCK = 256


def _ffn_routed_kernel(te_ref, nu_ref, tok_ref, tokn_ref, h_hbm, g_ref, w1_ref, w3_ref, w2_ref, o_ref,
                       buf, x_scr, acc, sem, *, n_f, tm):
    i = pl.program_id(0)
    f = pl.program_id(1)
    n_used = nu_ref[0]
    live = i < n_used

    def start_rows(tok):
        def body(r, c):
            _row_copy(h_hbm, tok[0, 0, r], buf, r, sem).start()
            return c
        lax.fori_loop(0, tm, body, 0, unroll=8)

    @pl.when(f == 0)
    def _():
        acc[...] = jnp.zeros(acc.shape, F32)

    @pl.when((f == 0) & (i == 0) & live)
    def _():
        start_rows(tok_ref)

    @pl.when((f == 0) & live)
    def _():
        def wait(r, c):
            _row_copy(h_hbm, tok_ref[0, 0, r], buf, r, sem).wait()
            return c

        lax.fori_loop(0, tm, wait, 0, unroll=8)
        rblk = min(ROW_BLOCK, tm)
        for rb in range(tm // rblk):
            rows = slice(rb * rblk, (rb + 1) * rblk)
            x_scr[rows, :] = _rmsnorm_rows(buf[rows, :], g_ref[...]).astype(BF16)

    @pl.when((f == min(1, n_f - 1)) & (i + 1 < n_used))
    def _():
        start_rows(tokn_ref)

    @pl.when(live)
    def _():
        x = x_scr[...]
        h1 = _dot(x, w1_ref[...])
        h3 = _dot(x, w3_ref[...])
        a = (h1 * jax.nn.sigmoid(h1)) * h3
        acc[...] += _dot(a.astype(BF16), w2_ref[...])

    @pl.when(f == n_f - 1)
    def _():
        o_ref[...] = acc[...]


def _ffn_routed_call(h2, g, row_token, tile_e, n_used, w1, w3, w2, tm):
    T, D = h2.shape
    R = row_token.shape[0]
    d_ff = w1.shape[-1]
    tf = _pick(d_ff, 512)
    n_f = d_ff // tf
    assert tm % min(ROW_BLOCK, tm) == 0
    n_tiles = R // tm
    tok = row_token.reshape(n_tiles, 1, tm)
    kern = functools.partial(_ffn_routed_kernel, n_f=n_f, tm=tm)

    def wmap(i, f, te, nu):
        return te[i], jnp.where(i < nu[0], f, n_f - 1)

    return pl.pallas_call(
        kern,
        grid_spec=pltpu.PrefetchScalarGridSpec(
            num_scalar_prefetch=2,
            grid=(R // tm, n_f),
            in_specs=[
                pl.BlockSpec((1, 1, tm), lambda i, f, te, nu: (i, 0, 0), memory_space=pltpu.SMEM),
                pl.BlockSpec((1, 1, tm), lambda i, f, te, nu: (jnp.minimum(i + 1, n_tiles - 1), 0, 0),
                             memory_space=pltpu.SMEM),
                pl.BlockSpec(memory_space=pl.ANY),
                pl.BlockSpec((1, D), lambda i, f, te, nu: (0, 0)),
                pl.BlockSpec((None, D, tf), lambda i, f, te, nu: (wmap(i, f, te, nu)[0], 0, wmap(i, f, te, nu)[1])),
                pl.BlockSpec((None, D, tf), lambda i, f, te, nu: (wmap(i, f, te, nu)[0], 0, wmap(i, f, te, nu)[1])),
                pl.BlockSpec((None, tf, D), lambda i, f, te, nu: (wmap(i, f, te, nu)[0], wmap(i, f, te, nu)[1], 0)),
            ],
            out_specs=pl.BlockSpec((tm, D), lambda i, f, te, nu: (i, 0)),
            scratch_shapes=[pltpu.VMEM((tm, D), F32), pltpu.VMEM((tm, D), BF16), pltpu.VMEM((tm, D), F32),
                            pltpu.SemaphoreType.DMA(())],
        ),
        out_shape=jax.ShapeDtypeStruct((R, D), F32),
        compiler_params=_cparams(("arbitrary", "arbitrary")),
        name="ffn_routed",
    )(tile_e, n_used, tok, tok, h2, g, w1, w3, w2)


def _moe_combine_kernel(pos_ref, posn_ref, h_ref, rw_ref, y_hbm, g_ref, o_ref, buf, sem, *, tc, n_tiles, final_norm):
    i = pl.program_id(0)
    slot = i % 2

    def rows(pos, s, fn):
        def body(r, c):
            fn(_row_copy(y_hbm, pos[0, 0, 2 * r], buf.at[s, 0], r, sem.at[s]))
            fn(_row_copy(y_hbm, pos[0, 0, 2 * r + 1], buf.at[s, 1], r, sem.at[s]))
            return c
        lax.fori_loop(0, tc, body, 0, unroll=8)

    @pl.when(i == 0)
    def _():
        rows(pos_ref, 0, lambda cp: cp.start())

    @pl.when(i + 1 < n_tiles)
    def _():
        rows(posn_ref, 1 - slot, lambda cp: cp.start())

    rows(pos_ref, slot, lambda cp: cp.wait())
    rw = rw_ref[...]
    out = h_ref[...] + (rw[:, 0:1] * buf[slot, 0] + rw[:, 1:2] * buf[slot, 1])
    if final_norm:
        out = _rmsnorm_rows(out, g_ref[...])
    o_ref[...] = out


def _moe_combine_call(h2, rw, ys, pos, g, final_norm):
    T, D = h2.shape
    tc = _pick(T, 256)
    n_tiles = T // tc
    pos3 = pos.reshape(n_tiles, 1, 2 * tc)
    kern = functools.partial(_moe_combine_kernel, tc=tc, n_tiles=n_tiles, final_norm=final_norm)
    return pl.pallas_call(
        kern,
        grid=(n_tiles,),
        in_specs=[
            pl.BlockSpec((1, 1, 2 * tc), lambda i: (i, 0, 0), memory_space=pltpu.SMEM),
            pl.BlockSpec((1, 1, 2 * tc), lambda i: (jnp.minimum(i + 1, n_tiles - 1), 0, 0), memory_space=pltpu.SMEM),
            pl.BlockSpec((tc, D), lambda i: (i, 0)),
            pl.BlockSpec((tc, LANES), lambda i: (i, 0)),
            pl.BlockSpec(memory_space=pl.ANY),
            pl.BlockSpec((1, D), lambda i: (0, 0)),
        ],
        out_specs=pl.BlockSpec((tc, D), lambda i: (i, 0)),
        out_shape=jax.ShapeDtypeStruct((T, D), F32),
        scratch_shapes=[pltpu.VMEM((2, 2, tc, D), F32), pltpu.SemaphoreType.DMA((2,))],
        compiler_params=_cparams(("arbitrary",)),
        name="moe_combine",
    )(pos3, pos3, h2, rw, ys, g)


def _route(ridx, n_experts, tm):
    T = ridx.shape[0]
    flat_e = ridx[:, :2].reshape(-1)
    onehot = (flat_e[:, None] == jnp.arange(n_experts, dtype=I32)[None, :]).astype(I32)
    cum = jnp.cumsum(onehot, axis=0)
    rank = jnp.take_along_axis(cum, flat_e[:, None], axis=1)[:, 0] - 1
    counts = cum[-1]
    padded = ((counts + tm - 1) // tm) * tm
    gend = jnp.cumsum(padded)
    gstart = gend - padded
    pos = gstart[flat_e] + rank
    R = 2 * T + n_experts * tm
    row_token = jnp.zeros((R,), I32).at[pos].set(jnp.arange(2 * T, dtype=I32) // 2)
    tile_e = jnp.searchsorted(gend, jnp.arange(R // tm, dtype=I32) * tm, side="right")
    tile_e = jnp.minimum(tile_e, n_experts - 1).astype(I32)
    n_used = (gend[-1:] // tm).astype(I32)
    return pos.astype(I32), row_token, tile_e, n_used


def _rope_tables(seq):
    pos = jnp.arange(seq, dtype=F32)

    def tab(dim):
        inv = ROPE_THETA ** (-jnp.arange(0, dim, 2, dtype=F32) / dim)
        ang = pos[:, None] * inv[None, :]
        cos = jnp.concatenate([jnp.cos(ang), jnp.cos(ang)], axis=-1)
        sin = jnp.concatenate([-jnp.sin(ang), jnp.sin(ang)], axis=-1)
        return cos, sin

    c64, s64 = tab(IDX_DIM)
    c128, s128 = tab(HEAD_DIM)
    return (jnp.tile(c64, (1, 2)), jnp.tile(s64, (1, 2)), c128, s128)


def kernel(x, attn_norm, w_in, conv_w, sgu_norm, sgu_w, sgu_b, diff_lambda, diff_subln, w_branch, w_out,
           ffn_norm, dense_w1, dense_w3, dense_w2, router, moe_w1, moe_w3, moe_w2, final_norm):
    batch, seq, D = x.shape
    depth = w_in.shape[0]
    n_experts = router.shape[-1]
    T = batch * seq
    rope = _rope_tables(seq)
    h = x.reshape(T, D)
    ik0 = MAIN_WIDTH
    iw0 = ik0 + IDX_DIM
    g0 = iw0 + IDX_HEADS

    for layer in range(depth):
        wl = w_in[layer]
        w_main = wl[:, :MAIN_WIDTH].astype(BF16)
        w_small = jnp.concatenate(
            [wl[:, ik0:iw0], wl[:, ik0:iw0], wl[:, iw0:g0], jnp.zeros((D, LANES - IDX_HEADS), F32)],
            axis=1).astype(BF16)
        w_gate = wl[:, g0:].astype(BF16)
        lam_init = 0.8 - 0.6 * math.exp(-0.3 * layer)

        proj, hn, ikd, iw = _proj_call(h, attn_norm[layer][None, :], w_main, w_small, rope, seq)
        y_c = _diff_call(proj, diff_lambda[layer], diff_subln[layer][:, None], lam_init, batch, seq)
        mask = _dsa_select_call(proj, ikd, iw[:, :IDX_HEADS].T, batch, seq)
        y_d = _dsa_attn_call(proj, mask, batch, seq)

        j = layer // 2
        is_moe = layer % 2 == 1
        router_pad = jnp.zeros((D, LANES), F32)
        if is_moe:
            router_pad = router_pad.at[:, :n_experts].set(router[j])
        h, hn2, ridx, rw = _merge_call(
            proj, y_c, y_d, hn, h, conv_w[layer], sgu_norm[layer][None, :], sgu_w[layer],
            sgu_b[layer].T, w_gate, w_branch[layer].astype(BF16), w_out[layer].astype(BF16),
            ffn_norm[layer][None, :], router_pad.astype(BF16), seq, n_experts)

        last = layer == depth - 1
        if is_moe:
            tm = _pick(T, 1024)
            pos, row_token, tile_e, n_used = _route(ridx, n_experts, tm)
            ys = _ffn_routed_call(h, ffn_norm[layer][None, :], row_token, tile_e, n_used,
                                  moe_w1[j].astype(BF16), moe_w3[j].astype(BF16), moe_w2[j].astype(BF16), tm)
            h = _moe_combine_call(h, rw, ys, pos, final_norm[None, :], last)
        else:
            h = _ffn_dense_call(hn2, dense_w1[j].astype(BF16), dense_w3[j].astype(BF16),
                                dense_w2[j].astype(BF16), h, final_norm[None, :], last)

    if depth == 0:
        raise ValueError("depth must be positive")
    return h.reshape(batch, seq, D)
```

```python
import functools
import math

import jax
import jax.numpy as jnp
from jax import lax
from jax.experimental import pallas as pl
from jax.experimental.pallas import tpu as pltpu

F32 = jnp.float32
BF16 = jnp.bfloat16
I32 = jnp.int32

LANES = 128
SLAB = 8
N_COUNTERS = 4
PACK = 16
INF_KEY16 = 0x7F80
MIN_NORMAL_KEY16 = 0x0080
MIN_NORMAL_F32 = 1.1754943508222875e-38
HEAD_DIM = 128
BRANCH_WIDTH = 512
N_BRANCHES = 4
N_HEADS = 4
IDX_HEADS = 8
IDX_DIM = 64
DIFF_SUB_DIM = 64
CONV_K = 3
CHUNK = 128
TOPK_MAX = 256
ROPE_THETA = 10000.0
EPS = 1e-6
N_MAIN_TILES = 12
MAIN_WIDTH = N_MAIN_TILES * BRANCH_WIDTH
(T_CB, T_CC, T_CH, T_SU, T_SV, T_DQ, T_DK, T_DV, T_AQ, T_AK, T_AV, T_IQ) = range(12)
NEG_BIG = -1e30
INT_MIN = -(2 ** 31)
VMEM_LIMIT = 56 * 1024 * 1024
MERGE_VMEM_LIMIT = 61 * 1024 * 1024


def _pick(n, pref):
    t = min(pref, n)
    while n % t:
        t //= 2
    return t


def _cparams(sem, vmem_limit=VMEM_LIMIT):
    return pltpu.CompilerParams(dimension_semantics=sem, vmem_limit_bytes=vmem_limit)


def _dot(a, b):
    return jnp.dot(a, b, preferred_element_type=F32)


def _dot_nt(a, b):
    return lax.dot_general(a, b, (((1,), (1,)), ((), ())), preferred_element_type=F32)


def _dot_tn(a, b):
    return lax.dot_general(a, b, (((0,), (0,)), ((), ())), preferred_element_type=F32)


def _rmsnorm_rows(x, g):
    ms = jnp.mean(x * x, axis=-1, keepdims=True)
    return (x * lax.rsqrt(ms + EPS)) * g


def _rope64(x, c, s):
    lane = lax.broadcasted_iota(I32, x.shape, 1)
    rot = jnp.where((lane & 32) == 0, pltpu.roll(x, 96, 1), pltpu.roll(x, 32, 1))
    return x * c + rot * s


def _rope128(x, c, s):
    return x * c + pltpu.roll(x, 64, 1) * s


def _gelu_tanh(x):
    c = math.sqrt(2.0 / math.pi)
    return x * (0.5 * (1.0 + jnp.tanh(c * (x + 0.044715 * (x * x * x)))))


def _proj_kernel(x_ref, g_ref, w_ref, ws_ref, c64_ref, s64_ref, c128_ref, s128_ref,
                 proj_ref, hn_ref, ikd_ref, iw_ref, hn_scr):
    j = pl.program_id(1)

    @pl.when(j == 0)
    def _():
        hb = _rmsnorm_rows(x_ref[...], g_ref[...]).astype(BF16)
        hn_scr[...] = hb
        hn_ref[...] = hb
        small = _dot(hb, ws_ref[...])
        ikd_ref[...] = _rope64(small[:, :LANES], c64_ref[...], s64_ref[...]).astype(BF16)
        iw_ref[...] = small[:, LANES:]

    acc = _dot(hn_scr[...], w_ref[...])
    is64 = (j == T_DQ) | (j == T_DK) | (j == T_IQ)
    is128 = (j == T_AQ) | (j == T_AK)

    @pl.when(jnp.logical_not(is64 | is128))
    def _():
        proj_ref[...] = acc.astype(BF16)

    @pl.when(is64)
    def _():
        scale = jnp.where(j == T_DQ, DIFF_SUB_DIM ** -0.5, 1.0).astype(F32)
        c, s = c64_ref[...], s64_ref[...]
        for cb in range(BRANCH_WIDTH // LANES):
            sl = slice(cb * LANES, (cb + 1) * LANES)
            proj_ref[:, sl] = (_rope64(acc[:, sl], c, s) * scale).astype(BF16)

    @pl.when(is128)
    def _():
        c, s = c128_ref[...], s128_ref[...]
        for cb in range(BRANCH_WIDTH // LANES):
            sl = slice(cb * LANES, (cb + 1) * LANES)
            proj_ref[:, sl] = _rope128(acc[:, sl], c, s).astype(BF16)


def _proj_call(h2, g, w_main, w_small, rope, seq):
    T, D = h2.shape
    tm = _pick(seq, 1024)
    nrb = seq // tm
    rope_spec = pl.BlockSpec((tm, LANES), lambda i, j: (i % nrb, 0))
    row_spec = lambda w: pl.BlockSpec((tm, w), lambda i, j: (i, 0))
    return pl.pallas_call(
        _proj_kernel,
        grid=(T // tm, N_MAIN_TILES),
        in_specs=[
            row_spec(D),
            pl.BlockSpec((1, D), lambda i, j: (0, 0)),
            pl.BlockSpec((D, BRANCH_WIDTH), lambda i, j: (0, j)),
            pl.BlockSpec((D, 2 * LANES), lambda i, j: (0, 0)),
            rope_spec, rope_spec, rope_spec, rope_spec,
        ],
        out_specs=[
            pl.BlockSpec((tm, BRANCH_WIDTH), lambda i, j: (i, j)),
            row_spec(D), row_spec(LANES), row_spec(LANES),
        ],
        out_shape=[
            jax.ShapeDtypeStruct((T, MAIN_WIDTH), BF16),
            jax.ShapeDtypeStruct((T, D), BF16),
            jax.ShapeDtypeStruct((T, LANES), BF16),
            jax.ShapeDtypeStruct((T, LANES), F32),
        ],
        scratch_shapes=[pltpu.VMEM((tm, D), BF16)],
        compiler_params=_cparams(("parallel", "arbitrary")),
        name="proj",
    )(h2, g, w_main, w_small, *rope)


def _softmax_step_t(parts, m_scr, l_scr, acc_scr):
    m_prev = m_scr[...]
    m_new = m_prev
    for s_t, _ in parts:
        m_new = jnp.maximum(m_new, jnp.max(s_t, axis=0, keepdims=True))
    alpha = jnp.exp(m_prev - m_new)
    l_new = alpha * l_scr[...]
    acc = alpha * acc_scr[...]
    for s_t, v in parts:
        p_t = jnp.exp(s_t - m_new)
        l_new = l_new + jnp.sum(p_t, axis=0, keepdims=True)
        acc = acc + _dot_tn(v, p_t.astype(BF16))
    l_scr[...] = l_new
    acc_scr[...] = acc
    m_scr[...] = m_new


def _init_softmax_state(m_scr, l_scr, acc_scr):
    m_scr[...] = jnp.full(m_scr.shape, NEG_BIG, F32)
    l_scr[...] = jnp.zeros(l_scr.shape, F32)
    acc_scr[...] = jnp.zeros(acc_scr.shape, F32)


def _pipelined_tiles(n, qk, sm):
    qk(0, 0)

    def pair(p, carry):
        qk(2 * p + 1, 1)
        sm(2 * p, 0, False)
        qk(2 * p + 2, 0)
        sm(2 * p + 1, 1, False)
        return carry

    n_pairs = (n - 1) // 2
    lax.fori_loop(0, n_pairs, pair, 0)
    rest = n - 2 * n_pairs

    @pl.when(rest == 1)
    def _():
        sm(n - 1, 0, True)

    @pl.when(rest == 2)
    def _():
        qk(n - 1, 1)
        sm(n - 2, 0, False)
        sm(n - 1, 1, True)


def _diff_kernel(q_ref, k_ref, v_ref, dl_ref, subln_ref, o_ref,
                 qq_scr, s_scr, m_scr, l_scr, acc_scr, *, tq, lam_init):
    qi = pl.program_id(2)
    q = q_ref[...]
    lane = lax.broadcasted_iota(I32, q.shape, 1)
    zero = jnp.zeros_like(q)
    qq_scr[0:tq, :] = jnp.where(lane < DIFF_SUB_DIM, q, zero)
    qq_scr[tq:2 * tq, :] = jnp.where(lane >= DIFF_SUB_DIM, q, zero)
    _init_softmax_state(m_scr, l_scr, acc_scr)

    def qk(kj, slot):
        off = pl.multiple_of(kj * tq, tq)
        s_scr[slot] = _dot_nt(k_ref[pl.ds(off, tq), :], qq_scr[...])

    def sm(kj, slot, causal):
        off = pl.multiple_of(kj * tq, tq)
        s_t = s_scr[slot]
        if causal:
            kpos = lax.broadcasted_iota(I32, s_t.shape, 0)
            qpos = lax.broadcasted_iota(I32, s_t.shape, 1)
            qpos = jnp.where(qpos >= tq, qpos - tq, qpos)
            s_t = jnp.where(kpos <= qpos, s_t, NEG_BIG)
        _softmax_step_t([(s_t, v_ref[pl.ds(off, tq), :])], m_scr, l_scr, acc_scr)

    _pipelined_tiles(qi + 1, qk, sm)

    dl = dl_ref[...]
    lam = (jnp.exp(jnp.sum(dl[0:1] * dl[1:2], axis=-1, keepdims=True))
           - jnp.exp(jnp.sum(dl[2:3] * dl[3:4], axis=-1, keepdims=True)) + lam_init)
    o_all = acc_scr[...] / l_scr[...]
    o_t = o_all[:, 0:tq] - lam * o_all[:, tq:2 * tq]
    ms = jnp.mean(o_t * o_t, axis=0, keepdims=True)
    o_t = (o_t * lax.rsqrt(ms + EPS)) * subln_ref[...] * (1.0 - lam_init)
    o_ref[...] = o_t.T.astype(BF16)


def _diff_call(proj, diff_lambda, subln_col, lam_init, batch, seq):
    T = proj.shape[0]
    tq = _pick(seq, 512)
    nq = seq // tq
    hpt = BRANCH_WIDTH // LANES
    kern = functools.partial(_diff_kernel, tq=tq, lam_init=lam_init)
    return pl.pallas_call(
        kern,
        grid=(batch, N_HEADS, nq),
        in_specs=[
            pl.BlockSpec((tq, LANES), lambda b, h, i: (b * nq + i, T_DQ * hpt + h)),
            pl.BlockSpec((seq, LANES), lambda b, h, i: (b, T_DK * hpt + h)),
            pl.BlockSpec((seq, LANES), lambda b, h, i: (b, T_DV * hpt + h)),
            pl.BlockSpec((4, DIFF_SUB_DIM), lambda b, h, i: (0, 0)),
            pl.BlockSpec((HEAD_DIM, 1), lambda b, h, i: (0, 0)),
        ],
        out_specs=pl.BlockSpec((tq, LANES), lambda b, h, i: (b * nq + i, h)),
        out_shape=jax.ShapeDtypeStruct((T, BRANCH_WIDTH), BF16),
        scratch_shapes=[
            pltpu.VMEM((2 * tq, LANES), BF16),
            pltpu.VMEM((2, tq, 2 * tq), F32),
            pltpu.VMEM((1, 2 * tq), F32),
            pltpu.VMEM((1, 2 * tq), F32),
            pltpu.VMEM((HEAD_DIM, 2 * tq), F32),
        ],
        compiler_params=_cparams(("parallel", "parallel", "arbitrary")),
        name="diff_attn",
    )(proj, proj, proj, diff_lambda, subln_col)


def _dsa_select_kernel(iq_ref, ikd_ref, iwt_ref, mask_ref, key_scr, hi_scr, *, tq, tc, nc, topk, idx_bits):
    qi = pl.program_id(1)
    n_ch = (qi * tq + tq + tc - 1) // tc
    iq = iq_ref[...]
    iwt = iwt_ref[...]
    lane = lax.broadcasted_iota(I32, (tq, LANES), 1)
    zero = jnp.zeros((tq, LANES), BF16)
    qm = []
    for h in range(IDX_HEADS):
        blk = iq[:, (h // 2) * LANES:(h // 2 + 1) * LANES]
        keep = (lane < IDX_DIM) if h % 2 == 0 else (lane >= IDX_DIM)
        qm.append(jnp.where(keep, blk, zero))
    idx_scale = IDX_DIM ** -0.5 * IDX_HEADS ** -0.5
    qpos = qi * tq + lax.broadcasted_iota(I32, (tc, tq), 1)
    krow = lax.broadcasted_iota(I32, (tc, tq), 0)
    krow8 = lax.broadcasted_iota(I32, (SLAB, tq), 0)

    def score_chunk(c, carry):
        ik = ikd_ref[pl.ds(pl.multiple_of(c * tc, tc), tc), :]
        sc = jnp.zeros((tc, tq), F32)
        for h in range(IDX_HEADS):
            sc = sc + iwt[h:h + 1, :] * jnp.maximum(_dot_nt(ik, qm[h]), 0.0)
        sc = sc * idx_scale
        sc = jnp.where(jnp.abs(sc) < MIN_NORMAL_F32, 0.0, sc)
        sc = jnp.where(c * tc + krow <= qpos, sc, -jnp.inf)
        bits = pltpu.bitcast(sc, I32)
        key_scr[c] = jnp.where(bits < 0, INT_MIN - bits, bits)
        hi_scr[c] = sc.astype(BF16)
        return carry

    lax.fori_loop(0, n_ch, score_chunk, 0)

    def count(pred):
        def body(c, accs):
            accs = list(accs)
            for r in range(tc // SLAB):
                ks = key_scr[c, r * SLAB:(r + 1) * SLAB, :]
                accs[r % N_COUNTERS] = accs[r % N_COUNTERS] + jnp.where(pred(ks, c * tc + r * SLAB), 1, 0)
            return tuple(accs)
        zero = jnp.zeros((SLAB, tq), I32)
        accs = lax.fori_loop(0, n_ch, body, (zero,) * N_COUNTERS)
        return jnp.sum(functools.reduce(lambda a, b: a + b, accs), axis=0, keepdims=True)

    def rows8(x):
        return jnp.broadcast_to(x, (SLAB, tq))

    def count_ge(cand):
        c8 = rows8(cand)
        return count(lambda ks, base: ks >= c8)

    def count16(cand_val):
        c16 = jnp.broadcast_to(cand_val, (PACK, tq)).astype(BF16)
        one = jnp.ones((PACK, tq), BF16)
        zero16 = jnp.zeros((PACK, tq), BF16)

        def body(c, accs):
            accs = list(accs)
            for r in range(tc // PACK):
                hs = hi_scr[c, r * PACK:(r + 1) * PACK, :]
                accs[r % N_COUNTERS] = accs[r % N_COUNTERS] + jnp.where(hs >= c16, one, zero16)
            return tuple(accs)
        accs = lax.fori_loop(0, n_ch, body, (zero16,) * N_COUNTERS)
        tot = functools.reduce(lambda a, b: a + b, [a.astype(F32) for a in accs])
        return jnp.sum(tot, axis=0, keepdims=True)

    def key16_value(k16):
        k16 = jnp.clip(k16, -INF_KEY16, INF_KEY16)
        k16 = jnp.where((k16 > 0) & (k16 < MIN_NORMAL_KEY16), MIN_NORMAL_KEY16, k16)
        bits16 = jnp.where(k16 >= 0, k16, -32768 - k16)
        return pltpu.bitcast(jnp.left_shift(bits16, 16), F32)

    nonneg = count16(jnp.zeros((1, tq), F32)) >= topk
    thr16 = jnp.where(nonneg, 0, -32768).astype(I32)

    def bit16_body(it, thr16):
        cand = thr16 + jnp.left_shift(jnp.int32(1), 14 - it)
        return jnp.where(count16(key16_value(cand)) >= topk, cand, thr16)

    thr16 = lax.fori_loop(0, 15, bit16_body, thr16)

    thr = jnp.left_shift(thr16, 16) - 2 ** 15

    def bit_body(it, thr):
        cand = thr + jnp.left_shift(jnp.int32(1), 16 - it)
        return jnp.where(count_ge(cand) >= topk, cand, thr)

    thr = lax.fori_loop(0, 17, bit_body, thr)
    thr8 = rows8(thr)
    cnt_ge = count_ge(thr)
    excess = jnp.max(cnt_ge - topk) > 0

    def tie_search(_):
        need = topk - count(lambda ks, base: ks > thr8)

        def tbit(it, p):
            cand = p + jnp.left_shift(jnp.int32(1), idx_bits - 1 - it)
            cand8 = rows8(cand)
            cnt = count(lambda ks, base: (ks == thr8) & (base + krow8 < cand8))
            return jnp.where(cnt < need, cand, p)
        return lax.fori_loop(0, idx_bits, tbit, jnp.zeros((1, tq), I32))

    last = lax.cond(excess, tie_search, lambda _: jnp.full((1, tq), 2 ** idx_bits, I32), 0)

    def emit(c, carry):
        kc = key_scr[c]
        kpos = c * tc + krow
        sel = (kc > thr) | ((kc == thr) & (kpos <= last))
        sel = sel & (kpos <= qpos)
        mask_ref[0, 0, c] = jnp.where(sel, 1, 0).astype(jnp.int8)
        return carry

    lax.fori_loop(0, n_ch, emit, 0)

    def clear(c, carry):
        mask_ref[0, 0, c] = jnp.zeros((tc, tq), jnp.int8)
        return carry

    lax.fori_loop(n_ch, nc, clear, 0)


def _dsa_tiles(seq):
    tq = _pick(seq, 256)
    tc = _pick(seq, 512)
    return tq, tc


def _dsa_select_call(proj, ikd, iw_t, batch, seq):
    tq, tc = _dsa_tiles(seq)
    nq, nc = seq // tq, seq // tc
    topk = min(TOPK_MAX, seq // 4)
    assert tc >= topk
    assert (tc // PACK // N_COUNTERS) * nc <= 256
    idx_bits = max(1, (seq - 1).bit_length())
    kern = functools.partial(_dsa_select_kernel, tq=tq, tc=tc, nc=nc, topk=topk, idx_bits=idx_bits)
    return pl.pallas_call(
        kern,
        grid=(batch, nq),
        in_specs=[
            pl.BlockSpec((tq, BRANCH_WIDTH), lambda b, i: (b * nq + i, T_IQ)),
            pl.BlockSpec((seq, LANES), lambda b, i: (b, 0)),
            pl.BlockSpec((IDX_HEADS, tq), lambda b, i: (0, b * nq + i)),
        ],
        out_specs=pl.BlockSpec((1, 1, nc, tc, tq), lambda b, i: (b, i, 0, 0, 0)),
        out_shape=jax.ShapeDtypeStruct((batch, nq, nc, tc, tq), jnp.int8),
        scratch_shapes=[pltpu.VMEM((nc, tc, tq), I32), pltpu.VMEM((nc, tc, tq), BF16)],
        compiler_params=_cparams(("parallel", "arbitrary")),
        name="dsa_select",
    )(proj, ikd, iw_t)


def _dsa_attn_kernel(q_ref, k_ref, v_ref, mask_ref, o_ref, s_scr, m_scr, l_scr, acc_scr, *, tq, tc, cpi):
    qi = pl.program_id(2)
    n_ch = (qi * tq + tq + tc - 1) // tc
    q = q_ref[...]
    scale = HEAD_DIM ** -0.5
    _init_softmax_state(m_scr, l_scr, acc_scr)

    def qk(it, slot):
        for u in range(cpi):
            c = it * cpi + u
            off = pl.multiple_of(c * tc, tc)
            s_t = _dot_nt(k_ref[pl.ds(off, tc), :], q) * scale
            sel = mask_ref[0, 0, c].astype(I32) != 0
            s_scr[slot, u] = jnp.where(sel, s_t, NEG_BIG)

    def sm(it, slot, last):
        parts = []
        for u in range(cpi):
            off = pl.multiple_of((it * cpi + u) * tc, tc)
            parts.append((s_scr[slot, u], v_ref[pl.ds(off, tc), :]))
        _softmax_step_t(parts, m_scr, l_scr, acc_scr)

    _pipelined_tiles((n_ch + cpi - 1) // cpi, qk, sm)
    o_ref[...] = (acc_scr[...] / l_scr[...]).T.astype(BF16)


def _dsa_attn_call(proj, mask, batch, seq):
    T = proj.shape[0]
    tq, tc = _dsa_tiles(seq)
    nq, nc = seq // tq, seq // tc
    hpt = BRANCH_WIDTH // LANES
    cpi = 2 if nc % 2 == 0 else 1
    kern = functools.partial(_dsa_attn_kernel, tq=tq, tc=tc, cpi=cpi)
    return pl.pallas_call(
        kern,
        grid=(batch, N_HEADS, nq),
        in_specs=[
            pl.BlockSpec((tq, LANES), lambda b, h, i: (b * nq + i, T_AQ * hpt + h)),
            pl.BlockSpec((seq, LANES), lambda b, h, i: (b, T_AK * hpt + h)),
            pl.BlockSpec((seq, LANES), lambda b, h, i: (b, T_AV * hpt + h)),
            pl.BlockSpec((1, 1, nc, tc, tq), lambda b, h, i: (b, i, 0, 0, 0)),
        ],
        out_specs=pl.BlockSpec((tq, LANES), lambda b, h, i: (b * nq + i, h)),
        out_shape=jax.ShapeDtypeStruct((T, BRANCH_WIDTH), BF16),
        scratch_shapes=[
            pltpu.VMEM((2, cpi, tc, tq), F32),
            pltpu.VMEM((1, tq), F32),
            pltpu.VMEM((1, tq), F32),
            pltpu.VMEM((HEAD_DIM, tq), F32),
        ],
        compiler_params=_cparams(("parallel", "parallel", "arbitrary")),
        name="dsa_attn",
    )(proj, proj, proj, mask)


def _merge_kernel(cb_ref, cc_ref, ch_ref, ccp_ref, chp_ref, su_ref, sv_ref, yc_ref, yd_ref,
                  hn_ref, res_ref, convw_ref, sgun_ref, sguw_ref, sgub_ref,
                  wg0_ref, wg1_ref, wg2_ref, wg3_ref, wb_ref, wo_ref, fng_ref, router_ref,
                  h_ref, hn2_ref, ridx_ref, rw_ref, ycat_scr, acc_scr, *, tm, rows_per_seq, n_j, n_experts):
    i = pl.program_id(0)
    j = pl.program_id(1)

    @pl.when(j == 0)
    def _():
        acc_scr[...] = jnp.zeros(acc_scr.shape, F32)
        z = cc_ref[...].astype(F32) * ch_ref[...].astype(F32)
        first = ((i * tm) % rows_per_seq) == 0
        zp = ccp_ref[...].astype(F32) * chp_ref[...].astype(F32)
        zp = zp * jnp.where(first, 0.0, 1.0).astype(F32)
        row = lax.broadcasted_iota(I32, z.shape, 0)
        z1 = jnp.where(row == 0, zp[7:8, :], pltpu.roll(z, 1, 0))
        z2 = jnp.where(row == 0, zp[6:7, :], jnp.where(row == 1, zp[7:8, :], pltpu.roll(z, 2, 0)))
        cw = convw_ref[...]
        conv = cw[0:1, :] * z2 + cw[1:2, :] * z1 + cw[2:3, :] * z
        ycat_scr[:, 0:BRANCH_WIDTH] = (cb_ref[...].astype(F32) * conv).astype(BF16)
        u = _gelu_tanh(su_ref[...].astype(F32))
        v = _rmsnorm_rows(_gelu_tanh(sv_ref[...].astype(F32)), sgun_ref[...]).astype(BF16)
        tri_r = lax.broadcasted_iota(I32, (CHUNK, CHUNK), 0)
        tri_c = lax.broadcasted_iota(I32, (CHUNK, CHUNK), 1)
        bias = sgub_ref[...]
        for g in range(BRANCH_WIDTH // LANES):
            wg = jnp.where(tri_c <= tri_r, sguw_ref[g], 0.0).astype(BF16)
            cols = slice(g * LANES, (g + 1) * LANES)
            for ck in range(tm // CHUNK):
                rows = slice(ck * CHUNK, (ck + 1) * CHUNK)
                y = _dot(wg, v[rows, cols]) + bias[:, g:g + 1]
                ycat_scr[rows, BRANCH_WIDTH + g * LANES:BRANCH_WIDTH + (g + 1) * LANES] = (
                    u[rows, cols] * y).astype(BF16)
        ycat_scr[:, 2 * BRANCH_WIDTH:3 * BRANCH_WIDTH] = yc_ref[...]
        ycat_scr[:, 3 * BRANCH_WIDTH:4 * BRANCH_WIDTH] = yd_ref[...]

    hn = hn_ref[...]
    merged = None
    for n, wg_ref in enumerate((wg0_ref, wg1_ref, wg2_ref, wg3_ref)):
        gate = jax.nn.sigmoid(_dot(hn, wg_ref[...]))
        br = _dot(ycat_scr[:, n * BRANCH_WIDTH:(n + 1) * BRANCH_WIDTH], wb_ref[n])
        merged = gate * br if merged is None else merged + gate * br
    acc_scr[...] += _dot(merged.astype(BF16), wo_ref[...])

    @pl.when(j == n_j - 1)
    def _():
        h_new = res_ref[...] + acc_scr[...]
        h_ref[...] = h_new
        hn2 = _rmsnorm_rows(h_new, fng_ref[...])
        hb = hn2.astype(BF16)
        hn2_ref[...] = hb
        logits = _dot(hb, router_ref[...])
        lane = lax.broadcasted_iota(I32, logits.shape, 1)
        logits = jnp.where(lane < n_experts, logits, -jnp.inf)
        v1 = jnp.max(logits, axis=-1, keepdims=True)
        i1 = jnp.min(jnp.where(logits == v1, lane, LANES), axis=-1, keepdims=True)
        rest = jnp.where(lane == i1, -jnp.inf, logits)
        v2 = jnp.max(rest, axis=-1, keepdims=True)
        i2 = jnp.min(jnp.where(rest == v2, lane, LANES), axis=-1, keepdims=True)
        e2 = jnp.exp(v2 - v1)
        w1 = 1.0 / (1.0 + e2)
        w2 = e2 / (1.0 + e2)
        ridx_ref[...] = jnp.where(lane == 0, i1, jnp.where(lane == 1, i2, 0))
        rw_ref[...] = jnp.where(lane == 0, w1, jnp.where(lane == 1, w2, 0.0))


def _merge_call(proj, y_c, y_d, hn, h2, conv_w, sgu_norm, sgu_w, sgu_b_t, w_gate, w_branch, w_out,
                ffn_g, router_pad, seq, n_experts):
    T, D = h2.shape
    tm = _pick(seq, 512)
    tn = _pick(D, 256)
    n_j = D // tn
    rb8 = tm // 8
    kern = functools.partial(_merge_kernel, tm=tm, rows_per_seq=seq, n_j=n_j, n_experts=n_experts)
    tile = lambda t: pl.BlockSpec((tm, BRANCH_WIDTH), lambda i, j, t=t: (i, t))
    prev = lambda t: pl.BlockSpec((8, BRANCH_WIDTH), lambda i, j, t=t: (jnp.maximum(i * rb8 - 1, 0), t))
    full = lambda shp: pl.BlockSpec(shp, lambda i, j: (0,) * len(shp))
    gate = lambda n: pl.BlockSpec((D, tn), lambda i, j, n=n: (0, n * n_j + j))
    rows = lambda w: pl.BlockSpec((tm, w), lambda i, j: (i, 0))
    return pl.pallas_call(
        kern,
        grid=(T // tm, n_j),
        in_specs=[
            tile(T_CB), tile(T_CC), tile(T_CH), prev(T_CC), prev(T_CH), tile(T_SU), tile(T_SV),
            rows(BRANCH_WIDTH), rows(BRANCH_WIDTH), rows(D), rows(D),
            full((CONV_K, BRANCH_WIDTH)), full((1, BRANCH_WIDTH)),
            full((BRANCH_WIDTH // LANES, CHUNK, CHUNK)), full((CHUNK, BRANCH_WIDTH // LANES)),
            gate(0), gate(1), gate(2), gate(3),
            pl.BlockSpec((N_BRANCHES, BRANCH_WIDTH, tn), lambda i, j: (0, 0, j)),
            pl.BlockSpec((tn, D), lambda i, j: (j, 0)),
            full((1, D)), full((D, LANES)),
        ],
        out_specs=[rows(D), rows(D), rows(LANES), rows(LANES)],
        out_shape=[
            jax.ShapeDtypeStruct((T, D), F32),
            jax.ShapeDtypeStruct((T, D), BF16),
            jax.ShapeDtypeStruct((T, LANES), I32),
            jax.ShapeDtypeStruct((T, LANES), F32),
        ],
        scratch_shapes=[pltpu.VMEM((tm, N_BRANCHES * BRANCH_WIDTH), BF16), pltpu.VMEM((tm, D), F32)],
        compiler_params=_cparams(("parallel", "arbitrary"), MERGE_VMEM_LIMIT),
        name="merge",
    )(proj, proj, proj, proj, proj, proj, proj, y_c, y_d, hn, h2, conv_w, sgu_norm, sgu_w, sgu_b_t,
      w_gate, w_gate, w_gate, w_gate, w_branch, w_out, ffn_g, router_pad)


def _ffn_dense_kernel(x_ref, w1_ref, w3_ref, w2_ref, res_ref, g_ref, o_ref, *, n_f, final_norm):
    f = pl.program_id(1)

    @pl.when(f == 0)
    def _():
        o_ref[...] = res_ref[...]

    x = x_ref[...]
    h1 = _dot(x, w1_ref[...])
    h3 = _dot(x, w3_ref[...])
    a = (h1 * jax.nn.sigmoid(h1)) * h3
    o_ref[...] += _dot(a.astype(BF16), w2_ref[...])

    if final_norm:
        @pl.when(f == n_f - 1)
        def _():
            o_ref[...] = _rmsnorm_rows(o_ref[...], g_ref[...])


def _ffn_dense_call(xb, w1, w3, w2, res, g, final_norm):
    T, D = res.shape
    d_ff = w1.shape[-1]
    tm = _pick(T, 1024)
    tf = _pick(d_ff, 512)
    n_f = d_ff // tf
    kern = functools.partial(_ffn_dense_kernel, n_f=n_f, final_norm=final_norm)
    rows = lambda w: pl.BlockSpec((tm, w), lambda i, f: (i, 0))
    return pl.pallas_call(
        kern,
        grid=(T // tm, n_f),
        in_specs=[
            rows(D),
            pl.BlockSpec((D, tf), lambda i, f: (0, f)),
            pl.BlockSpec((D, tf), lambda i, f: (0, f)),
            pl.BlockSpec((tf, D), lambda i, f: (f, 0)),
            rows(D),
            pl.BlockSpec((1, D), lambda i, f: (0, 0)),
        ],
        out_specs=rows(D),
        out_shape=jax.ShapeDtypeStruct((T, D), F32),
        compiler_params=_cparams(("parallel", "arbitrary"), MERGE_VMEM_LIMIT),
        name="ffn",
    )(xb, w1, w3, w2, res, g)


def _row_copy(src_hbm, row, dst_vmem, slot, sem):
    return pltpu.make_async_copy(src_hbm.at[pl.ds(row, 1), :], dst_vmem.at[pl.ds(slot, 1), :], sem)


ROW_BLOCK = 256


def _ffn_routed_kernel(te_ref, nu_ref, tok_ref, tokn_ref, h_hbm, g_ref, w1_ref, w3_ref, w2_ref, o_ref,
                       buf, x_scr, sem, *, n_f, tm):
    i = pl.program_id(0)
    f = pl.program_id(1)
    n_used = nu_ref[0]
    live = i < n_used

    def start_rows(tok):
        def body(r, c):
            _row_copy(h_hbm, tok[0, 0, r], buf, r, sem).start()
            return c
        lax.fori_loop(0, tm, body, 0, unroll=8)

    @pl.when(f == 0)
    def _():
        o_ref[...] = jnp.zeros(o_ref.shape, F32)

    @pl.when((f == 0) & (i == 0) & live)
    def _():
        start_rows(tok_ref)

    @pl.when((f == 0) & live)
    def _():
        def wait(r, c):
            _row_copy(h_hbm, tok_ref[0, 0, r], buf, r, sem).wait()
            return c

        lax.fori_loop(0, tm, wait, 0, unroll=8)
        rblk = min(ROW_BLOCK, tm)
        for rb in range(tm // rblk):
            rows = slice(rb * rblk, (rb + 1) * rblk)
            x_scr[rows, :] = _rmsnorm_rows(buf[rows, :], g_ref[...]).astype(BF16)

    @pl.when((f == min(1, n_f - 1)) & (i + 1 < n_used))
    def _():
        start_rows(tokn_ref)

    @pl.when(live)
    def _():
        x = x_scr[...]
        h1 = _dot(x, w1_ref[...].astype(BF16))
        h3 = _dot(x, w3_ref[...].astype(BF16))
        a = (h1 * jax.nn.sigmoid(h1)) * h3
        o_ref[...] += _dot(a.astype(BF16), w2_ref[...].astype(BF16))


def _ffn_routed_call(h2, g, row_token, tile_e, n_used, w1, w3, w2, tm):
    T, D = h2.shape
    R = row_token.shape[0]
    d_ff = w1.shape[-1]
    tf = _pick(d_ff, 512)
    n_f = d_ff // tf
    assert tm % min(ROW_BLOCK, tm) == 0
    n_tiles = R // tm
    tok = row_token.reshape(n_tiles, 1, tm)
    kern = functools.partial(_ffn_routed_kernel, n_f=n_f, tm=tm)

    def wmap(i, f, te, nu):
        return te[i], jnp.where(i < nu[0], f, n_f - 1)

    return pl.pallas_call(
        kern,
        grid_spec=pltpu.PrefetchScalarGridSpec(
            num_scalar_prefetch=2,
            grid=(R // tm, n_f),
            in_specs=[
                pl.BlockSpec((1, 1, tm), lambda i, f, te, nu: (i, 0, 0), memory_space=pltpu.SMEM),
                pl.BlockSpec((1, 1, tm), lambda i, f, te, nu: (jnp.minimum(i + 1, n_tiles - 1), 0, 0),
                             memory_space=pltpu.SMEM),
                pl.BlockSpec(memory_space=pl.ANY),
                pl.BlockSpec((1, D), lambda i, f, te, nu: (0, 0)),
                pl.BlockSpec((None, D, tf), lambda i, f, te, nu: (wmap(i, f, te, nu)[0], 0, wmap(i, f, te, nu)[1])),
                pl.BlockSpec((None, D, tf), lambda i, f, te, nu: (wmap(i, f, te, nu)[0], 0, wmap(i, f, te, nu)[1])),
                pl.BlockSpec((None, tf, D), lambda i, f, te, nu: (wmap(i, f, te, nu)[0], wmap(i, f, te, nu)[1], 0)),
            ],
            out_specs=pl.BlockSpec((tm, D), lambda i, f, te, nu: (i, 0)),
            scratch_shapes=[pltpu.VMEM((tm, D), F32), pltpu.VMEM((tm, D), BF16), pltpu.SemaphoreType.DMA(())],
        ),
        out_shape=jax.ShapeDtypeStruct((R, D), F32),
        compiler_params=_cparams(("arbitrary", "arbitrary"), MERGE_VMEM_LIMIT),
        name="ffn_routed",
    )(tile_e, n_used, tok, tok, h2, g, w1, w3, w2)


def _moe_combine_kernel(pos_ref, posn_ref, h_ref, rw_ref, y_hbm, g_ref, o_ref, buf, sem, *, tc, n_tiles, final_norm):
    i = pl.program_id(0)
    slot = i % 2

    def rows(pos, s, fn):
        def body(r, c):
            fn(_row_copy(y_hbm, pos[0, 0, 2 * r], buf.at[s, 0], r, sem.at[s]))
            fn(_row_copy(y_hbm, pos[0, 0, 2 * r + 1], buf.at[s, 1], r, sem.at[s]))
            return c
        lax.fori_loop(0, tc, body, 0, unroll=8)

    @pl.when(i == 0)
    def _():
        rows(pos_ref, 0, lambda cp: cp.start())

    @pl.when(i + 1 < n_tiles)
    def _():
        rows(posn_ref, 1 - slot, lambda cp: cp.start())

    rows(pos_ref, slot, lambda cp: cp.wait())
    rw = rw_ref[...]
    out = h_ref[...] + (rw[:, 0:1] * buf[slot, 0] + rw[:, 1:2] * buf[slot, 1])
    if final_norm:
        out = _rmsnorm_rows(out, g_ref[...])
    o_ref[...] = out


def _moe_combine_call(h2, rw, ys, pos, g, final_norm):
    T, D = h2.shape
    tc = _pick(T, 256)
    n_tiles = T // tc
    pos3 = pos.reshape(n_tiles, 1, 2 * tc)
    kern = functools.partial(_moe_combine_kernel, tc=tc, n_tiles=n_tiles, final_norm=final_norm)
    return pl.pallas_call(
        kern,
        grid=(n_tiles,),
        in_specs=[
            pl.BlockSpec((1, 1, 2 * tc), lambda i: (i, 0, 0), memory_space=pltpu.SMEM),
            pl.BlockSpec((1, 1, 2 * tc), lambda i: (jnp.minimum(i + 1, n_tiles - 1), 0, 0), memory_space=pltpu.SMEM),
            pl.BlockSpec((tc, D), lambda i: (i, 0)),
            pl.BlockSpec((tc, LANES), lambda i: (i, 0)),
            pl.BlockSpec(memory_space=pl.ANY),
            pl.BlockSpec((1, D), lambda i: (0, 0)),
        ],
        out_specs=pl.BlockSpec((tc, D), lambda i: (i, 0)),
        out_shape=jax.ShapeDtypeStruct((T, D), F32),
        scratch_shapes=[pltpu.VMEM((2, 2, tc, D), F32), pltpu.SemaphoreType.DMA((2,))],
        compiler_params=_cparams(("arbitrary",)),
        name="moe_combine",
    )(pos3, pos3, h2, rw, ys, g)


def _route(ridx, n_experts, tm):
    T = ridx.shape[0]
    flat_e = ridx[:, :2].reshape(-1)
    onehot = (flat_e[:, None] == jnp.arange(n_experts, dtype=I32)[None, :]).astype(I32)
    cum = jnp.cumsum(onehot, axis=0)
    rank = jnp.take_along_axis(cum, flat_e[:, None], axis=1)[:, 0] - 1
    counts = cum[-1]
    padded = ((counts + tm - 1) // tm) * tm
    gend = jnp.cumsum(padded)
    gstart = gend - padded
    pos = gstart[flat_e] + rank
    R = 2 * T + n_experts * tm
    row_token = jnp.zeros((R,), I32).at[pos].set(jnp.arange(2 * T, dtype=I32) // 2)
    tile_e = jnp.searchsorted(gend, jnp.arange(R // tm, dtype=I32) * tm, side="right")
    tile_e = jnp.minimum(tile_e, n_experts - 1).astype(I32)
    n_used = (gend[-1:] // tm).astype(I32)
    return pos.astype(I32), row_token, tile_e, n_used


def _rope_tables(seq):
    pos = jnp.arange(seq, dtype=F32)

    def tab(dim):
        inv = ROPE_THETA ** (-jnp.arange(0, dim, 2, dtype=F32) / dim)
        ang = pos[:, None] * inv[None, :]
        cos = jnp.concatenate([jnp.cos(ang), jnp.cos(ang)], axis=-1)
        sin = jnp.concatenate([-jnp.sin(ang), jnp.sin(ang)], axis=-1)
        return cos, sin

    c64, s64 = tab(IDX_DIM)
    c128, s128 = tab(HEAD_DIM)
    return (jnp.tile(c64, (1, 2)), jnp.tile(s64, (1, 2)), c128, s128)


def kernel(x, attn_norm, w_in, conv_w, sgu_norm, sgu_w, sgu_b, diff_lambda, diff_subln, w_branch, w_out,
           ffn_norm, dense_w1, dense_w3, dense_w2, router, moe_w1, moe_w3, moe_w2, final_norm):
    batch, seq, D = x.shape
    depth = w_in.shape[0]
    n_experts = router.shape[-1]
    T = batch * seq
    rope = _rope_tables(seq)
    h = x.reshape(T, D)
    ik0 = MAIN_WIDTH
    iw0 = ik0 + IDX_DIM
    g0 = iw0 + IDX_HEADS

    for layer in range(depth):
        wl = w_in[layer]
        w_main = wl[:, :MAIN_WIDTH].astype(BF16)
        w_small = jnp.concatenate(
            [wl[:, ik0:iw0], wl[:, ik0:iw0], wl[:, iw0:g0], jnp.zeros((D, LANES - IDX_HEADS), F32)],
            axis=1).astype(BF16)
        w_gate = wl[:, g0:].astype(BF16)
        lam_init = 0.8 - 0.6 * math.exp(-0.3 * layer)

        proj, hn, ikd, iw = _proj_call(h, attn_norm[layer][None, :], w_main, w_small, rope, seq)
        y_c = _diff_call(proj, diff_lambda[layer], diff_subln[layer][:, None], lam_init, batch, seq)
        mask = _dsa_select_call(proj, ikd, iw[:, :IDX_HEADS].T, batch, seq)
        y_d = _dsa_attn_call(proj, mask, batch, seq)

        j = layer // 2
        is_moe = layer % 2 == 1
        router_pad = jnp.zeros((D, LANES), F32)
        if is_moe:
            router_pad = router_pad.at[:, :n_experts].set(router[j])
        h, hn2, ridx, rw = _merge_call(
            proj, y_c, y_d, hn, h, conv_w[layer], sgu_norm[layer][None, :], sgu_w[layer],
            sgu_b[layer].T, w_gate, w_branch[layer].astype(BF16), w_out[layer].astype(BF16),
            ffn_norm[layer][None, :], router_pad.astype(BF16), seq, n_experts)

        last = layer == depth - 1
        if is_moe:
            tm = _pick(T, 1024)
            pos, row_token, tile_e, n_used = _route(ridx, n_experts, tm)
            ys = _ffn_routed_call(h, ffn_norm[layer][None, :], row_token, tile_e, n_used,
                                  moe_w1[j], moe_w3[j], moe_w2[j], tm)
            h = _moe_combine_call(h, rw, ys, pos, final_norm[None, :], last)
        else:
            h = _ffn_dense_call(hn2, dense_w1[j].astype(BF16), dense_w3[j].astype(BF16),
                                dense_w2[j].astype(BF16), h, final_norm[None, :], last)

    if depth == 0:
        raise ValueError("depth must be positive")
    return h.reshape(batch, seq, D)
```

```python
import functools
import math

import jax
import jax.numpy as jnp
from jax import lax
from jax.experimental import pallas as pl
from jax.experimental.pallas import tpu as pltpu

F32 = jnp.float32
BF16 = jnp.bfloat16
I32 = jnp.int32

LANES = 128
SLAB = 8
N_COUNTERS = 4
PACK = 16
INF_KEY16 = 0x7F80
MIN_NORMAL_KEY16 = 0x0080
MIN_NORMAL_F32 = 1.1754943508222875e-38
HEAD_DIM = 128
BRANCH_WIDTH = 512
N_BRANCHES = 4
N_HEADS = 4
IDX_HEADS = 8
IDX_DIM = 64
DIFF_SUB_DIM = 64
CONV_K = 3
CHUNK = 128
TOPK_MAX = 256
ROPE_THETA = 10000.0
EPS = 1e-6
N_MAIN_TILES = 12
MAIN_WIDTH = N_MAIN_TILES * BRANCH_WIDTH
(T_CB, T_CC, T_CH, T_SU, T_SV, T_DQ, T_DK, T_DV, T_AQ, T_AK, T_AV, T_IQ) = range(12)
NEG_BIG = -1e30
INT_MIN = -(2 ** 31)
VMEM_LIMIT = 56 * 1024 * 1024
MERGE_VMEM_LIMIT = 61 * 1024 * 1024


def _pick(n, pref):
    t = min(pref, n)
    while n % t:
        t //= 2
    return t


def _cparams(sem, vmem_limit=VMEM_LIMIT):
    return pltpu.CompilerParams(dimension_semantics=sem, vmem_limit_bytes=vmem_limit)


def _dot(a, b):
    return jnp.dot(a, b, preferred_element_type=F32)


def _dot_nt(a, b):
    return lax.dot_general(a, b, (((1,), (1,)), ((), ())), preferred_element_type=F32)


def _dot_tn(a, b):
    return lax.dot_general(a, b, (((0,), (0,)), ((), ())), preferred_element_type=F32)


def _rmsnorm_rows(x, g):
    ms = jnp.mean(x * x, axis=-1, keepdims=True)
    return (x * lax.rsqrt(ms + EPS)) * g


def _rope64(x, c, s):
    lane = lax.broadcasted_iota(I32, x.shape, 1)
    rot = jnp.where((lane & 32) == 0, pltpu.roll(x, 96, 1), pltpu.roll(x, 32, 1))
    return x * c + rot * s


def _rope128(x, c, s):
    return x * c + pltpu.roll(x, 64, 1) * s


def _gelu_tanh(x):
    c = math.sqrt(2.0 / math.pi)
    return x * (0.5 * (1.0 + jnp.tanh(c * (x + 0.044715 * (x * x * x)))))


def _proj_kernel(x_ref, g_ref, w_ref, ws_ref, c64_ref, s64_ref, c128_ref, s128_ref,
                 proj_ref, hn_ref, ikd_ref, iw_ref, hn_scr):
    j = pl.program_id(1)

    @pl.when(j == 0)
    def _():
        hb = _rmsnorm_rows(x_ref[...], g_ref[...]).astype(BF16)
        hn_scr[...] = hb
        hn_ref[...] = hb
        small = _dot(hb, ws_ref[...])
        ikd_ref[...] = _rope64(small[:, :LANES], c64_ref[...], s64_ref[...]).astype(BF16)
        iw_ref[...] = small[:, LANES:]

    acc = _dot(hn_scr[...], w_ref[...])
    is64 = (j == T_DQ) | (j == T_DK) | (j == T_IQ)
    is128 = (j == T_AQ) | (j == T_AK)

    @pl.when(jnp.logical_not(is64 | is128))
    def _():
        proj_ref[...] = acc.astype(BF16)

    @pl.when(is64)
    def _():
        scale = jnp.where(j == T_DQ, DIFF_SUB_DIM ** -0.5, 1.0).astype(F32)
        c, s = c64_ref[...], s64_ref[...]
        for cb in range(BRANCH_WIDTH // LANES):
            sl = slice(cb * LANES, (cb + 1) * LANES)
            proj_ref[:, sl] = (_rope64(acc[:, sl], c, s) * scale).astype(BF16)

    @pl.when(is128)
    def _():
        c, s = c128_ref[...], s128_ref[...]
        for cb in range(BRANCH_WIDTH // LANES):
            sl = slice(cb * LANES, (cb + 1) * LANES)
            proj_ref[:, sl] = _rope128(acc[:, sl], c, s).astype(BF16)


def _proj_call(h2, g, w_main, w_small, rope, seq):
    T, D = h2.shape
    tm = _pick(seq, 1024)
    nrb = seq // tm
    rope_spec = pl.BlockSpec((tm, LANES), lambda i, j: (i % nrb, 0))
    row_spec = lambda w: pl.BlockSpec((tm, w), lambda i, j: (i, 0))
    return pl.pallas_call(
        _proj_kernel,
        grid=(T // tm, N_MAIN_TILES),
        in_specs=[
            row_spec(D),
            pl.BlockSpec((1, D), lambda i, j: (0, 0)),
            pl.BlockSpec((D, BRANCH_WIDTH), lambda i, j: (0, j)),
            pl.BlockSpec((D, 2 * LANES), lambda i, j: (0, 0)),
            rope_spec, rope_spec, rope_spec, rope_spec,
        ],
        out_specs=[
            pl.BlockSpec((tm, BRANCH_WIDTH), lambda i, j: (i, j)),
            row_spec(D), row_spec(LANES), row_spec(LANES),
        ],
        out_shape=[
            jax.ShapeDtypeStruct((T, MAIN_WIDTH), BF16),
            jax.ShapeDtypeStruct((T, D), BF16),
            jax.ShapeDtypeStruct((T, LANES), BF16),
            jax.ShapeDtypeStruct((T, LANES), F32),
        ],
        scratch_shapes=[pltpu.VMEM((tm, D), BF16)],
        compiler_params=_cparams(("parallel", "arbitrary")),
        name="proj",
    )(h2, g, w_main, w_small, *rope)


def _softmax_step_t(parts, m_scr, l_scr, acc_scr):
    m_prev = m_scr[...]
    m_new = m_prev
    for s_t, _ in parts:
        m_new = jnp.maximum(m_new, jnp.max(s_t, axis=0, keepdims=True))
    alpha = jnp.exp(m_prev - m_new)
    l_new = alpha * l_scr[...]
    acc = alpha * acc_scr[...]
    for s_t, v in parts:
        p_t = jnp.exp(s_t - m_new)
        l_new = l_new + jnp.sum(p_t, axis=0, keepdims=True)
        acc = acc + _dot_tn(v, p_t.astype(BF16))
    l_scr[...] = l_new
    acc_scr[...] = acc
    m_scr[...] = m_new


def _init_softmax_state(m_scr, l_scr, acc_scr):
    m_scr[...] = jnp.full(m_scr.shape, NEG_BIG, F32)
    l_scr[...] = jnp.zeros(l_scr.shape, F32)
    acc_scr[...] = jnp.zeros(acc_scr.shape, F32)


def _pipelined_tiles(n, qk, sm):
    qk(0, 0)

    def pair(p, carry):
        qk(2 * p + 1, 1)
        sm(2 * p, 0, False)
        qk(2 * p + 2, 0)
        sm(2 * p + 1, 1, False)
        return carry

    n_pairs = (n - 1) // 2
    lax.fori_loop(0, n_pairs, pair, 0)
    rest = n - 2 * n_pairs

    @pl.when(rest == 1)
    def _():
        sm(n - 1, 0, True)

    @pl.when(rest == 2)
    def _():
        qk(n - 1, 1)
        sm(n - 2, 0, False)
        sm(n - 1, 1, True)


def _diff_kernel(q_ref, k_ref, v_ref, dl_ref, subln_ref, o_ref,
                 qq_scr, s_scr, m_scr, l_scr, acc_scr, *, tq, lam_init):
    qi = pl.program_id(2)
    q = q_ref[...]
    lane = lax.broadcasted_iota(I32, q.shape, 1)
    zero = jnp.zeros_like(q)
    qq_scr[0:tq, :] = jnp.where(lane < DIFF_SUB_DIM, q, zero)
    qq_scr[tq:2 * tq, :] = jnp.where(lane >= DIFF_SUB_DIM, q, zero)
    _init_softmax_state(m_scr, l_scr, acc_scr)

    def qk(kj, slot):
        off = pl.multiple_of(kj * tq, tq)
        s_scr[slot] = _dot_nt(k_ref[pl.ds(off, tq), :], qq_scr[...])

    def sm(kj, slot, causal):
        off = pl.multiple_of(kj * tq, tq)
        s_t = s_scr[slot]
        if causal:
            kpos = lax.broadcasted_iota(I32, s_t.shape, 0)
            qpos = lax.broadcasted_iota(I32, s_t.shape, 1)
            qpos = jnp.where(qpos >= tq, qpos - tq, qpos)
            s_t = jnp.where(kpos <= qpos, s_t, NEG_BIG)
        _softmax_step_t([(s_t, v_ref[pl.ds(off, tq), :])], m_scr, l_scr, acc_scr)

    _pipelined_tiles(qi + 1, qk, sm)

    dl = dl_ref[...]
    lam = (jnp.exp(jnp.sum(dl[0:1] * dl[1:2], axis=-1, keepdims=True))
           - jnp.exp(jnp.sum(dl[2:3] * dl[3:4], axis=-1, keepdims=True)) + lam_init)
    o_all = acc_scr[...] / l_scr[...]
    o_t = o_all[:, 0:tq] - lam * o_all[:, tq:2 * tq]
    ms = jnp.mean(o_t * o_t, axis=0, keepdims=True)
    o_t = (o_t * lax.rsqrt(ms + EPS)) * subln_ref[...] * (1.0 - lam_init)
    o_ref[...] = o_t.T.astype(BF16)


def _diff_call(proj, diff_lambda, subln_col, lam_init, batch, seq):
    T = proj.shape[0]
    tq = _pick(seq, 512)
    nq = seq // tq
    hpt = BRANCH_WIDTH // LANES
    kern = functools.partial(_diff_kernel, tq=tq, lam_init=lam_init)
    return pl.pallas_call(
        kern,
        grid=(batch, N_HEADS, nq),
        in_specs=[
            pl.BlockSpec((tq, LANES), lambda b, h, i: (b * nq + i, T_DQ * hpt + h)),
            pl.BlockSpec((seq, LANES), lambda b, h, i: (b, T_DK * hpt + h)),
            pl.BlockSpec((seq, LANES), lambda b, h, i: (b, T_DV * hpt + h)),
            pl.BlockSpec((4, DIFF_SUB_DIM), lambda b, h, i: (0, 0)),
            pl.BlockSpec((HEAD_DIM, 1), lambda b, h, i: (0, 0)),
        ],
        out_specs=pl.BlockSpec((tq, LANES), lambda b, h, i: (b * nq + i, h)),
        out_shape=jax.ShapeDtypeStruct((T, BRANCH_WIDTH), BF16),
        scratch_shapes=[
            pltpu.VMEM((2 * tq, LANES), BF16),
            pltpu.VMEM((2, tq, 2 * tq), F32),
            pltpu.VMEM((1, 2 * tq), F32),
            pltpu.VMEM((1, 2 * tq), F32),
            pltpu.VMEM((HEAD_DIM, 2 * tq), F32),
        ],
        compiler_params=_cparams(("parallel", "parallel", "arbitrary")),
        name="diff_attn",
    )(proj, proj, proj, diff_lambda, subln_col)


def _dsa_select_kernel(iq_ref, ikd_ref, iwt_ref, mask_ref, key_scr, hi_scr, *, tq, tc, nc, topk, idx_bits):
    qi = pl.program_id(1)
    n_ch = (qi * tq + tq + tc - 1) // tc
    iq = iq_ref[...]
    iwt = iwt_ref[...]
    lane = lax.broadcasted_iota(I32, (tq, LANES), 1)
    zero = jnp.zeros((tq, LANES), BF16)
    qm = []
    for h in range(IDX_HEADS):
        blk = iq[:, (h // 2) * LANES:(h // 2 + 1) * LANES]
        keep = (lane < IDX_DIM) if h % 2 == 0 else (lane >= IDX_DIM)
        qm.append(jnp.where(keep, blk, zero))
    idx_scale = IDX_DIM ** -0.5 * IDX_HEADS ** -0.5
    qpos = qi * tq + lax.broadcasted_iota(I32, (tc, tq), 1)
    krow = lax.broadcasted_iota(I32, (tc, tq), 0)
    krow8 = lax.broadcasted_iota(I32, (SLAB, tq), 0)

    def score_chunk(c, carry):
        ik = ikd_ref[pl.ds(pl.multiple_of(c * tc, tc), tc), :]
        sc = jnp.zeros((tc, tq), F32)
        for h in range(IDX_HEADS):
            sc = sc + iwt[h:h + 1, :] * jnp.maximum(_dot_nt(ik, qm[h]), 0.0)
        sc = sc * idx_scale
        sc = jnp.where(jnp.abs(sc) < MIN_NORMAL_F32, 0.0, sc)
        sc = jnp.where(c * tc + krow <= qpos, sc, -jnp.inf)
        bits = pltpu.bitcast(sc, I32)
        key_scr[c] = jnp.where(bits < 0, INT_MIN - bits, bits)
        hi_scr[c] = sc.astype(BF16)
        return carry

    lax.fori_loop(0, n_ch, score_chunk, 0)

    def count(pred):
        def body(c, accs):
            accs = list(accs)
            for r in range(tc // SLAB):
                ks = key_scr[c, r * SLAB:(r + 1) * SLAB, :]
                accs[r % N_COUNTERS] = accs[r % N_COUNTERS] + jnp.where(pred(ks, c * tc + r * SLAB), 1, 0)
            return tuple(accs)
        zero = jnp.zeros((SLAB, tq), I32)
        accs = lax.fori_loop(0, n_ch, body, (zero,) * N_COUNTERS)
        return jnp.sum(functools.reduce(lambda a, b: a + b, accs), axis=0, keepdims=True)

    def rows8(x):
        return jnp.broadcast_to(x, (SLAB, tq))

    def count_ge(cand):
        c8 = rows8(cand)
        return count(lambda ks, base: ks >= c8)

    def count16(cand_val):
        c16 = jnp.broadcast_to(cand_val, (PACK, tq)).astype(BF16)
        one = jnp.ones((PACK, tq), BF16)
        zero16 = jnp.zeros((PACK, tq), BF16)

        def body(c, accs):
            accs = list(accs)
            for r in range(tc // PACK):
                hs = hi_scr[c, r * PACK:(r + 1) * PACK, :]
                accs[r % N_COUNTERS] = accs[r % N_COUNTERS] + jnp.where(hs >= c16, one, zero16)
            return tuple(accs)
        accs = lax.fori_loop(0, n_ch, body, (zero16,) * N_COUNTERS)
        tot = functools.reduce(lambda a, b: a + b, [a.astype(F32) for a in accs])
        return jnp.sum(tot, axis=0, keepdims=True)

    def key16_value(k16):
        k16 = jnp.clip(k16, -INF_KEY16, INF_KEY16)
        k16 = jnp.where((k16 > 0) & (k16 < MIN_NORMAL_KEY16), MIN_NORMAL_KEY16, k16)
        bits16 = jnp.where(k16 >= 0, k16, -32768 - k16)
        return pltpu.bitcast(jnp.left_shift(bits16, 16), F32)

    nonneg = count16(jnp.zeros((1, tq), F32)) >= topk
    thr16 = jnp.where(nonneg, 0, -32768).astype(I32)

    def bit16_body(it, thr16):
        cand = thr16 + jnp.left_shift(jnp.int32(1), 14 - it)
        return jnp.where(count16(key16_value(cand)) >= topk, cand, thr16)

    thr16 = lax.fori_loop(0, 15, bit16_body, thr16)

    thr = jnp.left_shift(thr16, 16) - 2 ** 15

    def bit_body(it, thr):
        cand = thr + jnp.left_shift(jnp.int32(1), 16 - it)
        return jnp.where(count_ge(cand) >= topk, cand, thr)

    thr = lax.fori_loop(0, 17, bit_body, thr)
    thr8 = rows8(thr)
    cnt_ge = count_ge(thr)
    excess = jnp.max(cnt_ge - topk) > 0

    def tie_search(_):
        need = topk - count(lambda ks, base: ks > thr8)

        def tbit(it, p):
            cand = p + jnp.left_shift(jnp.int32(1), idx_bits - 1 - it)
            cand8 = rows8(cand)
            cnt = count(lambda ks, base: (ks == thr8) & (base + krow8 < cand8))
            return jnp.where(cnt < need, cand, p)
        return lax.fori_loop(0, idx_bits, tbit, jnp.zeros((1, tq), I32))

    last = lax.cond(excess, tie_search, lambda _: jnp.full((1, tq), 2 ** idx_bits, I32), 0)

    def emit(c, carry):
        kc = key_scr[c]
        kpos = c * tc + krow
        sel = (kc > thr) | ((kc == thr) & (kpos <= last))
        sel = sel & (kpos <= qpos)
        mask_ref[0, 0, c] = jnp.where(sel, 1, 0).astype(jnp.int8)
        return carry

    lax.fori_loop(0, n_ch, emit, 0)

    def clear(c, carry):
        mask_ref[0, 0, c] = jnp.zeros((tc, tq), jnp.int8)
        return carry

    lax.fori_loop(n_ch, nc, clear, 0)


def _dsa_tiles(seq):
    tq = _pick(seq, 256)
    tc = _pick(seq, 512)
    return tq, tc


def _dsa_select_call(proj, ikd, iw_t, batch, seq):
    tq, tc = _dsa_tiles(seq)
    nq, nc = seq // tq, seq // tc
    topk = min(TOPK_MAX, seq // 4)
    assert tc >= topk
    assert (tc // PACK // N_COUNTERS) * nc <= 256
    idx_bits = max(1, (seq - 1).bit_length())
    kern = functools.partial(_dsa_select_kernel, tq=tq, tc=tc, nc=nc, topk=topk, idx_bits=idx_bits)
    return pl.pallas_call(
        kern,
        grid=(batch, nq),
        in_specs=[
            pl.BlockSpec((tq, BRANCH_WIDTH), lambda b, i: (b * nq + i, T_IQ)),
            pl.BlockSpec((seq, LANES), lambda b, i: (b, 0)),
            pl.BlockSpec((IDX_HEADS, tq), lambda b, i: (0, b * nq + i)),
        ],
        out_specs=pl.BlockSpec((1, 1, nc, tc, tq), lambda b, i: (b, i, 0, 0, 0)),
        out_shape=jax.ShapeDtypeStruct((batch, nq, nc, tc, tq), jnp.int8),
        scratch_shapes=[pltpu.VMEM((nc, tc, tq), I32), pltpu.VMEM((nc, tc, tq), BF16)],
        compiler_params=_cparams(("parallel", "arbitrary")),
        name="dsa_select",
    )(proj, ikd, iw_t)


def _dsa_attn_kernel(q_ref, k_ref, v_ref, mask_ref, o_ref, s_scr, m_scr, l_scr, acc_scr, *, tq, tc, cpi):
    qi = pl.program_id(2)
    n_ch = (qi * tq + tq + tc - 1) // tc
    q = q_ref[...]
    scale = HEAD_DIM ** -0.5
    _init_softmax_state(m_scr, l_scr, acc_scr)

    def qk(it, slot):
        for u in range(cpi):
            c = it * cpi + u
            off = pl.multiple_of(c * tc, tc)
            s_t = _dot_nt(k_ref[pl.ds(off, tc), :], q) * scale
            sel = mask_ref[0, 0, c].astype(I32) != 0
            s_scr[slot, u] = jnp.where(sel, s_t, NEG_BIG)

    def sm(it, slot, last):
        parts = []
        for u in range(cpi):
            off = pl.multiple_of((it * cpi + u) * tc, tc)
            parts.append((s_scr[slot, u], v_ref[pl.ds(off, tc), :]))
        _softmax_step_t(parts, m_scr, l_scr, acc_scr)

    _pipelined_tiles((n_ch + cpi - 1) // cpi, qk, sm)
    o_ref[...] = (acc_scr[...] / l_scr[...]).T.astype(BF16)


def _dsa_attn_call(proj, mask, batch, seq):
    T = proj.shape[0]
    tq, tc = _dsa_tiles(seq)
    nq, nc = seq // tq, seq // tc
    hpt = BRANCH_WIDTH // LANES
    cpi = 2 if nc % 2 == 0 else 1
    kern = functools.partial(_dsa_attn_kernel, tq=tq, tc=tc, cpi=cpi)
    return pl.pallas_call(
        kern,
        grid=(batch, N_HEADS, nq),
        in_specs=[
            pl.BlockSpec((tq, LANES), lambda b, h, i: (b * nq + i, T_AQ * hpt + h)),
            pl.BlockSpec((seq, LANES), lambda b, h, i: (b, T_AK * hpt + h)),
            pl.BlockSpec((seq, LANES), lambda b, h, i: (b, T_AV * hpt + h)),
            pl.BlockSpec((1, 1, nc, tc, tq), lambda b, h, i: (b, i, 0, 0, 0)),
        ],
        out_specs=pl.BlockSpec((tq, LANES), lambda b, h, i: (b * nq + i, h)),
        out_shape=jax.ShapeDtypeStruct((T, BRANCH_WIDTH), BF16),
        scratch_shapes=[
            pltpu.VMEM((2, cpi, tc, tq), F32),
            pltpu.VMEM((1, tq), F32),
            pltpu.VMEM((1, tq), F32),
            pltpu.VMEM((HEAD_DIM, tq), F32),
        ],
        compiler_params=_cparams(("parallel", "parallel", "arbitrary")),
        name="dsa_attn",
    )(proj, proj, proj, mask)


def _merge_kernel(cb_ref, cc_ref, ch_ref, ccp_ref, chp_ref, su_ref, sv_ref, yc_ref, yd_ref,
                  hn_ref, res_ref, convw_ref, sgun_ref, sguw_ref, sgub_ref,
                  wg0_ref, wg1_ref, wg2_ref, wg3_ref, wb_ref, wo_ref, fng_ref, router_ref,
                  h_ref, hn2_ref, ridx_ref, rw_ref, ycat_scr, acc_scr, *, tm, rows_per_seq, n_j, n_experts):
    i = pl.program_id(0)
    j = pl.program_id(1)

    @pl.when(j == 0)
    def _():
        acc_scr[...] = jnp.zeros(acc_scr.shape, F32)
        z = cc_ref[...].astype(F32) * ch_ref[...].astype(F32)
        first = ((i * tm) % rows_per_seq) == 0
        zp = ccp_ref[...].astype(F32) * chp_ref[...].astype(F32)
        zp = zp * jnp.where(first, 0.0, 1.0).astype(F32)
        row = lax.broadcasted_iota(I32, z.shape, 0)
        z1 = jnp.where(row == 0, zp[7:8, :], pltpu.roll(z, 1, 0))
        z2 = jnp.where(row == 0, zp[6:7, :], jnp.where(row == 1, zp[7:8, :], pltpu.roll(z, 2, 0)))
        cw = convw_ref[...]
        conv = cw[0:1, :] * z2 + cw[1:2, :] * z1 + cw[2:3, :] * z
        ycat_scr[:, 0:BRANCH_WIDTH] = (cb_ref[...].astype(F32) * conv).astype(BF16)
        u = _gelu_tanh(su_ref[...].astype(F32))
        v = _rmsnorm_rows(_gelu_tanh(sv_ref[...].astype(F32)), sgun_ref[...]).astype(BF16)
        tri_r = lax.broadcasted_iota(I32, (CHUNK, CHUNK), 0)
        tri_c = lax.broadcasted_iota(I32, (CHUNK, CHUNK), 1)
        bias = sgub_ref[...]
        for g in range(BRANCH_WIDTH // LANES):
            wg = jnp.where(tri_c <= tri_r, sguw_ref[g], 0.0).astype(BF16)
            cols = slice(g * LANES, (g + 1) * LANES)
            for ck in range(tm // CHUNK):
                rows = slice(ck * CHUNK, (ck + 1) * CHUNK)
                y = _dot(wg, v[rows, cols]) + bias[:, g:g + 1]
                ycat_scr[rows, BRANCH_WIDTH + g * LANES:BRANCH_WIDTH + (g + 1) * LANES] = (
                    u[rows, cols] * y).astype(BF16)
        ycat_scr[:, 2 * BRANCH_WIDTH:3 * BRANCH_WIDTH] = yc_ref[...]
        ycat_scr[:, 3 * BRANCH_WIDTH:4 * BRANCH_WIDTH] = yd_ref[...]

    hn = hn_ref[...]
    merged = None
    for n, wg_ref in enumerate((wg0_ref, wg1_ref, wg2_ref, wg3_ref)):
        gate = jax.nn.sigmoid(_dot(hn, wg_ref[...]))
        br = _dot(ycat_scr[:, n * BRANCH_WIDTH:(n + 1) * BRANCH_WIDTH], wb_ref[n])
        merged = gate * br if merged is None else merged + gate * br
    acc_scr[...] += _dot(merged.astype(BF16), wo_ref[...])

    @pl.when(j == n_j - 1)
    def _():
        h_new = res_ref[...] + acc_scr[...]
        h_ref[...] = h_new
        hn2 = _rmsnorm_rows(h_new, fng_ref[...])
        hb = hn2.astype(BF16)
        hn2_ref[...] = hb
        logits = _dot(hb, router_ref[...])
        lane = lax.broadcasted_iota(I32, logits.shape, 1)
        logits = jnp.where(lane < n_experts, logits, -jnp.inf)
        v1 = jnp.max(logits, axis=-1, keepdims=True)
        i1 = jnp.min(jnp.where(logits == v1, lane, LANES), axis=-1, keepdims=True)
        rest = jnp.where(lane == i1, -jnp.inf, logits)
        v2 = jnp.max(rest, axis=-1, keepdims=True)
        i2 = jnp.min(jnp.where(rest == v2, lane, LANES), axis=-1, keepdims=True)
        e2 = jnp.exp(v2 - v1)
        w1 = 1.0 / (1.0 + e2)
        w2 = e2 / (1.0 + e2)
        ridx_ref[...] = jnp.where(lane == 0, i1, jnp.where(lane == 1, i2, 0))
        rw_ref[...] = jnp.where(lane == 0, w1, jnp.where(lane == 1, w2, 0.0))


def _merge_call(proj, y_c, y_d, hn, h2, conv_w, sgu_norm, sgu_w, sgu_b_t, w_gate, w_branch, w_out,
                ffn_g, router_pad, seq, n_experts):
    T, D = h2.shape
    tm = _pick(seq, 512)
    tn = _pick(D, 256)
    n_j = D // tn
    rb8 = tm // 8
    kern = functools.partial(_merge_kernel, tm=tm, rows_per_seq=seq, n_j=n_j, n_experts=n_experts)
    tile = lambda t: pl.BlockSpec((tm, BRANCH_WIDTH), lambda i, j, t=t: (i, t))
    prev = lambda t: pl.BlockSpec((8, BRANCH_WIDTH), lambda i, j, t=t: (jnp.maximum(i * rb8 - 1, 0), t))
    full = lambda shp: pl.BlockSpec(shp, lambda i, j: (0,) * len(shp))
    gate = lambda n: pl.BlockSpec((D, tn), lambda i, j, n=n: (0, n * n_j + j))
    rows = lambda w: pl.BlockSpec((tm, w), lambda i, j: (i, 0))
    return pl.pallas_call(
        kern,
        grid=(T // tm, n_j),
        in_specs=[
            tile(T_CB), tile(T_CC), tile(T_CH), prev(T_CC), prev(T_CH), tile(T_SU), tile(T_SV),
            rows(BRANCH_WIDTH), rows(BRANCH_WIDTH), rows(D), rows(D),
            full((CONV_K, BRANCH_WIDTH)), full((1, BRANCH_WIDTH)),
            full((BRANCH_WIDTH // LANES, CHUNK, CHUNK)), full((CHUNK, BRANCH_WIDTH // LANES)),
            gate(0), gate(1), gate(2), gate(3),
            pl.BlockSpec((N_BRANCHES, BRANCH_WIDTH, tn), lambda i, j: (0, 0, j)),
            pl.BlockSpec((tn, D), lambda i, j: (j, 0)),
            full((1, D)), full((D, LANES)),
        ],
        out_specs=[rows(D), rows(D), rows(LANES), rows(LANES)],
        out_shape=[
            jax.ShapeDtypeStruct((T, D), F32),
            jax.ShapeDtypeStruct((T, D), BF16),
            jax.ShapeDtypeStruct((T, LANES), I32),
            jax.ShapeDtypeStruct((T, LANES), F32),
        ],
        scratch_shapes=[pltpu.VMEM((tm, N_BRANCHES * BRANCH_WIDTH), BF16), pltpu.VMEM((tm, D), F32)],
        compiler_params=_cparams(("parallel", "arbitrary"), MERGE_VMEM_LIMIT),
        name="merge",
    )(proj, proj, proj, proj, proj, proj, proj, y_c, y_d, hn, h2, conv_w, sgu_norm, sgu_w, sgu_b_t,
      w_gate, w_gate, w_gate, w_gate, w_branch, w_out, ffn_g, router_pad)


def _ffn_dense_kernel(x_ref, w1_ref, w3_ref, w2_ref, res_ref, g_ref, o_ref, *, n_f, final_norm):
    f = pl.program_id(1)

    @pl.when(f == 0)
    def _():
        o_ref[...] = res_ref[...]

    x = x_ref[...]
    h1 = _dot(x, w1_ref[...])
    h3 = _dot(x, w3_ref[...])
    a = (h1 * jax.nn.sigmoid(h1)) * h3
    o_ref[...] += _dot(a.astype(BF16), w2_ref[...])

    if final_norm:
        @pl.when(f == n_f - 1)
        def _():
            o_ref[...] = _rmsnorm_rows(o_ref[...], g_ref[...])


def _ffn_dense_call(xb, w1, w3, w2, res, g, final_norm):
    T, D = res.shape
    d_ff = w1.shape[-1]
    tm = _pick(T, 1024)
    tf = _pick(d_ff, 512)
    n_f = d_ff // tf
    kern = functools.partial(_ffn_dense_kernel, n_f=n_f, final_norm=final_norm)
    rows = lambda w: pl.BlockSpec((tm, w), lambda i, f: (i, 0))
    return pl.pallas_call(
        kern,
        grid=(T // tm, n_f),
        in_specs=[
            rows(D),
            pl.BlockSpec((D, tf), lambda i, f: (0, f)),
            pl.BlockSpec((D, tf), lambda i, f: (0, f)),
            pl.BlockSpec((tf, D), lambda i, f: (f, 0)),
            rows(D),
            pl.BlockSpec((1, D), lambda i, f: (0, 0)),
        ],
        out_specs=rows(D),
        out_shape=jax.ShapeDtypeStruct((T, D), F32),
        compiler_params=_cparams(("parallel", "arbitrary"), MERGE_VMEM_LIMIT),
        name="ffn",
    )(xb, w1, w3, w2, res, g)


def _row_copy(src_hbm, row, dst_vmem, slot, sem):
    return pltpu.make_async_copy(src_hbm.at[pl.ds(row, 1), :], dst_vmem.at[pl.ds(slot, 1), :], sem)


ROW_BLOCK = 256


def _ffn_routed_kernel(te_ref, nu_ref, tok_ref, tokn_ref, h_hbm, g_ref, w1_ref, w3_ref, w2_ref, o_ref,
                       buf, x_scr, sem, *, n_f, tm):
    i = pl.program_id(0)
    f = pl.program_id(1)
    n_used = nu_ref[0]
    live = i < n_used

    def start_rows(tok):
        def body(r, c):
            _row_copy(h_hbm, tok[0, 0, r], buf, r, sem).start()
            return c
        lax.fori_loop(0, tm, body, 0, unroll=8)

    @pl.when(f == 0)
    def _():
        o_ref[...] = jnp.zeros(o_ref.shape, F32)

    @pl.when((f == 0) & (i == 0) & live)
    def _():
        start_rows(tok_ref)

    @pl.when((f == 0) & live)
    def _():
        def wait(r, c):
            _row_copy(h_hbm, tok_ref[0, 0, r], buf, r, sem).wait()
            return c

        lax.fori_loop(0, tm, wait, 0, unroll=8)
        rblk = min(ROW_BLOCK, tm)
        for rb in range(tm // rblk):
            rows = slice(rb * rblk, (rb + 1) * rblk)
            x_scr[rows, :] = _rmsnorm_rows(buf[rows, :], g_ref[...]).astype(BF16)

    @pl.when((f == min(1, n_f - 1)) & (i + 1 < n_used))
    def _():
        start_rows(tokn_ref)

    @pl.when(live)
    def _():
        x = x_scr[...]
        h1 = _dot(x, w1_ref[...].astype(BF16))
        h3 = _dot(x, w3_ref[...].astype(BF16))
        a = (h1 * jax.nn.sigmoid(h1)) * h3
        o_ref[...] += _dot(a.astype(BF16), w2_ref[...].astype(BF16))


def _ffn_routed_call(h2, g, row_token, tile_e, n_used, w1, w3, w2, tm):
    T, D = h2.shape
    R = row_token.shape[0]
    d_ff = w1.shape[-1]
    tf = _pick(d_ff, 512)
    n_f = d_ff // tf
    assert tm % min(ROW_BLOCK, tm) == 0
    n_tiles = R // tm
    tok = row_token.reshape(n_tiles, 1, tm)
    kern = functools.partial(_ffn_routed_kernel, n_f=n_f, tm=tm)

    def wmap(i, f, te, nu):
        return te[i], jnp.where(i < nu[0], f, n_f - 1)

    return pl.pallas_call(
        kern,
        grid_spec=pltpu.PrefetchScalarGridSpec(
            num_scalar_prefetch=2,
            grid=(R // tm, n_f),
            in_specs=[
                pl.BlockSpec((1, 1, tm), lambda i, f, te, nu: (i, 0, 0), memory_space=pltpu.SMEM),
                pl.BlockSpec((1, 1, tm), lambda i, f, te, nu: (jnp.minimum(i + 1, n_tiles - 1), 0, 0),
                             memory_space=pltpu.SMEM),
                pl.BlockSpec(memory_space=pl.ANY),
                pl.BlockSpec((1, D), lambda i, f, te, nu: (0, 0)),
                pl.BlockSpec((None, D, tf), lambda i, f, te, nu: (wmap(i, f, te, nu)[0], 0, wmap(i, f, te, nu)[1])),
                pl.BlockSpec((None, D, tf), lambda i, f, te, nu: (wmap(i, f, te, nu)[0], 0, wmap(i, f, te, nu)[1])),
                pl.BlockSpec((None, tf, D), lambda i, f, te, nu: (wmap(i, f, te, nu)[0], wmap(i, f, te, nu)[1], 0)),
            ],
            out_specs=pl.BlockSpec((tm, D), lambda i, f, te, nu: (i, 0)),
            scratch_shapes=[pltpu.VMEM((tm, D), F32), pltpu.VMEM((tm, D), BF16), pltpu.SemaphoreType.DMA(())],
        ),
        out_shape=jax.ShapeDtypeStruct((R, D), F32),
        compiler_params=_cparams(("arbitrary", "arbitrary"), MERGE_VMEM_LIMIT),
        name="ffn_routed",
    )(tile_e, n_used, tok, tok, h2, g, w1, w3, w2)


def _moe_combine_kernel(pos_ref, posn_ref, h_ref, rw_ref, y_hbm, g_ref, o_ref, buf, sem, *, tc, n_tiles, final_norm):
    i = pl.program_id(0)
    slot = i % 2

    def rows(pos, s, fn):
        def body(r, c):
            fn(_row_copy(y_hbm, pos[0, 0, 2 * r], buf.at[s, 0], r, sem.at[s]), 0)
            fn(_row_copy(y_hbm, pos[0, 0, 2 * r + 1], buf.at[s, 1], r, sem.at[s]), 1)
            return c
        lax.fori_loop(0, tc, body, 0, unroll=8)

    def start(cp, k):
        cp.start(priority=k)

    @pl.when(i == 0)
    def _():
        rows(pos_ref, 0, start)

    @pl.when(i + 1 < n_tiles)
    def _():
        rows(posn_ref, 1 - slot, start)

    rows(pos_ref, slot, lambda cp, k: cp.wait())
    rw = rw_ref[...]
    out = h_ref[...] + (rw[:, 0:1] * buf[slot, 0] + rw[:, 1:2] * buf[slot, 1])
    if final_norm:
        out = _rmsnorm_rows(out, g_ref[...])
    o_ref[...] = out


def _moe_combine_call(h2, rw, ys, pos, g, final_norm):
    T, D = h2.shape
    tc = _pick(T, 256)
    n_tiles = T // tc
    pos3 = pos.reshape(n_tiles, 1, 2 * tc)
    kern = functools.partial(_moe_combine_kernel, tc=tc, n_tiles=n_tiles, final_norm=final_norm)
    return pl.pallas_call(
        kern,
        grid=(n_tiles,),
        in_specs=[
            pl.BlockSpec((1, 1, 2 * tc), lambda i: (i, 0, 0), memory_space=pltpu.SMEM),
            pl.BlockSpec((1, 1, 2 * tc), lambda i: (jnp.minimum(i + 1, n_tiles - 1), 0, 0), memory_space=pltpu.SMEM),
            pl.BlockSpec((tc, D), lambda i: (i, 0)),
            pl.BlockSpec((tc, LANES), lambda i: (i, 0)),
            pl.BlockSpec(memory_space=pl.ANY),
            pl.BlockSpec((1, D), lambda i: (0, 0)),
        ],
        out_specs=pl.BlockSpec((tc, D), lambda i: (i, 0)),
        out_shape=jax.ShapeDtypeStruct((T, D), F32),
        scratch_shapes=[pltpu.VMEM((2, 2, tc, D), F32), pltpu.SemaphoreType.DMA((2,))],
        compiler_params=_cparams(("arbitrary",)),
        name="moe_combine",
    )(pos3, pos3, h2, rw, ys, g)


def _route(ridx, n_experts, tm):
    T = ridx.shape[0]
    flat_e = ridx[:, :2].reshape(-1)
    onehot = (flat_e[:, None] == jnp.arange(n_experts, dtype=I32)[None, :]).astype(I32)
    cum = jnp.cumsum(onehot, axis=0)
    rank = jnp.take_along_axis(cum, flat_e[:, None], axis=1)[:, 0] - 1
    counts = cum[-1]
    padded = ((counts + tm - 1) // tm) * tm
    gend = jnp.cumsum(padded)
    gstart = gend - padded
    pos = gstart[flat_e] + rank
    R = 2 * T + n_experts * tm
    row_token = jnp.zeros((R,), I32).at[pos].set(jnp.arange(2 * T, dtype=I32) // 2)
    tile_e = jnp.searchsorted(gend, jnp.arange(R // tm, dtype=I32) * tm, side="right")
    tile_e = jnp.minimum(tile_e, n_experts - 1).astype(I32)
    n_used = (gend[-1:] // tm).astype(I32)
    return pos.astype(I32), row_token, tile_e, n_used


def _rope_tables(seq):
    pos = jnp.arange(seq, dtype=F32)

    def tab(dim):
        inv = ROPE_THETA ** (-jnp.arange(0, dim, 2, dtype=F32) / dim)
        ang = pos[:, None] * inv[None, :]
        cos = jnp.concatenate([jnp.cos(ang), jnp.cos(ang)], axis=-1)
        sin = jnp.concatenate([-jnp.sin(ang), jnp.sin(ang)], axis=-1)
        return cos, sin

    c64, s64 = tab(IDX_DIM)
    c128, s128 = tab(HEAD_DIM)
    return (jnp.tile(c64, (1, 2)), jnp.tile(s64, (1, 2)), c128, s128)


def kernel(x, attn_norm, w_in, conv_w, sgu_norm, sgu_w, sgu_b, diff_lambda, diff_subln, w_branch, w_out,
           ffn_norm, dense_w1, dense_w3, dense_w2, router, moe_w1, moe_w3, moe_w2, final_norm):
    batch, seq, D = x.shape
    depth = w_in.shape[0]
    n_experts = router.shape[-1]
    T = batch * seq
    rope = _rope_tables(seq)
    h = x.reshape(T, D)
    ik0 = MAIN_WIDTH
    iw0 = ik0 + IDX_DIM
    g0 = iw0 + IDX_HEADS

    for layer in range(depth):
        wl = w_in[layer]
        w_main = wl[:, :MAIN_WIDTH].astype(BF16)
        w_small = jnp.concatenate(
            [wl[:, ik0:iw0], wl[:, ik0:iw0], wl[:, iw0:g0], jnp.zeros((D, LANES - IDX_HEADS), F32)],
            axis=1).astype(BF16)
        w_gate = wl[:, g0:].astype(BF16)
        lam_init = 0.8 - 0.6 * math.exp(-0.3 * layer)

        proj, hn, ikd, iw = _proj_call(h, attn_norm[layer][None, :], w_main, w_small, rope, seq)
        y_c = _diff_call(proj, diff_lambda[layer], diff_subln[layer][:, None], lam_init, batch, seq)
        mask = _dsa_select_call(proj, ikd, iw[:, :IDX_HEADS].T, batch, seq)
        y_d = _dsa_attn_call(proj, mask, batch, seq)

        j = layer // 2
        is_moe = layer % 2 == 1
        router_pad = jnp.zeros((D, LANES), F32)
        if is_moe:
            router_pad = router_pad.at[:, :n_experts].set(router[j])
        h, hn2, ridx, rw = _merge_call(
            proj, y_c, y_d, hn, h, conv_w[layer], sgu_norm[layer][None, :], sgu_w[layer],
            sgu_b[layer].T, w_gate, w_branch[layer].astype(BF16), w_out[layer].astype(BF16),
            ffn_norm[layer][None, :], router_pad.astype(BF16), seq, n_experts)

        last = layer == depth - 1
        if is_moe:
            tm = _pick(T, 1024)
            pos, row_token, tile_e, n_used = _route(ridx, n_experts, tm)
            ys = _ffn_routed_call(h, ffn_norm[layer][None, :], row_token, tile_e, n_used,
                                  moe_w1[j], moe_w3[j], moe_w2[j], tm)
            h = _moe_combine_call(h, rw, ys, pos, final_norm[None, :], last)
        else:
            h = _ffn_dense_call(hn2, dense_w1[j].astype(BF16), dense_w3[j].astype(BF16),
                                dense_w2[j].astype(BF16), h, final_norm[None, :], last)

    if depth == 0:
        raise ValueError("depth must be positive")
    return h.reshape(batch, seq, D)
```
